```python
import math
import jax, jax.numpy as jnp
from jax import lax
import numpy as np

D_MODEL = 1024
BATCH = 16
SEQ = 256
DEPTH = 1
DEC_BATCH = 2
DEC_SEQ = 1024
PAST_LEN = 256

GRID_W = 64
D_INNER = 2 * D_MODEL
HEAD_DIM = 64
N_HEADS = D_INNER // HEAD_DIM
N_GROUPS = 8
HEADS_PER_GROUP = N_HEADS // N_GROUPS
D_STATE = 128
SSM_CONV = 5
CHUNK = 128
D_XBC = D_INNER + 2 * N_GROUPS * D_STATE
D_CONF = D_MODEL
CONF_KERNEL = 31
N_EXPERTS = 32
TOP_K = 4
D_EXPERT = D_MODEL
SWIGLU_LIMIT = 7.0
SWIGLU_ALPHA = 1.702
D_IN_PROJ = D_INNER + D_XBC + 2 * N_HEADS + 2 * D_CONF + 2 * D_MODEL
EPS = 1e-6

kernel_name = "hybrid_ssd_conformer_moe_diffusion_step"


def _rmsnorm(x, w):
    xf = x.astype(jnp.float32)
    xf = xf * lax.rsqrt(jnp.mean(xf * xf, axis=-1, keepdims=True) + EPS)
    return xf.astype(x.dtype) * w


def _layernorm(x, w, b):
    xf = x.astype(jnp.float32)
    mu = jnp.mean(xf, axis=-1, keepdims=True)
    xc = xf - mu
    var = jnp.mean(xc * xc, axis=-1, keepdims=True)
    return (xc * lax.rsqrt(var + EPS)).astype(x.dtype) * w + b


def _dwconv(x, w, b):
    k = w.shape[0]
    y = lax.conv_general_dilated(
        x, w[:, None, :].astype(x.dtype), window_strides=(1,), padding=[(k // 2, k // 2)],
        dimension_numbers=("NWC", "WIO", "NWC"), feature_group_count=x.shape[-1])
    return y + b


def _segsum(a):
    t = a.shape[-1]
    a_rep = jnp.broadcast_to(a[..., :, None], a.shape + (t,))
    a_rep = jnp.where(jnp.tril(jnp.ones((t, t), bool), -1), a_rep, 0.0)
    ss = jnp.cumsum(a_rep, axis=-2)
    return jnp.where(jnp.tril(jnp.ones((t, t), bool)), ss, -jnp.inf)


def _ssd_scan(x, dt, a, bm, cm, h0):
    bsz, seq = x.shape[:2]
    nc = seq // CHUNK
    g, r = N_GROUPS, HEADS_PER_GROUP
    xs = (x * dt[..., None]).reshape(bsz, nc, CHUNK, g, r, HEAD_DIM)
    bs = bm.reshape(bsz, nc, CHUNK, g, D_STATE)
    cs = cm.reshape(bsz, nc, CHUNK, g, D_STATE)
    adt = (dt * a).reshape(bsz, nc, CHUNK, g, r).transpose(0, 3, 4, 1, 2)
    a_cum = jnp.cumsum(adt, axis=-1)
    lmat = jnp.exp(_segsum(adt))
    scores = jnp.einsum("bclgn,bcsgn->bgcls", cs, bs)
    y_diag = jnp.einsum("bgrcls,bcsgrp->bclgrp", scores[:, :, None] * lmat, xs)
    decay_states = jnp.exp(a_cum[..., -1:] - a_cum)
    states = jnp.einsum("bcsgn,bgrcs,bcsgrp->bcgrpn", bs, decay_states, xs)
    h0r = h0.reshape(bsz, g, r, HEAD_DIM, D_STATE)
    states = jnp.concatenate([h0r[:, None], states], axis=1)
    chunk_tot = jnp.pad(a_cum[..., -1], ((0, 0), (0, 0), (0, 0), (1, 0)))
    decay_chunk = jnp.exp(_segsum(chunk_tot))
    new_states = jnp.einsum("bgrzc,bcgrpn->bzgrpn", decay_chunk, states)
    prev_states, h_final = new_states[:, :-1], new_states[:, -1]
    y_off = jnp.einsum("bclgn,bcgrpn,bgrcl->bclgrp", cs, prev_states, jnp.exp(a_cum))
    y = (y_diag + y_off).reshape(bsz, seq, N_HEADS, HEAD_DIM)
    return y, h_final.reshape(bsz, N_HEADS, HEAD_DIM, D_STATE)


def _mixer(h, h0, grid_rows, lp):
    bsz, seq, _ = h.shape
    f32 = jnp.float32
    proj = h @ lp["w_in"]
    z, xbc, dt_raw, glu_in, gate_raw = jnp.split(
        proj, (D_INNER, D_INNER + D_XBC, D_INNER + D_XBC + 2 * N_HEADS,
               D_INNER + D_XBC + 2 * N_HEADS + 2 * D_CONF), axis=-1)
    xbc = jax.nn.silu(_dwconv(xbc, lp["ssm_conv_w"], lp["ssm_conv_b"]))
    xs, bm, cm = jnp.split(xbc, (D_INNER, D_INNER + N_GROUPS * D_STATE), axis=-1)
    xs = xs.reshape(bsz, seq, N_HEADS, HEAD_DIM).astype(f32)
    bm = bm.reshape(bsz, seq, N_GROUPS, D_STATE).astype(f32)
    cm = cm.reshape(bsz, seq, N_GROUPS, D_STATE).astype(f32)
    dt = jax.nn.softplus(dt_raw.reshape(bsz, seq, 2, N_HEADS).astype(f32) + lp["dt_bias"].astype(f32))
    a = -jnp.exp(lp["a_log"].astype(f32))
    y_f, h_f = _ssd_scan(xs, dt[:, :, 0], a[0], bm, cm, h0[:, 0])
    y_b, h_b = _ssd_scan(jnp.flip(xs, 1), jnp.flip(dt[:, :, 1], 1), a[1],
                         jnp.flip(bm, 1), jnp.flip(cm, 1), h0[:, 1])
    y = y_f + jnp.flip(y_b, 1) + xs * lp["d_skip"].astype(f32)[:, None]
    y = y.reshape(bsz, seq, D_INNER).astype(h.dtype) * jax.nn.silu(z)
    y = _rmsnorm(y.reshape(bsz, seq, N_GROUPS, D_INNER // N_GROUPS),
                 lp["ssm_norm_w"].reshape(N_GROUPS, D_INNER // N_GROUPS)).reshape(bsz, seq, D_INNER)
    o_ssd = y @ lp["w_ssd_out"]
    a_glu, b_glu = jnp.split(glu_in, 2, axis=-1)
    u = a_glu * jax.nn.sigmoid(b_glu)
    if grid_rows is None:
        u = _dwconv(u, lp["conf_dw_w"], lp["conf_dw_b"])
    else:
        u = _dwconv(u.reshape(bsz * grid_rows, GRID_W, D_CONF), lp["conf_dw_w"],
                    lp["conf_dw_b"]).reshape(bsz, seq, D_CONF)
    u = jax.nn.silu(_layernorm(u, lp["conf_ln_w"], lp["conf_ln_b"]))
    o_conf = u @ lp["w_conf_out"] + lp["b_conf_out"]
    g_ssd, g_conf = jnp.split(jax.nn.sigmoid(gate_raw + lp["b_gate"]), 2, axis=-1)
    out = (g_ssd * o_ssd + g_conf * o_conf) @ lp["w_o"]
    return out, jnp.stack([h_f, h_b], axis=1)


def _moe(h, lp):
    bsz, seq, d = h.shape
    t = h.reshape(bsz * seq, d)
    logits = (t @ lp["w_router"] + lp["b_router"]).astype(jnp.float32)
    top_vals, top_idx = lax.top_k(logits, TOP_K)
    probs = jax.nn.softmax(top_vals, axis=-1)
    combine = jnp.sum(jax.nn.one_hot(top_idx, N_EXPERTS, dtype=jnp.float32) * probs[..., None], axis=1)

    def expert_step(acc, e):
        w_gu_e, b_gu_e, w_down_e, b_down_e, w_tok = e
        g, u = jnp.split(t @ w_gu_e + b_gu_e, 2, axis=-1)
        g = jnp.minimum(g, SWIGLU_LIMIT)
        u = jnp.clip(u, -SWIGLU_LIMIT, SWIGLU_LIMIT)
        act = (u + 1.0) * g * jax.nn.sigmoid(SWIGLU_ALPHA * g)
        out = act @ w_down_e + b_down_e
        return acc + w_tok[:, None].astype(out.dtype) * out, None

    acc, _ = lax.scan(expert_step, jnp.zeros_like(t),
                      (lp["w_gu"], lp["b_gu"], lp["w_down"], lp["b_down"], combine.T))
    return acc.reshape(bsz, seq, d)


def _block(x, mod, h0, grid_rows, lp):
    sh1, sc1, g1, sh2, sc2, g2 = jnp.split(mod, 6, axis=-1)
    h = _rmsnorm(x, lp["norm_mix"]) * (1.0 + sc1) + sh1
    o, h_fin = _mixer(h, h0, grid_rows, lp)
    x = x + g1 * o
    h = _rmsnorm(x, lp["norm_ffn"]) * (1.0 + sc2) + sh2
    x = x + g2 * _moe(h, lp)
    return x, h_fin


def setup_inputs(seed: int = 0) -> dict:
    key = jax.random.key(seed)
    ks = jax.random.split(key, 40)
    f32 = jnp.float32

    def nrm(k, shape, scale):
        return jax.random.normal(k, shape, f32) * scale

    dt0 = jnp.exp(jax.random.uniform(ks[10], (DEPTH, 2, N_HEADS), f32,
                                     minval=math.log(1e-3), maxval=math.log(1e-1)))
    return {
        "x_prompt": nrm(ks[0], (BATCH, SEQ, D_MODEL), 1.0),
        "x_sample": nrm(ks[1], (DEC_BATCH, DEC_SEQ, D_MODEL), 1.0),
        "state_ssm": nrm(ks[2], (DEC_BATCH, DEPTH, 2, N_HEADS, HEAD_DIM, D_STATE), 0.1),
        "c": nrm(ks[3], (DEC_BATCH, D_MODEL), 1.0),
        "c_ctx": nrm(ks[4], (D_MODEL,), 1.0),
        "w_ada": nrm(ks[5], (DEPTH, D_MODEL, 6 * D_MODEL), 0.5 * D_MODEL ** -0.5),
        "b_ada": nrm(ks[6], (DEPTH, 6 * D_MODEL), 0.02),
        "norm_mix": 1.0 + nrm(ks[7], (DEPTH, D_MODEL), 0.02),
        "norm_ffn": 1.0 + nrm(ks[8], (DEPTH, D_MODEL), 0.02),
        "w_in": nrm(ks[9], (DEPTH, D_MODEL, D_IN_PROJ), D_MODEL ** -0.5),
        "ssm_conv_w": nrm(ks[11], (DEPTH, SSM_CONV, D_XBC), SSM_CONV ** -0.5),
        "ssm_conv_b": nrm(ks[12], (DEPTH, D_XBC), 0.02),
        "dt_bias": dt0 + jnp.log(-jnp.expm1(-dt0)),
        "a_log": jnp.log(jax.random.uniform(ks[13], (DEPTH, 2, N_HEADS), f32, minval=1.0, maxval=16.0)),
        "d_skip": 1.0 + nrm(ks[14], (DEPTH, N_HEADS), 0.1),
        "ssm_norm_w": 1.0 + nrm(ks[15], (DEPTH, D_INNER), 0.02),
        "w_ssd_out": nrm(ks[16], (DEPTH, D_INNER, D_MODEL), D_INNER ** -0.5),
        "conf_dw_w": nrm(ks[17], (DEPTH, CONF_KERNEL, D_CONF), CONF_KERNEL ** -0.5),
        "conf_dw_b": nrm(ks[18], (DEPTH, D_CONF), 0.02),
        "conf_ln_w": 1.0 + nrm(ks[19], (DEPTH, D_CONF), 0.02),
        "conf_ln_b": nrm(ks[20], (DEPTH, D_CONF), 0.02),
        "w_conf_out": nrm(ks[21], (DEPTH, D_CONF, D_MODEL), D_CONF ** -0.5),
        "b_conf_out": nrm(ks[22], (DEPTH, D_MODEL), 0.02),
        "b_gate": nrm(ks[23], (DEPTH, 2 * D_MODEL), 0.02),
        "w_o": nrm(ks[24], (DEPTH, D_MODEL, D_MODEL), D_MODEL ** -0.5),
        "w_router": nrm(ks[25], (DEPTH, D_MODEL, N_EXPERTS), D_MODEL ** -0.5),
        "b_router": nrm(ks[26], (DEPTH, N_EXPERTS), 0.01),
        "w_gu": nrm(ks[27], (DEPTH, N_EXPERTS, D_MODEL, 2 * D_EXPERT), D_MODEL ** -0.5),
        "b_gu": nrm(ks[28], (DEPTH, N_EXPERTS, 2 * D_EXPERT), 0.02),
        "w_down": nrm(ks[29], (DEPTH, N_EXPERTS, D_EXPERT, D_MODEL), D_EXPERT ** -0.5),
        "b_down": nrm(ks[30], (DEPTH, N_EXPERTS, D_MODEL), 0.02),
        "norm_final": 1.0 + nrm(ks[31], (D_MODEL,), 0.02),
    }


def reference(x_prompt, x_sample, state_ssm, c, c_ctx, w_ada, b_ada, norm_mix, norm_ffn, w_in,
              ssm_conv_w, ssm_conv_b, dt_bias, a_log, d_skip, ssm_norm_w, w_ssd_out,
              conf_dw_w, conf_dw_b, conf_ln_w, conf_ln_b, w_conf_out, b_conf_out, b_gate, w_o,
              w_router, b_router, w_gu, b_gu, w_down, b_down, norm_final):
    n_ctx = x_prompt.shape[0]
    grid_rows = x_sample.shape[1] // GRID_W
    xc, xl = x_prompt, x_sample
    new_states = []
    for l in range(DEPTH):
        lp = {
            "norm_mix": norm_mix[l], "norm_ffn": norm_ffn[l], "w_in": w_in[l],
            "ssm_conv_w": ssm_conv_w[l], "ssm_conv_b": ssm_conv_b[l], "dt_bias": dt_bias[l],
            "a_log": a_log[l], "d_skip": d_skip[l], "ssm_norm_w": ssm_norm_w[l],
            "w_ssd_out": w_ssd_out[l], "conf_dw_w": conf_dw_w[l], "conf_dw_b": conf_dw_b[l],
            "conf_ln_w": conf_ln_w[l], "conf_ln_b": conf_ln_b[l], "w_conf_out": w_conf_out[l],
            "b_conf_out": b_conf_out[l], "b_gate": b_gate[l], "w_o": w_o[l],
            "w_router": w_router[l], "b_router": b_router[l], "w_gu": w_gu[l], "b_gu": b_gu[l],
            "w_down": w_down[l], "b_down": b_down[l],
        }
        mod_ctx = (jax.nn.silu(c_ctx) @ w_ada[l] + b_ada[l])[None, None, :]
        mod_lat = (jax.nn.silu(c) @ w_ada[l] + b_ada[l])[:, None, :]
        h0_ctx = jnp.zeros((n_ctx, 2, N_HEADS, HEAD_DIM, D_STATE), jnp.float32)
        xc, st_ctx = _block(xc, mod_ctx, h0_ctx, None, lp)
        new_states.append(st_ctx)
        xl, _ = _block(xl, mod_lat, state_ssm[:, l].astype(jnp.float32), grid_rows, lp)
    y_prompt = _rmsnorm(xc, norm_final)
    y_sample = _rmsnorm(xl, norm_final)
    new_state_ssm = jnp.stack(new_states, axis=1)
    return (y_prompt, y_sample, new_state_ssm)
```

```python
import functools

import jax
import jax.numpy as jnp
from jax import lax
from jax.experimental import pallas as pl
from jax.experimental.pallas import tpu as pltpu

F32 = jnp.float32
BF16 = jnp.bfloat16
U32 = jnp.uint32
HIGHEST = lax.Precision.HIGHEST

D_MODEL = 1024
GRID_W = 64
D_INNER = 2 * D_MODEL
HEAD_DIM = 64
N_HEADS = D_INNER // HEAD_DIM
N_GROUPS = 8
HEADS_PER_GROUP = N_HEADS // N_GROUPS
D_STATE = 128
SSM_CONV = 5
CHUNK = 128
D_XBC = D_INNER + 2 * N_GROUPS * D_STATE
D_CONF = D_MODEL
CONF_KERNEL = 31
N_EXPERTS = 32
TOP_K = 4
D_EXPERT = D_MODEL
SWIGLU_LIMIT = 7.0
SWIGLU_ALPHA = 1.702
EPS = 1e-6

LANES = 128
UNIT = 256
GROUP_W = D_INNER // N_GROUPS
MOE_TILE = 256
VMEM_LIMIT = 56 * 1024 * 1024

COL_XBC = 0
COL_Z = D_XBC
COL_GLU = D_XBC + D_INNER
COL_GATE = D_XBC + D_INNER + 2 * D_CONF
N_MAIN = D_XBC + D_INNER + 2 * D_CONF + 2 * D_MODEL


def _sigmoid(x):
    return 1.0 / (1.0 + jnp.exp(-x))


def _silu(x):
    return x * _sigmoid(x)


def _cparams(n_axes, vmem=None):
    return pltpu.CompilerParams(
        dimension_semantics=("arbitrary",) * n_axes,
        vmem_limit_bytes=vmem)


def _ada_kernel(c_ref, w_ref, b_ref, o_ref):
    s = _silu(c_ref[...])
    o_ref[...] = jnp.dot(s.astype(BF16), w_ref[...].astype(BF16),
                         preferred_element_type=F32) + b_ref[...]


def _ada_mod(cvec, w_ada, b_ada):
    n = w_ada.shape[1]
    tn = 1536
    return pl.pallas_call(
        _ada_kernel,
        grid=(n // tn,),
        in_specs=[pl.BlockSpec((8, D_MODEL), lambda j: (0, 0)),
                  pl.BlockSpec((D_MODEL, tn), lambda j: (0, j)),
                  pl.BlockSpec((1, tn), lambda j: (0, j))],
        out_specs=pl.BlockSpec((8, tn), lambda j: (0, j)),
        out_shape=jax.ShapeDtypeStruct((8, n), F32),
        compiler_params=_cparams(1, 40 * 1024 * 1024),
        name="ada_mod",
    )(cvec, w_ada, b_ada.reshape(1, n))


def _inproj_kernel(x_ref, nw_ref, sc_ref, sh_ref, w_ref, wdt_ref, o_ref, dt_ref, h_scr):
    @pl.when(pl.program_id(1) == 0)
    def _():
        xf = x_ref[...]
        ms = jnp.mean(xf * xf, axis=-1, keepdims=True)
        hn = (xf * lax.rsqrt(ms + EPS)) * nw_ref[...]
        hn = hn * (1.0 + sc_ref[...]) + sh_ref[...]
        hb = hn.astype(BF16)
        h_scr[...] = hb
        dt_ref[...] = jnp.dot(hb, wdt_ref[...], preferred_element_type=F32)

    o_ref[...] = jnp.dot(h_scr[...], w_ref[...], preferred_element_type=F32).astype(BF16)


def _in_proj(x_all, norm_w, modr, w_main, w_dt):
    t = x_all.shape[0]
    tm, tn = 512, 1024
    upt = tm // UNIT
    return pl.pallas_call(
        _inproj_kernel,
        grid=(t // tm, N_MAIN // tn),
        in_specs=[pl.BlockSpec((tm, D_MODEL), lambda m, n: (m, 0)),
                  pl.BlockSpec((1, D_MODEL), lambda m, n: (0, 0)),
                  pl.BlockSpec((None, None, 1, D_MODEL), lambda m, n: (m * upt, 1, 0, 0)),
                  pl.BlockSpec((None, None, 1, D_MODEL), lambda m, n: (m * upt, 0, 0, 0)),
                  pl.BlockSpec((D_MODEL, tn), lambda m, n: (0, n)),
                  pl.BlockSpec((D_MODEL, 2 * LANES), lambda m, n: (0, 0))],
        out_specs=[pl.BlockSpec((tm, tn), lambda m, n: (m, n)),
                   pl.BlockSpec((tm, 2 * LANES), lambda m, n: (m, 0))],
        out_shape=[jax.ShapeDtypeStruct((t, N_MAIN), BF16),
                   jax.ShapeDtypeStruct((t, 2 * LANES), F32)],
        scratch_shapes=[pltpu.VMEM((tm, D_MODEL), BF16)],
        compiler_params=_cparams(2),
        name="in_proj",
    )(x_all, norm_w.reshape(1, D_MODEL), modr, modr, w_main, w_dt)


def _ssm_conv_kernel(x_ref, w_ref, b_ref, o_ref, pad_scr, *, seq):
    cb = x_ref.shape[1]
    pad_scr[0:8, :] = jnp.zeros((8, cb), F32)
    pad_scr[8 + seq:16 + seq, :] = jnp.zeros((8, cb), F32)
    pad_scr[8:8 + seq, :] = x_ref[...].astype(F32)
    half = SSM_CONV // 2
    rows = 256
    for r0 in range(0, seq, rows):
        acc = jnp.broadcast_to(b_ref[...], (rows, cb))
        for k in range(SSM_CONV):
            acc = acc + w_ref[k:k + 1, :] * pad_scr[8 + r0 + k - half:8 + r0 + k - half + rows, :]
        o_ref[r0:r0 + rows, :] = _silu(acc).astype(BF16)


def _ssm_conv(proj, conv_w, conv_b, *, seq, nseq, row0):
    cb = 512
    blk0 = row0 // seq
    return pl.pallas_call(
        functools.partial(_ssm_conv_kernel, seq=seq),
        grid=(nseq, D_XBC // cb),
        in_specs=[pl.BlockSpec((seq, cb), lambda s, j: (blk0 + s, j)),
                  pl.BlockSpec((SSM_CONV, cb), lambda s, j: (0, j)),
                  pl.BlockSpec((1, cb), lambda s, j: (0, j))],
        out_specs=pl.BlockSpec((seq, cb), lambda s, j: (s, j)),
        out_shape=jax.ShapeDtypeStruct((nseq * seq, D_XBC), BF16),
        scratch_shapes=[pltpu.VMEM((seq + 16, cb), F32)],
        compiler_params=_cparams(2),
        name="ssm_conv",
    )(proj, conv_w, conv_b.reshape(1, D_XBC))


def _expand_heads(v, g, lane):
    r = v.shape[0]
    cols = []
    for half in range(GROUP_W // LANES):
        he = HEADS_PER_GROUP * g + 2 * half
        be = jnp.broadcast_to(v[:, he:he + 1], (r, LANES))
        bo = jnp.broadcast_to(v[:, he + 1:he + 2], (r, LANES))
        cols.append(jnp.where(lane[:r] < HEAD_DIM, be, bo))
    return jnp.concatenate(cols, axis=1)


def _ssd_kernel(*refs, nc, has_h0, write_state):
    (xs_ref, b_ref, c_ref, z_ref, dtr_ref, dtb_ref, alog_ref, dskip_ref, nw_ref), rest = refs[:9], refs[9:]
    if has_h0:
        h0_ref, rest = rest[0], rest[1:]
    y_ref, rest = rest[0], rest[1:]
    if write_state:
        st_ref, rest = rest[0], rest[1:]
    h_scr, ybuf, ychunk = rest

    q = CHUNK
    phase = pl.program_id(1)
    c = pl.program_id(2)
    is_fwd = phase == 1
    c_eff = jnp.where(is_fwd, c, nc - 1 - c)

    @pl.when(c == 0)
    def _():
        if has_h0:
            h_scr[...] = h0_ref[...]
        else:
            h_scr[...] = jnp.zeros(h_scr.shape, F32)

    xdt = dtr_ref[...] + dtb_ref[...]
    dt = jnp.maximum(xdt, 0.0) + jnp.log(1.0 + jnp.exp(-jnp.abs(xdt)))
    a = -jnp.exp(alog_ref[...])
    adt = dt * a
    row = lax.broadcasted_iota(jnp.int32, (q, q), 0)
    col = lax.broadcasted_iota(jnp.int32, (q, q), 1)
    tri = (col - row) * jnp.where(is_fwd, 1, -1) <= 0
    cum = jnp.dot(tri.astype(F32), adt, precision=HIGHEST, preferred_element_type=F32)
    tot = jnp.where(is_fwd, cum[q - 1:q, :], cum[0:1, :])
    rowq = (cum - jnp.log(dt)).T
    wdec = dt * jnp.exp(tot - cum)
    eo = jnp.exp(cum)
    etot = jnp.exp(tot)
    lane = lax.broadcasted_iota(jnp.int32, (q, LANES), 1)
    lane_g = lax.broadcasted_iota(jnp.int32, (q, GROUP_W), 1)
    neg_inf = jnp.float32(-jnp.inf)

    for g in range(N_GROUPS):
        gs = slice(g * GROUP_W, (g + 1) * GROUP_W)
        ns = slice(g * D_STATE, (g + 1) * D_STATE)
        bg = b_ref[:, ns]
        cg = c_ref[:, ns]
        scores = lax.dot_general(cg, bg, (((1,), (1,)), ((), ())), preferred_element_type=F32)
        xs_f = xs_ref[:, gs].astype(F32)
        ms, xb = [], []
        for hh in range(HEADS_PER_GROUP):
            h = HEADS_PER_GROUP * g + hh
            colb = jnp.broadcast_to(cum[:, h:h + 1], (q, q))
            seg = jnp.where(tri, colb - rowq[h:h + 1, :], neg_inf)
            ms.append((scores * jnp.exp(seg)).astype(BF16))
            xb.append(jnp.where(lane_g // HEAD_DIM == hh, xs_f, 0.0).astype(BF16))
        y_diag = jnp.dot(jnp.concatenate(ms, axis=1), jnp.concatenate(xb, axis=0),
                         preferred_element_type=F32)
        xd = (xs_f * _expand_heads(wdec, g, lane)).astype(BF16)
        bt = bg.astype(F32).T.astype(BF16)
        st = jnp.dot(bt, xd, preferred_element_type=F32)
        hg = h_scr[:, gs]
        y_off = jnp.dot(cg, hg.astype(BF16), preferred_element_type=F32) * _expand_heads(eo, g, lane)
        ychunk[:, gs] = y_diag + y_off
        h_scr[:, gs] = hg * _expand_heads(etot, g, lane) + st

    rows = pl.ds(pl.multiple_of(c_eff * q, q), q)

    @pl.when(jnp.logical_not(is_fwd))
    def _():
        ybuf[rows, :] = ychunk[...]

    @pl.when(is_fwd)
    def _():
        zf = z_ref[...].astype(F32)
        yt = ychunk[...] + ybuf[rows, :] + xs_ref[...].astype(F32) * dskip_ref[...]
        yz = yt * _silu(zf)
        for g in range(N_GROUPS):
            gs = slice(g * GROUP_W, (g + 1) * GROUP_W)
            blk = yz[:, gs]
            ms = jnp.mean(blk * blk, axis=-1, keepdims=True)
            y_ref[:, gs] = (blk * lax.rsqrt(ms + EPS) * nw_ref[:, gs]).astype(BF16)

    if write_state:
        @pl.when(c == nc - 1)
        def _():
            for j in range(D_INNER // LANES):
                st_ref[j * LANES:(j + 1) * LANES, :] = h_scr[:, j * LANES:(j + 1) * LANES].T


def _ssd(xbc_c, proj, dt_raw, dt_bias, a_log, d_skip, norm_w, h0t, *, seq, nseq, row0, write_state):
    nc = seq // CHUNK
    blk0 = row0 // CHUNK
    has_h0 = h0t is not None

    def tok(s, p, c):
        return s * nc + p * c + (1 - p) * (nc - 1 - c)

    in_specs = [
        pl.BlockSpec((CHUNK, D_INNER), lambda s, p, c: (tok(s, p, c), 0)),
        pl.BlockSpec((CHUNK, N_GROUPS * D_STATE), lambda s, p, c: (tok(s, p, c), 2)),
        pl.BlockSpec((CHUNK, N_GROUPS * D_STATE), lambda s, p, c: (tok(s, p, c), 3)),
        pl.BlockSpec((CHUNK, D_INNER), lambda s, p, c: (blk0 + s * nc + p * c, COL_Z // D_INNER)),
        pl.BlockSpec((CHUNK, LANES), lambda s, p, c: (blk0 + tok(s, p, c), 1 - p)),
        pl.BlockSpec((None, 1, LANES), lambda s, p, c: (1 - p, 0, 0)),
        pl.BlockSpec((None, 1, LANES), lambda s, p, c: (1 - p, 0, 0)),
        pl.BlockSpec((1, D_INNER), lambda s, p, c: (0, 0)),
        pl.BlockSpec((1, D_INNER), lambda s, p, c: (0, 0)),
    ]
    args = [xbc_c, xbc_c, xbc_c, proj, dt_raw, dt_bias, a_log, d_skip, norm_w]
    if has_h0:
        in_specs.append(pl.BlockSpec((None, None, D_STATE, D_INNER), lambda s, p, c: (s, 1 - p, 0, 0)))
        args.append(h0t)
    out_specs = [pl.BlockSpec((CHUNK, D_INNER), lambda s, p, c: (s * nc + p * c, 0))]
    out_shape = [jax.ShapeDtypeStruct((nseq * seq, D_INNER), BF16)]
    if write_state:
        out_specs.append(pl.BlockSpec((None, None, D_INNER, D_STATE), lambda s, p, c: (s, 1 - p, 0, 0)))
        out_shape.append(jax.ShapeDtypeStruct((nseq, 2, D_INNER, D_STATE), F32))
    return pl.pallas_call(
        functools.partial(_ssd_kernel, nc=nc, has_h0=has_h0, write_state=write_state),
        grid=(nseq, 2, nc),
        in_specs=in_specs,
        out_specs=out_specs,
        out_shape=out_shape,
        scratch_shapes=[pltpu.VMEM((D_STATE, D_INNER), F32),
                        pltpu.VMEM((seq, D_INNER), F32),
                        pltpu.VMEM((CHUNK, D_INNER), F32)],
        compiler_params=_cparams(3, 48 * 1024 * 1024),
        name="ssd_scan",
    )(*args)


def _conf_kernel(glu_ref, w_ref, b_ref, lnw_ref, lnb_ref, o_ref, pad_scr, conv_scr, *, seg):
    rows = glu_ref.shape[0]
    nseg = rows // seg
    half = CONF_KERNEL // 2
    front = 16
    a = glu_ref[:, :D_CONF].astype(F32)
    b = glu_ref[:, D_CONF:].astype(F32)
    u = a * _sigmoid(b)
    for i in range(nseg):
        pad_scr[i, 0:front, :] = jnp.zeros((front, D_CONF), F32)
        pad_scr[i, front + seg:front + seg + 16, :] = jnp.zeros((16, D_CONF), F32)
        pad_scr[i, front:front + seg, :] = u[i * seg:(i + 1) * seg, :]
    rb = 64
    for i in range(nseg):
        for cbi in range(D_CONF // LANES):
            cs = slice(cbi * LANES, (cbi + 1) * LANES)
            for r0 in range(0, seg, rb):
                acc = jnp.broadcast_to(b_ref[:, cs], (rb, LANES))
                for k in range(CONF_KERNEL):
                    start = front + r0 + k - half
                    acc = acc + w_ref[k:k + 1, cs] * pad_scr[i, start:start + rb, cs]
                conv_scr[i * seg + r0:i * seg + r0 + rb, cs] = acc
    v = conv_scr[...]
    mu = jnp.mean(v, axis=-1, keepdims=True)
    vc = v - mu
    var = jnp.mean(vc * vc, axis=-1, keepdims=True)
    ln = (vc * lax.rsqrt(var + EPS)) * lnw_ref[...] + lnb_ref[...]
    o_ref[...] = _silu(ln).astype(BF16)


def _conformer(proj, dw_w, dw_b, ln_w, ln_b, *, seg, ntok, row0):
    rows = UNIT
    blk0 = row0 // rows
    return pl.pallas_call(
        functools.partial(_conf_kernel, seg=seg),
        grid=(ntok // rows,),
        in_specs=[pl.BlockSpec((rows, 2 * D_CONF), lambda i: (blk0 + i, COL_GLU // (2 * D_CONF))),
                  pl.BlockSpec((CONF_KERNEL, D_CONF), lambda i: (0, 0)),
                  pl.BlockSpec((1, D_CONF), lambda i: (0, 0)),
                  pl.BlockSpec((1, D_CONF), lambda i: (0, 0)),
                  pl.BlockSpec((1, D_CONF), lambda i: (0, 0))],
        out_specs=pl.BlockSpec((rows, D_CONF), lambda i: (i, 0)),
        out_shape=jax.ShapeDtypeStruct((ntok, D_CONF), BF16),
        scratch_shapes=[pltpu.VMEM((rows // seg, seg + 32, D_CONF), F32),
                        pltpu.VMEM((rows, D_CONF), F32)],
        compiler_params=_cparams(1),
        name="conformer_conv",
    )(proj, dw_w, dw_b.reshape(1, D_CONF), ln_w.reshape(1, D_CONF), ln_b.reshape(1, D_CONF))


def _pack_halves(x):
    outs = []
    for cb in range(x.shape[1] // (2 * LANES)):
        hi = x[:, cb * 2 * LANES:cb * 2 * LANES + LANES].astype(BF16).astype(F32)
        lo = x[:, cb * 2 * LANES + LANES:(cb + 1) * 2 * LANES].astype(BF16).astype(F32)
        hw = lax.bitcast_convert_type(hi, U32)
        lw = jnp.right_shift(lax.bitcast_convert_type(lo, U32), jnp.uint32(16))
        outs.append(jnp.bitwise_or(hw, lw))
    return jnp.concatenate(outs, axis=1)


def _unpack_halves(w):
    outs = []
    for cb in range(w.shape[1] // LANES):
        wc = w[:, cb * LANES:(cb + 1) * LANES]
        outs.append(lax.bitcast_convert_type(jnp.bitwise_and(wc, jnp.uint32(0xFFFF0000)), F32))
        outs.append(lax.bitcast_convert_type(jnp.left_shift(wc, jnp.uint32(16)), F32))
    return jnp.concatenate(outs, axis=1)


def _mix_kernel(x_ref, y_ref, u_ref, gate_ref, g1_ref, sc2_ref, sh2_ref, wssd_ref, wconf_ref, bconf_ref,
                bgate_ref, wo_ref, nffn_ref, wr_ref, br_ref,
                x1_ref, h2p_ref, topi_ref, topp_ref):
    o_ssd = jnp.dot(y_ref[...], wssd_ref[...], preferred_element_type=F32)
    o_conf = jnp.dot(u_ref[...], wconf_ref[...], preferred_element_type=F32) + bconf_ref[...]
    gates = _sigmoid(gate_ref[...].astype(F32) + bgate_ref[...])
    merged = gates[:, :D_MODEL] * o_ssd + gates[:, D_MODEL:] * o_conf
    out = jnp.dot(merged.astype(BF16), wo_ref[...], preferred_element_type=F32)
    x1 = x_ref[...] + g1_ref[...] * out
    x1_ref[...] = x1
    ms = jnp.mean(x1 * x1, axis=-1, keepdims=True)
    h2 = (x1 * lax.rsqrt(ms + EPS)) * nffn_ref[...]
    h2 = h2 * (1.0 + sc2_ref[...]) + sh2_ref[...]
    h2p_ref[...] = _pack_halves(h2)
    logits = jnp.dot(h2, wr_ref[...], precision=HIGHEST, preferred_element_type=F32) + br_ref[...]
    rows = logits.shape[0]
    lane = lax.broadcasted_iota(jnp.int32, (rows, LANES), 1)
    lane_f = lane.astype(F32)
    neg_inf = jnp.float32(-jnp.inf)
    work = jnp.where(lane < N_EXPERTS, logits, neg_inf)
    vals, idxs = [], []
    for _ in range(TOP_K):
        m = jnp.max(work, axis=-1, keepdims=True)
        idx = jnp.min(jnp.where(work == m, lane_f, jnp.float32(LANES)), axis=-1, keepdims=True)
        vals.append(m)
        idxs.append(idx)
        work = jnp.where(lane_f == idx, neg_inf, work)
    es = [jnp.exp(v - vals[0]) for v in vals]
    denom = es[0] + es[1] + es[2] + es[3]
    topi = jnp.zeros((rows, LANES), F32)
    topp = jnp.zeros((rows, LANES), F32)
    for k in range(TOP_K):
        topi = jnp.where(lane == k, idxs[k], topi)
        topp = jnp.where(lane == k, es[k] / denom, topp)
    topi_ref[...] = topi.astype(jnp.int32)
    topp_ref[...] = topp


def _mix_out(x_all, y_n, u_c, proj, modr, w_ssd, w_conf, b_conf, b_gate, w_o, norm_ffn, w_router, b_router):
    t = x_all.shape[0]
    tm = 256
    upt = tm // UNIT
    full = lambda shape: pl.BlockSpec(shape, lambda m: (0,) * len(shape))
    mod = lambda which: pl.BlockSpec((None, None, 1, D_MODEL), lambda m: (m * upt, which, 0, 0))
    return pl.pallas_call(
        _mix_kernel,
        grid=(t // tm,),
        in_specs=[pl.BlockSpec((tm, D_MODEL), lambda m: (m, 0)),
                  pl.BlockSpec((tm, D_INNER), lambda m: (m, 0)),
                  pl.BlockSpec((tm, D_CONF), lambda m: (m, 0)),
                  pl.BlockSpec((tm, 2 * D_MODEL), lambda m: (m, COL_GATE // (2 * D_MODEL))),
                  mod(2), mod(4), mod(3),
                  full((D_INNER, D_MODEL)), full((D_CONF, D_MODEL)), full((1, D_MODEL)),
                  full((1, 2 * D_MODEL)), full((D_MODEL, D_MODEL)), full((1, D_MODEL)),
                  full((D_MODEL, LANES)), full((1, LANES))],
        out_specs=[pl.BlockSpec((tm, D_MODEL), lambda m: (m, 0)),
                   pl.BlockSpec((tm, D_MODEL // 2), lambda m: (m, 0)),
                   pl.BlockSpec((tm, LANES), lambda m: (m, 0)),
                   pl.BlockSpec((tm, LANES), lambda m: (m, 0))],
        out_shape=[jax.ShapeDtypeStruct((t, D_MODEL), F32),
                   jax.ShapeDtypeStruct((t, D_MODEL // 2), U32),
                   jax.ShapeDtypeStruct((t, LANES), jnp.int32),
                   jax.ShapeDtypeStruct((t, LANES), F32)],
        compiler_params=_cparams(1, 48 * 1024 * 1024),
        name="mix_out_router",
    )(x_all, y_n, u_c, proj, modr, modr, modr, w_ssd, w_conf, b_conf.reshape(1, D_MODEL),
      b_gate.reshape(1, 2 * D_MODEL), w_o, norm_ffn.reshape(1, D_MODEL), w_router, b_router)


def _moe_kernel(te_ref, tv_ref, rt_ref, h2p_ref, wgu_ref, bgu_ref, wd_ref, bd_ref, o_ref,
                wgu_scr, wd_scr, x_scr):
    i = pl.program_id(0)
    e_prev = te_ref[jnp.maximum(i - 1, 0)]
    new_expert = jnp.logical_or(i == 0, te_ref[i] != e_prev)
    valid = tv_ref[i] != 0

    @pl.when(jnp.logical_and(valid, new_expert))
    def _():
        wgu_scr[...] = wgu_ref[...].astype(BF16)
        wd_scr[...] = wd_ref[...].astype(BF16)

    @pl.when(valid)
    def _():
        base = i * MOE_TILE

        def gather(r, carry):
            tok = rt_ref[base + r]
            x_scr[pl.ds(r, 1), :] = h2p_ref[pl.ds(tok, 1), :]
            return carry

        lax.fori_loop(0, MOE_TILE, gather, 0, unroll=8)
        x = _unpack_halves(x_scr[...]).astype(BF16)
        gu = jnp.dot(x, wgu_scr[...], preferred_element_type=F32) + bgu_ref[...]
        g = jnp.minimum(gu[:, :D_EXPERT], SWIGLU_LIMIT)
        u = jnp.clip(gu[:, D_EXPERT:], -SWIGLU_LIMIT, SWIGLU_LIMIT)
        act = (u + 1.0) * g * _sigmoid(SWIGLU_ALPHA * g)
        y = jnp.dot(act.astype(BF16), wd_scr[...], preferred_element_type=F32) + bd_ref[...]
        o_ref[...] = _pack_halves(y)

    @pl.when(jnp.logical_not(valid))
    def _():
        o_ref[...] = jnp.zeros(o_ref.shape, U32)


def _moe_grouped(h2p, tile_e, tile_valid, row_tok, w_gu, b_gu, w_down, b_down):
    t = h2p.shape[0]
    nt = tile_e.shape[0]
    grid_spec = pltpu.PrefetchScalarGridSpec(
        num_scalar_prefetch=3,
        grid=(nt,),
        in_specs=[pl.BlockSpec((t, D_MODEL // 2), lambda i, te, tv, rt: (0, 0), pipeline_mode=pl.Buffered(1)),
                  pl.BlockSpec((None, D_MODEL, 2 * D_EXPERT), lambda i, te, tv, rt: (te[i], 0, 0)),
                  pl.BlockSpec((None, 1, 2 * D_EXPERT), lambda i, te, tv, rt: (te[i], 0, 0)),
                  pl.BlockSpec((None, D_EXPERT, D_MODEL), lambda i, te, tv, rt: (te[i], 0, 0)),
                  pl.BlockSpec((None, 1, D_MODEL), lambda i, te, tv, rt: (te[i], 0, 0))],
        out_specs=pl.BlockSpec((MOE_TILE, D_MODEL // 2), lambda i, te, tv, rt: (i, 0)),
        scratch_shapes=[pltpu.VMEM((D_MODEL, 2 * D_EXPERT), BF16),
                        pltpu.VMEM((D_EXPERT, D_MODEL), BF16),
                        pltpu.VMEM((MOE_TILE, D_MODEL // 2), U32)])
    return pl.pallas_call(
        _moe_kernel,
        grid_spec=grid_spec,
        out_shape=jax.ShapeDtypeStruct((nt * MOE_TILE, D_MODEL // 2), U32),
        compiler_params=_cparams(1, VMEM_LIMIT),
        name="moe_grouped",
    )(tile_e, tile_valid, row_tok, h2p, w_gu, b_gu.reshape(N_EXPERTS, 1, 2 * D_EXPERT),
      w_down, b_down.reshape(N_EXPERTS, 1, D_MODEL))


def _combine_kernel(pos_ref, ys_ref, p_ref, o_ref, tmp_scr, *, tt):
    ntile = p_ref.shape[0] // tt

    def tile_body(ti, carry):
        t0 = pl.multiple_of(ti * tt, tt)

        def gather(r, c2):
            for k in range(TOP_K):
                tmp_scr[k, pl.ds(r, 1), :] = ys_ref[pl.ds(pos_ref[(t0 + r) * TOP_K + k], 1), :]
            return c2

        lax.fori_loop(0, tt, gather, 0, unroll=4)
        pt = p_ref[pl.ds(t0, tt), :]
        acc_hi = jnp.zeros((tt, LANES), F32)
        acc_lo = jnp.zeros((tt, LANES), F32)
        for k in range(TOP_K):
            w = tmp_scr[k]
            pk = jnp.broadcast_to(pt[:, k:k + 1], (tt, LANES))
            acc_hi = acc_hi + pk * lax.bitcast_convert_type(jnp.bitwise_and(w, jnp.uint32(0xFFFF0000)), F32)
            acc_lo = acc_lo + pk * lax.bitcast_convert_type(jnp.left_shift(w, jnp.uint32(16)), F32)
        o_ref[pl.ds(t0, tt), 0:LANES] = acc_hi
        o_ref[pl.ds(t0, tt), LANES:2 * LANES] = acc_lo
        return carry

    lax.fori_loop(0, ntile, tile_body, 0)


def _moe_combine(ysp, pos, topp):
    t = topp.shape[0]
    prow = ysp.shape[0]
    tt = 512
    grid_spec = pltpu.PrefetchScalarGridSpec(
        num_scalar_prefetch=1,
        grid=(D_MODEL // (2 * LANES),),
        in_specs=[pl.BlockSpec((prow, LANES), lambda j, pos: (0, j), pipeline_mode=pl.Buffered(1)),
                  pl.BlockSpec((t, LANES), lambda j, pos: (0, 0))],
        out_specs=pl.BlockSpec((t, 2 * LANES), lambda j, pos: (0, j)),
        scratch_shapes=[pltpu.VMEM((TOP_K, tt, LANES), U32)])
    return pl.pallas_call(
        functools.partial(_combine_kernel, tt=tt),
        grid_spec=grid_spec,
        out_shape=jax.ShapeDtypeStruct((t, D_MODEL), F32),
        compiler_params=_cparams(1, VMEM_LIMIT),
        name="moe_combine",
    )(pos, ysp, topp)


def _final_kernel(x1_ref, moe_ref, g2_ref, nw_ref, o_ref):
    x2 = x1_ref[...] + g2_ref[...] * moe_ref[...]
    ms = jnp.mean(x2 * x2, axis=-1, keepdims=True)
    o_ref[...] = (x2 * lax.rsqrt(ms + EPS)) * nw_ref[...]


def _final(x1, moe, modr, norm_final):
    t = x1.shape[0]
    tm = 512
    upt = tm // UNIT
    return pl.pallas_call(
        _final_kernel,
        grid=(t // tm,),
        in_specs=[pl.BlockSpec((tm, D_MODEL), lambda m: (m, 0)),
                  pl.BlockSpec((tm, D_MODEL), lambda m: (m, 0)),
                  pl.BlockSpec((None, None, 1, D_MODEL), lambda m: (m * upt, 5, 0, 0)),
                  pl.BlockSpec((1, D_MODEL), lambda m: (0, 0))],
        out_specs=pl.BlockSpec((tm, D_MODEL), lambda m: (m, 0)),
        out_shape=jax.ShapeDtypeStruct((t, D_MODEL), F32),
        compiler_params=_cparams(1),
        name="final_norm",
    )(x1, moe, modr, norm_final.reshape(1, D_MODEL))


def _routing_tables(topi, n_tiles):
    t = topi.shape[0]
    eid = topi[:, :TOP_K].reshape(-1)
    onehot = (eid[:, None] == jnp.arange(N_EXPERTS, dtype=jnp.int32)[None, :]).astype(jnp.int32)
    csum = jnp.cumsum(onehot, axis=0)
    rank = jnp.take_along_axis(csum, eid[:, None], axis=1)[:, 0] - 1
    cnt = csum[-1]
    padded = ((cnt + MOE_TILE - 1) // MOE_TILE) * MOE_TILE
    off_end = jnp.cumsum(padded)
    off = off_end - padded
    pos = off[eid] + rank
    row_tok = jnp.zeros((n_tiles * MOE_TILE,), jnp.int32).at[pos].set(
        jnp.arange(t * TOP_K, dtype=jnp.int32) // TOP_K)
    tile_start = jnp.arange(n_tiles, dtype=jnp.int32) * MOE_TILE
    tile_e = jnp.minimum(jnp.searchsorted(off_end, tile_start, side="right").astype(jnp.int32), N_EXPERTS - 1)
    total = off_end[-1]
    tile_valid = (tile_start < total).astype(jnp.int32)
    last_e = tile_e[jnp.maximum(total // MOE_TILE - 1, 0)]
    tile_e = jnp.where(tile_valid != 0, tile_e, last_e)
    return tile_e, tile_valid, row_tok, pos.astype(jnp.int32)


def kernel(x_prompt, x_sample, state_ssm, c, c_ctx, w_ada, b_ada, norm_mix, norm_ffn, w_in, ssm_conv_w, ssm_conv_b, dt_bias, a_log, d_skip, ssm_norm_w, w_ssd_out, conf_dw_w, conf_dw_b, conf_ln_w, conf_ln_b, w_conf_out, b_conf_out, b_gate, w_o, w_router, b_router, w_gu, b_gu, w_down, b_down, norm_final):
    n_ctx, seq_ctx, _ = x_prompt.shape
    n_lat, seq_lat, _ = x_sample.shape
    depth = w_in.shape[0]
    t_ctx, t_lat = n_ctx * seq_ctx, n_lat * seq_lat
    t_all = t_ctx + t_lat
    assert seq_ctx == UNIT and seq_lat % 1024 == 0 and t_ctx % 1024 == 0
    assert n_lat + 1 <= 8 and seq_lat % GRID_W == 0 and UNIT % GRID_W == 0

    x_all = jnp.concatenate([x_prompt.reshape(t_ctx, D_MODEL), x_sample.reshape(t_lat, D_MODEL)], axis=0)
    cvec = jnp.concatenate([c_ctx[None, :], c, jnp.zeros((8 - 1 - n_lat, D_MODEL), F32)], axis=0)
    unit_row = jnp.concatenate([jnp.zeros((t_ctx // UNIT,), jnp.int32),
                                1 + jnp.arange(t_lat // UNIT, dtype=jnp.int32) // (seq_lat // UNIT)])
    n_tiles = (t_all * TOP_K) // MOE_TILE + N_EXPERTS

    new_states = []
    for l in range(depth):
        mod = _ada_mod(cvec, w_ada[l], b_ada[l])
        modr = mod[unit_row].reshape(t_all // UNIT, 6, 1, D_MODEL)

        wl = w_in[l]
        w_main = jnp.concatenate([wl[:, D_INNER:D_INNER + D_XBC], wl[:, :D_INNER],
                                  wl[:, D_INNER + D_XBC + 2 * N_HEADS:]], axis=1).astype(BF16)
        dt0 = D_INNER + D_XBC
        zpad = jnp.zeros((D_MODEL, LANES - N_HEADS), F32)
        w_dt = jnp.concatenate([wl[:, dt0:dt0 + N_HEADS], zpad,
                                wl[:, dt0 + N_HEADS:dt0 + 2 * N_HEADS], zpad], axis=1).astype(BF16)
        proj, dt_raw = _in_proj(x_all, norm_mix[l], modr, w_main, w_dt)

        pad_h = lambda v: jnp.pad(v.reshape(2, 1, N_HEADS), ((0, 0), (0, 0), (0, LANES - N_HEADS)))
        dtb, alog = pad_h(dt_bias[l]), pad_h(a_log[l])
        dskip = jnp.repeat(d_skip[l], HEAD_DIM).reshape(1, D_INNER)
        nw = ssm_norm_w[l].reshape(1, D_INNER)
        h0t = jnp.swapaxes(state_ssm[:, l].astype(F32).reshape(n_lat, 2, D_INNER, D_STATE), 2, 3)

        xbc_ctx = _ssm_conv(proj, ssm_conv_w[l], ssm_conv_b[l], seq=seq_ctx, nseq=n_ctx, row0=0)
        xbc_lat = _ssm_conv(proj, ssm_conv_w[l], ssm_conv_b[l], seq=seq_lat, nseq=n_lat, row0=t_ctx)
        y_ctx, st_ctx = _ssd(xbc_ctx, proj, dt_raw, dtb, alog, dskip, nw, None,
                             seq=seq_ctx, nseq=n_ctx, row0=0, write_state=True)
        (y_lat,) = _ssd(xbc_lat, proj, dt_raw, dtb, alog, dskip, nw, h0t,
                        seq=seq_lat, nseq=n_lat, row0=t_ctx, write_state=False)
        u_ctx = _conformer(proj, conf_dw_w[l], conf_dw_b[l], conf_ln_w[l], conf_ln_b[l],
                           seg=seq_ctx, ntok=t_ctx, row0=0)
        u_lat = _conformer(proj, conf_dw_w[l], conf_dw_b[l], conf_ln_w[l], conf_ln_b[l],
                           seg=GRID_W, ntok=t_lat, row0=t_ctx)
        y_n = jnp.concatenate([y_ctx, y_lat], axis=0)
        u_c = jnp.concatenate([u_ctx, u_lat], axis=0)

        wr = jnp.pad(w_router[l], ((0, 0), (0, LANES - N_EXPERTS)))
        br = jnp.pad(b_router[l], (0, LANES - N_EXPERTS)).reshape(1, LANES)
        x1, h2p, topi, topp = _mix_out(
            x_all, y_n, u_c, proj, modr, w_ssd_out[l].astype(BF16), w_conf_out[l].astype(BF16),
            b_conf_out[l], b_gate[l], w_o[l].astype(BF16), norm_ffn[l], wr, br)

        tile_e, tile_valid, row_tok, pos = _routing_tables(topi, n_tiles)
        ysp = _moe_grouped(h2p, tile_e, tile_valid, row_tok, w_gu[l], b_gu[l], w_down[l], b_down[l])
        moe = _moe_combine(ysp, pos, topp)
        if l + 1 < depth:
            x_all = x1 + jnp.repeat(modr[:, 5, 0, :], UNIT, axis=0) * moe
        new_states.append(st_ctx.reshape(n_ctx, 2, N_HEADS, HEAD_DIM, D_STATE))

    y_all = _final(x1, moe, modr, norm_final)
    y_prompt = y_all[:t_ctx].reshape(n_ctx, seq_ctx, D_MODEL)
    y_sample = y_all[t_ctx:].reshape(n_lat, seq_lat, D_MODEL)
    return (y_prompt, y_sample, jnp.stack(new_states, axis=1))
```

```python
import functools

import jax
import jax.numpy as jnp
from jax import lax
from jax.experimental import pallas as pl
from jax.experimental.pallas import tpu as pltpu

F32 = jnp.float32
BF16 = jnp.bfloat16
U32 = jnp.uint32
HIGHEST = lax.Precision.HIGHEST

D_MODEL = 1024
GRID_W = 64
D_INNER = 2 * D_MODEL
HEAD_DIM = 64
N_HEADS = D_INNER // HEAD_DIM
N_GROUPS = 8
HEADS_PER_GROUP = N_HEADS // N_GROUPS
D_STATE = 128
SSM_CONV = 5
CHUNK = 128
D_XBC = D_INNER + 2 * N_GROUPS * D_STATE
D_CONF = D_MODEL
CONF_KERNEL = 31
N_EXPERTS = 32
TOP_K = 4
D_EXPERT = D_MODEL
SWIGLU_LIMIT = 7.0
SWIGLU_ALPHA = 1.702
EPS = 1e-6

LANES = 128
UNIT = 256
GROUP_W = D_INNER // N_GROUPS
MOE_TILE = 256
MOE_HALVES = 2
MOE_DH = D_EXPERT // MOE_HALVES
XW = D_MODEL // (2 * LANES)
AW = D_MODEL // LANES
RANK_SHIFT = 5
VMEM_LIMIT = 62 * 1024 * 1024

COL_XBC = 0
COL_Z = D_XBC
COL_GLU = D_XBC + D_INNER
COL_GATE = D_XBC + D_INNER + 2 * D_CONF
N_MAIN = D_XBC + D_INNER + 2 * D_CONF + 2 * D_MODEL


def _sigmoid(x):
    return 1.0 / (1.0 + jnp.exp(-x))


def _silu(x):
    return x * _sigmoid(x)


def _cparams(n_axes, vmem=None):
    return pltpu.CompilerParams(
        dimension_semantics=("arbitrary",) * n_axes,
        vmem_limit_bytes=vmem)


def _ada_kernel(c_ref, w_ref, b_ref, o_ref):
    s = _silu(c_ref[...])
    o_ref[...] = jnp.dot(s.astype(BF16), w_ref[...].astype(BF16),
                         preferred_element_type=F32) + b_ref[...]


def _ada_mod(cvec, w_ada, b_ada):
    n = w_ada.shape[1]
    tn = 1536
    return pl.pallas_call(
        _ada_kernel,
        grid=(n // tn,),
        in_specs=[pl.BlockSpec((8, D_MODEL), lambda j: (0, 0)),
                  pl.BlockSpec((D_MODEL, tn), lambda j: (0, j)),
                  pl.BlockSpec((1, tn), lambda j: (0, j))],
        out_specs=pl.BlockSpec((8, tn), lambda j: (0, j)),
        out_shape=jax.ShapeDtypeStruct((8, n), F32),
        compiler_params=_cparams(1, 40 * 1024 * 1024),
        name="ada_mod",
    )(cvec, w_ada, b_ada.reshape(1, n))


def _inproj_kernel(x_ref, nw_ref, sc_ref, sh_ref, w_ref, wdt_ref, o_ref, dt_ref, h_scr):
    @pl.when(pl.program_id(1) == 0)
    def _():
        xf = x_ref[...]
        ms = jnp.mean(xf * xf, axis=-1, keepdims=True)
        hn = (xf * lax.rsqrt(ms + EPS)) * nw_ref[...]
        hn = hn * (1.0 + sc_ref[...]) + sh_ref[...]
        hb = hn.astype(BF16)
        h_scr[...] = hb
        dt_ref[...] = jnp.dot(hb, wdt_ref[...], preferred_element_type=F32)

    o_ref[...] = jnp.dot(h_scr[...], w_ref[...], preferred_element_type=F32).astype(BF16)


def _in_proj(x_all, norm_w, modr, w_main, w_dt):
    t = x_all.shape[0]
    tm, tn = 512, 1024
    upt = tm // UNIT
    return pl.pallas_call(
        _inproj_kernel,
        grid=(t // tm, N_MAIN // tn),
        in_specs=[pl.BlockSpec((tm, D_MODEL), lambda m, n: (m, 0)),
                  pl.BlockSpec((1, D_MODEL), lambda m, n: (0, 0)),
                  pl.BlockSpec((None, None, 1, D_MODEL), lambda m, n: (m * upt, 1, 0, 0)),
                  pl.BlockSpec((None, None, 1, D_MODEL), lambda m, n: (m * upt, 0, 0, 0)),
                  pl.BlockSpec((D_MODEL, tn), lambda m, n: (0, n)),
                  pl.BlockSpec((D_MODEL, 2 * LANES), lambda m, n: (0, 0))],
        out_specs=[pl.BlockSpec((tm, tn), lambda m, n: (m, n)),
                   pl.BlockSpec((tm, 2 * LANES), lambda m, n: (m, 0))],
        out_shape=[jax.ShapeDtypeStruct((t, N_MAIN), BF16),
                   jax.ShapeDtypeStruct((t, 2 * LANES), F32)],
        scratch_shapes=[pltpu.VMEM((tm, D_MODEL), BF16)],
        compiler_params=_cparams(2),
        name="in_proj",
    )(x_all, norm_w.reshape(1, D_MODEL), modr, modr, w_main, w_dt)


def _ssm_conv_kernel(x_ref, w_ref, b_ref, o_ref, pad_scr, *, seq):
    cb = x_ref.shape[1]
    pad_scr[0:8, :] = jnp.zeros((8, cb), F32)
    pad_scr[8 + seq:16 + seq, :] = jnp.zeros((8, cb), F32)
    pad_scr[8:8 + seq, :] = x_ref[...].astype(F32)
    half = SSM_CONV // 2
    rows = 256
    for r0 in range(0, seq, rows):
        acc = jnp.broadcast_to(b_ref[...], (rows, cb))
        for k in range(SSM_CONV):
            acc = acc + w_ref[k:k + 1, :] * pad_scr[8 + r0 + k - half:8 + r0 + k - half + rows, :]
        o_ref[r0:r0 + rows, :] = _silu(acc).astype(BF16)


def _ssm_conv(proj, conv_w, conv_b, *, seq, nseq, row0):
    cb = 512
    blk0 = row0 // seq
    return pl.pallas_call(
        functools.partial(_ssm_conv_kernel, seq=seq),
        grid=(nseq, D_XBC // cb),
        in_specs=[pl.BlockSpec((seq, cb), lambda s, j: (blk0 + s, j)),
                  pl.BlockSpec((SSM_CONV, cb), lambda s, j: (0, j)),
                  pl.BlockSpec((1, cb), lambda s, j: (0, j))],
        out_specs=pl.BlockSpec((seq, cb), lambda s, j: (s, j)),
        out_shape=jax.ShapeDtypeStruct((nseq * seq, D_XBC), BF16),
        scratch_shapes=[pltpu.VMEM((seq + 16, cb), F32)],
        compiler_params=_cparams(2),
        name="ssm_conv",
    )(proj, conv_w, conv_b.reshape(1, D_XBC))


def _expand_heads(v, g, lane):
    r = v.shape[0]
    cols = []
    for half in range(GROUP_W // LANES):
        he = HEADS_PER_GROUP * g + 2 * half
        be = jnp.broadcast_to(v[:, he:he + 1], (r, LANES))
        bo = jnp.broadcast_to(v[:, he + 1:he + 2], (r, LANES))
        cols.append(jnp.where(lane[:r] < HEAD_DIM, be, bo))
    return jnp.concatenate(cols, axis=1)


def _ssd_kernel(*refs, nc, has_h0, write_state):
    (xs_ref, b_ref, c_ref, z_ref, dtr_ref, dtb_ref, alog_ref, dskip_ref, nw_ref), rest = refs[:9], refs[9:]
    if has_h0:
        h0_ref, rest = rest[0], rest[1:]
    y_ref, rest = rest[0], rest[1:]
    if write_state:
        st_ref, rest = rest[0], rest[1:]
    h_scr, ybuf, ychunk = rest

    q = CHUNK
    phase = pl.program_id(1)
    c = pl.program_id(2)
    is_fwd = phase == 1
    c_eff = jnp.where(is_fwd, c, nc - 1 - c)

    @pl.when(c == 0)
    def _():
        if has_h0:
            h_scr[...] = h0_ref[...]
        else:
            h_scr[...] = jnp.zeros(h_scr.shape, F32)

    xdt = dtr_ref[...] + dtb_ref[...]
    dt = jnp.maximum(xdt, 0.0) + jnp.log(1.0 + jnp.exp(-jnp.abs(xdt)))
    a = -jnp.exp(alog_ref[...])
    adt = dt * a
    row = lax.broadcasted_iota(jnp.int32, (q, q), 0)
    col = lax.broadcasted_iota(jnp.int32, (q, q), 1)
    tri = (col - row) * jnp.where(is_fwd, 1, -1) <= 0
    cum = jnp.dot(tri.astype(F32), adt, precision=HIGHEST, preferred_element_type=F32)
    tot = jnp.where(is_fwd, cum[q - 1:q, :], cum[0:1, :])
    rowq = (cum - jnp.log(dt)).T
    wdec = dt * jnp.exp(tot - cum)
    eo = jnp.exp(cum)
    etot = jnp.exp(tot)
    lane = lax.broadcasted_iota(jnp.int32, (q, LANES), 1)
    lane_g = lax.broadcasted_iota(jnp.int32, (q, GROUP_W), 1)
    neg_inf = jnp.float32(-jnp.inf)

    for g in range(N_GROUPS):
        gs = slice(g * GROUP_W, (g + 1) * GROUP_W)
        ns = slice(g * D_STATE, (g + 1) * D_STATE)
        bg = b_ref[:, ns]
        cg = c_ref[:, ns]
        scores = lax.dot_general(cg, bg, (((1,), (1,)), ((), ())), preferred_element_type=F32)
        xs_f = xs_ref[:, gs].astype(F32)
        ms, xb = [], []
        for hh in range(HEADS_PER_GROUP):
            h = HEADS_PER_GROUP * g + hh
            colb = jnp.broadcast_to(cum[:, h:h + 1], (q, q))
            seg = jnp.where(tri, colb - rowq[h:h + 1, :], neg_inf)
            ms.append((scores * jnp.exp(seg)).astype(BF16))
            xb.append(jnp.where(lane_g // HEAD_DIM == hh, xs_f, 0.0).astype(BF16))
        y_diag = jnp.dot(jnp.concatenate(ms, axis=1), jnp.concatenate(xb, axis=0),
                         preferred_element_type=F32)
        xd = (xs_f * _expand_heads(wdec, g, lane)).astype(BF16)
        bt = bg.astype(F32).T.astype(BF16)
        st = jnp.dot(bt, xd, preferred_element_type=F32)
        hg = h_scr[:, gs]
        y_off = jnp.dot(cg, hg.astype(BF16), preferred_element_type=F32) * _expand_heads(eo, g, lane)
        ychunk[:, gs] = y_diag + y_off
        h_scr[:, gs] = hg * _expand_heads(etot, g, lane) + st

    rows = pl.ds(pl.multiple_of(c_eff * q, q), q)

    @pl.when(jnp.logical_not(is_fwd))
    def _():
        ybuf[rows, :] = ychunk[...]

    @pl.when(is_fwd)
    def _():
        zf = z_ref[...].astype(F32)
        yt = ychunk[...] + ybuf[rows, :] + xs_ref[...].astype(F32) * dskip_ref[...]
        yz = yt * _silu(zf)
        for g in range(N_GROUPS):
            gs = slice(g * GROUP_W, (g + 1) * GROUP_W)
            blk = yz[:, gs]
            ms = jnp.mean(blk * blk, axis=-1, keepdims=True)
            y_ref[:, gs] = (blk * lax.rsqrt(ms + EPS) * nw_ref[:, gs]).astype(BF16)

    if write_state:
        @pl.when(c == nc - 1)
        def _():
            for j in range(D_INNER // LANES):
                st_ref[j * LANES:(j + 1) * LANES, :] = h_scr[:, j * LANES:(j + 1) * LANES].T


def _ssd(xbc_c, proj, dt_raw, dt_bias, a_log, d_skip, norm_w, h0t, *, seq, nseq, row0, write_state):
    nc = seq // CHUNK
    blk0 = row0 // CHUNK
    has_h0 = h0t is not None

    def tok(s, p, c):
        return s * nc + p * c + (1 - p) * (nc - 1 - c)

    in_specs = [
        pl.BlockSpec((CHUNK, D_INNER), lambda s, p, c: (tok(s, p, c), 0)),
        pl.BlockSpec((CHUNK, N_GROUPS * D_STATE), lambda s, p, c: (tok(s, p, c), 2)),
        pl.BlockSpec((CHUNK, N_GROUPS * D_STATE), lambda s, p, c: (tok(s, p, c), 3)),
        pl.BlockSpec((CHUNK, D_INNER), lambda s, p, c: (blk0 + s * nc + p * c, COL_Z // D_INNER)),
        pl.BlockSpec((CHUNK, LANES), lambda s, p, c: (blk0 + tok(s, p, c), 1 - p)),
        pl.BlockSpec((None, 1, LANES), lambda s, p, c: (1 - p, 0, 0)),
        pl.BlockSpec((None, 1, LANES), lambda s, p, c: (1 - p, 0, 0)),
        pl.BlockSpec((1, D_INNER), lambda s, p, c: (0, 0)),
        pl.BlockSpec((1, D_INNER), lambda s, p, c: (0, 0)),
    ]
    args = [xbc_c, xbc_c, xbc_c, proj, dt_raw, dt_bias, a_log, d_skip, norm_w]
    if has_h0:
        in_specs.append(pl.BlockSpec((None, None, D_STATE, D_INNER), lambda s, p, c: (s, 1 - p, 0, 0)))
        args.append(h0t)
    out_specs = [pl.BlockSpec((CHUNK, D_INNER), lambda s, p, c: (s * nc + p * c, 0))]
    out_shape = [jax.ShapeDtypeStruct((nseq * seq, D_INNER), BF16)]
    if write_state:
        out_specs.append(pl.BlockSpec((None, None, D_INNER, D_STATE), lambda s, p, c: (s, 1 - p, 0, 0)))
        out_shape.append(jax.ShapeDtypeStruct((nseq, 2, D_INNER, D_STATE), F32))
    return pl.pallas_call(
        functools.partial(_ssd_kernel, nc=nc, has_h0=has_h0, write_state=write_state),
        grid=(nseq, 2, nc),
        in_specs=in_specs,
        out_specs=out_specs,
        out_shape=out_shape,
        scratch_shapes=[pltpu.VMEM((D_STATE, D_INNER), F32),
                        pltpu.VMEM((seq, D_INNER), F32),
                        pltpu.VMEM((CHUNK, D_INNER), F32)],
        compiler_params=_cparams(3, 48 * 1024 * 1024),
        name="ssd_scan",
    )(*args)


def _conf_kernel(glu_ref, w_ref, b_ref, lnw_ref, lnb_ref, o_ref, pad_scr, conv_scr, *, seg):
    rows = glu_ref.shape[0]
    nseg = rows // seg
    half = CONF_KERNEL // 2
    front = 16
    a = glu_ref[:, :D_CONF].astype(F32)
    b = glu_ref[:, D_CONF:].astype(F32)
    u = a * _sigmoid(b)
    for i in range(nseg):
        pad_scr[i, 0:front, :] = jnp.zeros((front, D_CONF), F32)
        pad_scr[i, front + seg:front + seg + 16, :] = jnp.zeros((16, D_CONF), F32)
        pad_scr[i, front:front + seg, :] = u[i * seg:(i + 1) * seg, :]
    rb = 64
    for i in range(nseg):
        for cbi in range(D_CONF // LANES):
            cs = slice(cbi * LANES, (cbi + 1) * LANES)
            for r0 in range(0, seg, rb):
                acc = jnp.broadcast_to(b_ref[:, cs], (rb, LANES))
                for k in range(CONF_KERNEL):
                    start = front + r0 + k - half
                    acc = acc + w_ref[k:k + 1, cs] * pad_scr[i, start:start + rb, cs]
                conv_scr[i * seg + r0:i * seg + r0 + rb, cs] = acc
    v = conv_scr[...]
    mu = jnp.mean(v, axis=-1, keepdims=True)
    vc = v - mu
    var = jnp.mean(vc * vc, axis=-1, keepdims=True)
    ln = (vc * lax.rsqrt(var + EPS)) * lnw_ref[...] + lnb_ref[...]
    o_ref[...] = _silu(ln).astype(BF16)


def _conformer(proj, dw_w, dw_b, ln_w, ln_b, *, seg, ntok, row0):
    rows = UNIT
    blk0 = row0 // rows
    return pl.pallas_call(
        functools.partial(_conf_kernel, seg=seg),
        grid=(ntok // rows,),
        in_specs=[pl.BlockSpec((rows, 2 * D_CONF), lambda i: (blk0 + i, COL_GLU // (2 * D_CONF))),
                  pl.BlockSpec((CONF_KERNEL, D_CONF), lambda i: (0, 0)),
                  pl.BlockSpec((1, D_CONF), lambda i: (0, 0)),
                  pl.BlockSpec((1, D_CONF), lambda i: (0, 0)),
                  pl.BlockSpec((1, D_CONF), lambda i: (0, 0))],
        out_specs=pl.BlockSpec((rows, D_CONF), lambda i: (i, 0)),
        out_shape=jax.ShapeDtypeStruct((ntok, D_CONF), BF16),
        scratch_shapes=[pltpu.VMEM((rows // seg, seg + 32, D_CONF), F32),
                        pltpu.VMEM((rows, D_CONF), F32)],
        compiler_params=_cparams(1),
        name="conformer_conv",
    )(proj, dw_w, dw_b.reshape(1, D_CONF), ln_w.reshape(1, D_CONF), ln_b.reshape(1, D_CONF))


def _pack_halves(x):
    outs = []
    for cb in range(x.shape[1] // (2 * LANES)):
        hi = x[:, cb * 2 * LANES:cb * 2 * LANES + LANES].astype(BF16).astype(F32)
        lo = x[:, cb * 2 * LANES + LANES:(cb + 1) * 2 * LANES].astype(BF16).astype(F32)
        hw = lax.bitcast_convert_type(hi, U32)
        lw = jnp.right_shift(lax.bitcast_convert_type(lo, U32), jnp.uint32(16))
        outs.append(jnp.bitwise_or(hw, lw))
    return jnp.concatenate(outs, axis=1)


def _mix_kernel(x_ref, y_ref, u_ref, gate_ref, g1_ref, sc2_ref, sh2_ref, wssd_ref, wconf_ref, bconf_ref,
                bgate_ref, wo_ref, nffn_ref, wr_ref, br_ref,
                x1_ref, h2i_ref, er_ref, topp_ref, cnt_ref, cnt_scr):
    @pl.when(pl.program_id(0) == 0)
    def _():
        cnt_scr[...] = jnp.zeros(cnt_scr.shape, F32)

    o_ssd = jnp.dot(y_ref[...], wssd_ref[...], preferred_element_type=F32)
    o_conf = jnp.dot(u_ref[...], wconf_ref[...], preferred_element_type=F32) + bconf_ref[...]
    gates = _sigmoid(gate_ref[...].astype(F32) + bgate_ref[...])
    merged = gates[:, :D_MODEL] * o_ssd + gates[:, D_MODEL:] * o_conf
    out = jnp.dot(merged.astype(BF16), wo_ref[...], preferred_element_type=F32)
    x1 = x_ref[...] + g1_ref[...] * out
    x1_ref[...] = x1
    ms = jnp.mean(x1 * x1, axis=-1, keepdims=True)
    h2 = (x1 * lax.rsqrt(ms + EPS)) * nffn_ref[...]
    h2 = h2 * (1.0 + sc2_ref[...]) + sh2_ref[...]
    packed = _pack_halves(h2)
    for qd in range(XW):
        h2i_ref[pl.ds(qd, packed.shape[0], stride=XW), :] = packed[:, qd * LANES:(qd + 1) * LANES]
    logits = jnp.dot(h2, wr_ref[...], precision=HIGHEST, preferred_element_type=F32) + br_ref[...]
    rows = logits.shape[0]
    lane = lax.broadcasted_iota(jnp.int32, (rows, LANES), 1)
    lane_f = lane.astype(F32)
    neg_inf = jnp.float32(-jnp.inf)
    work = jnp.where(lane < N_EXPERTS, logits, neg_inf)
    vals, idxs = [], []
    for _ in range(TOP_K):
        m = jnp.max(work, axis=-1, keepdims=True)
        idx = jnp.min(jnp.where(work == m, lane_f, jnp.float32(LANES)), axis=-1, keepdims=True)
        vals.append(m)
        idxs.append(idx)
        work = jnp.where(lane_f == idx, neg_inf, work)
    es = [jnp.exp(v - vals[0]) for v in vals]
    denom = es[0] + es[1] + es[2] + es[3]
    member = jnp.zeros((rows, LANES), F32)
    for k in range(TOP_K):
        member = member + jnp.where(lane_f == idxs[k], 1.0, 0.0)
    r_i = lax.broadcasted_iota(jnp.int32, (rows, rows), 0)
    c_i = lax.broadcasted_iota(jnp.int32, (rows, rows), 1)
    earlier = jnp.where(c_i < r_i, 1.0, 0.0).astype(BF16)
    rank = jnp.dot(earlier, member.astype(BF16), preferred_element_type=F32) + cnt_scr[...]
    cnt = cnt_scr[...] + jnp.sum(member, axis=0, keepdims=True)
    cnt_scr[...] = cnt
    cnt_ref[...] = jnp.broadcast_to(cnt, cnt_ref.shape).astype(jnp.int32)
    er = jnp.zeros((rows, LANES), F32)
    topp = jnp.zeros((rows, LANES), F32)
    for k in range(TOP_K):
        rank_k = jnp.sum(jnp.where(lane_f == idxs[k], rank, 0.0), axis=-1, keepdims=True)
        er = jnp.where(lane == k, idxs[k] + N_EXPERTS * rank_k, er)
        topp = jnp.where(lane == k, es[k] / denom, topp)
    er_ref[...] = er.astype(jnp.int32)
    topp_ref[...] = topp


def _mix_out(x_all, y_n, u_c, proj, modr, w_ssd, w_conf, b_conf, b_gate, w_o, norm_ffn, w_router, b_router):
    t = x_all.shape[0]
    tm = 256
    upt = tm // UNIT
    full = lambda shape: pl.BlockSpec(shape, lambda m: (0,) * len(shape))
    mod = lambda which: pl.BlockSpec((None, None, 1, D_MODEL), lambda m: (m * upt, which, 0, 0))
    return pl.pallas_call(
        _mix_kernel,
        grid=(t // tm,),
        in_specs=[pl.BlockSpec((tm, D_MODEL), lambda m: (m, 0)),
                  pl.BlockSpec((tm, D_INNER), lambda m: (m, 0)),
                  pl.BlockSpec((tm, D_CONF), lambda m: (m, 0)),
                  pl.BlockSpec((tm, 2 * D_MODEL), lambda m: (m, COL_GATE // (2 * D_MODEL))),
                  mod(2), mod(4), mod(3),
                  full((D_INNER, D_MODEL)), full((D_CONF, D_MODEL)), full((1, D_MODEL)),
                  full((1, 2 * D_MODEL)), full((D_MODEL, D_MODEL)), full((1, D_MODEL)),
                  full((D_MODEL, LANES)), full((1, LANES))],
        out_specs=[pl.BlockSpec((tm, D_MODEL), lambda m: (m, 0)),
                   pl.BlockSpec((tm * XW, LANES), lambda m: (m, 0)),
                   pl.BlockSpec((tm, LANES), lambda m: (m, 0)),
                   pl.BlockSpec((tm, LANES), lambda m: (m, 0)),
                   pl.BlockSpec((8, LANES), lambda m: (0, 0))],
        out_shape=[jax.ShapeDtypeStruct((t, D_MODEL), F32),
                   jax.ShapeDtypeStruct((t * XW, LANES), U32),
                   jax.ShapeDtypeStruct((t, LANES), jnp.int32),
                   jax.ShapeDtypeStruct((t, LANES), F32),
                   jax.ShapeDtypeStruct((8, LANES), jnp.int32)],
        scratch_shapes=[pltpu.VMEM((1, LANES), F32)],
        compiler_params=_cparams(1, 48 * 1024 * 1024),
        name="mix_out_router",
    )(x_all, y_n, u_c, proj, modr, modr, modr, w_ssd, w_conf, b_conf.reshape(1, D_MODEL),
      b_gate.reshape(1, 2 * D_MODEL), w_o, norm_ffn.reshape(1, D_MODEL), w_router, b_router)


def _moe_kernel(off_ref, pair_ref, pb_ref, h2i_ref, wg_ref, wu_ref, wd_ref, bgu_ref, bd_ref, out_ref,
                acc, x_scr, y_scr, wg_b, wu_b, wd_b, sem):
    e = pl.program_id(0)
    j = pl.program_id(1)
    tr = MOE_TILE

    @pl.when(jnp.logical_and(e == 0, j == 0))
    def _():
        acc[...] = jnp.zeros(acc.shape, F32)

    wg_b[...] = wg_ref[...].astype(BF16)
    wu_b[...] = wu_ref[...].astype(BF16)
    wd_b[...] = wd_ref[...].astype(BF16)

    start = off_ref[e]
    end = off_ref[e + 1]
    ntiles = (end - start + tr - 1) // tr
    bg = bgu_ref[:, pl.ds(pl.multiple_of(j * MOE_DH, MOE_DH), MOE_DH)]
    bu = bgu_ref[:, pl.ds(pl.multiple_of(D_EXPERT + j * MOE_DH, MOE_DH), MOE_DH)]
    bd = bd_ref[...] * jnp.where(j == 0, 1.0, 0.0)

    def tile_body(ti, carry):
        base = start + ti * tr

        def gather(r8, c2):
            for rr in range(8):
                r = r8 * 8 + rr
                src = jnp.bitwise_and(pair_ref[base + r], -TOP_K)
                x_scr[pl.ds(pl.multiple_of(r * XW, XW), XW), :] = h2i_ref[pl.ds(pl.multiple_of(src, XW), XW), :]
            return c2

        lax.fori_loop(0, tr // 8, gather, 0)
        parts = []
        for qd in range(XW):
            w = x_scr[pl.ds(qd, tr, stride=XW), :]
            parts.append(lax.bitcast_convert_type(jnp.bitwise_and(w, jnp.uint32(0xFFFF0000)), F32))
            parts.append(lax.bitcast_convert_type(jnp.left_shift(w, jnp.uint32(16)), F32))
        x = jnp.concatenate(parts, axis=1).astype(BF16)
        g = jnp.dot(x, wg_b[...], preferred_element_type=F32) + bg
        u = jnp.dot(x, wu_b[...], preferred_element_type=F32) + bu
        g = jnp.minimum(g, SWIGLU_LIMIT)
        u = jnp.clip(u, -SWIGLU_LIMIT, SWIGLU_LIMIT)
        act = (u + 1.0) * g * _sigmoid(SWIGLU_ALPHA * g)
        y = jnp.dot(act.astype(BF16), wd_b[...], preferred_element_type=F32) + bd
        for qd in range(AW):
            y_scr[pl.ds(qd, tr, stride=AW), :] = y[:, qd * LANES:(qd + 1) * LANES]

        def scatter(r8, c2):
            dsts, ps = [], []
            for rr in range(8):
                r = base + r8 * 8 + rr
                pair = pair_ref[r]
                dsts.append(pl.multiple_of(jnp.bitwise_and(pair, -TOP_K) * (AW // XW), AW))
                pw = lax.bitcast_convert_type(pb_ref[pair], F32)
                ps.append(jnp.where(r < end, pw, 0.0))
            olds = [acc[pl.ds(dsts[rr], AW), :] for rr in range(8)]
            news = [olds[rr] + ps[rr] * y_scr[pl.ds(pl.multiple_of((r8 * 8 + rr) * AW, AW), AW), :]
                    for rr in range(8)]
            for rr in range(8):
                acc[pl.ds(dsts[rr], AW), :] = news[rr]
            return c2

        lax.fori_loop(0, tr // 8, scatter, 0)
        return carry

    lax.fori_loop(0, ntiles, tile_body, 0)

    @pl.when(jnp.logical_and(e == N_EXPERTS - 1, j == MOE_HALVES - 1))
    def _():
        cp = pltpu.make_async_copy(acc, out_ref, sem)
        cp.start()
        cp.wait()


def _moe_grouped(h2i, off, row_pair, pbits, w_gu, b_gu, w_down, b_down):
    t = h2i.shape[0] // XW
    grid_spec = pltpu.PrefetchScalarGridSpec(
        num_scalar_prefetch=3,
        grid=(N_EXPERTS, MOE_HALVES),
        in_specs=[pl.BlockSpec((t * XW, LANES), lambda e, j, *_: (0, 0), pipeline_mode=pl.Buffered(1)),
                  pl.BlockSpec((None, D_MODEL, MOE_DH), lambda e, j, *_: (e, 0, j)),
                  pl.BlockSpec((None, D_MODEL, MOE_DH), lambda e, j, *_: (e, 0, MOE_HALVES + j)),
                  pl.BlockSpec((None, MOE_DH, D_MODEL), lambda e, j, *_: (e, j, 0)),
                  pl.BlockSpec((None, 1, 2 * D_EXPERT), lambda e, j, *_: (e, 0, 0)),
                  pl.BlockSpec((None, 1, D_MODEL), lambda e, j, *_: (e, 0, 0))],
        out_specs=pl.BlockSpec(memory_space=pl.ANY),
        scratch_shapes=[pltpu.VMEM((t * AW, LANES), F32),
                        pltpu.VMEM((MOE_TILE * XW, LANES), U32),
                        pltpu.VMEM((MOE_TILE * AW, LANES), F32),
                        pltpu.VMEM((D_MODEL, MOE_DH), BF16),
                        pltpu.VMEM((D_MODEL, MOE_DH), BF16),
                        pltpu.VMEM((MOE_DH, D_MODEL), BF16),
                        pltpu.SemaphoreType.DMA(())])
    return pl.pallas_call(
        _moe_kernel,
        grid_spec=grid_spec,
        out_shape=jax.ShapeDtypeStruct((t * AW, LANES), F32),
        compiler_params=_cparams(2, VMEM_LIMIT),
        name="moe_grouped",
    )(off, row_pair, pbits, h2i, w_gu, w_gu, w_down, b_gu.reshape(N_EXPERTS, 1, 2 * D_EXPERT),
      b_down.reshape(N_EXPERTS, 1, D_MODEL))


def _residual_kernel(x1_ref, moe_ref, g2_ref, nw_ref, o_ref, *, normalize):
    tm = x1_ref.shape[0]
    moe = jnp.concatenate([moe_ref[pl.ds(qd, tm, stride=AW), :] for qd in range(AW)], axis=1)
    x2 = x1_ref[...] + g2_ref[...] * moe
    if normalize:
        ms = jnp.mean(x2 * x2, axis=-1, keepdims=True)
        x2 = (x2 * lax.rsqrt(ms + EPS)) * nw_ref[...]
    o_ref[...] = x2


def _residual(x1, moe_i, modr, norm_w, *, normalize):
    t = x1.shape[0]
    tm = 512
    upt = tm // UNIT
    return pl.pallas_call(
        functools.partial(_residual_kernel, normalize=normalize),
        grid=(t // tm,),
        in_specs=[pl.BlockSpec((tm, D_MODEL), lambda m: (m, 0)),
                  pl.BlockSpec((tm * AW, LANES), lambda m: (m, 0)),
                  pl.BlockSpec((None, None, 1, D_MODEL), lambda m: (m * upt, 5, 0, 0)),
                  pl.BlockSpec((1, D_MODEL), lambda m: (0, 0))],
        out_specs=pl.BlockSpec((tm, D_MODEL), lambda m: (m, 0)),
        out_shape=jax.ShapeDtypeStruct((t, D_MODEL), F32),
        compiler_params=_cparams(1),
        name="ffn_residual_norm",
    )(x1, moe_i, modr, norm_w.reshape(1, D_MODEL))


def _route_kernel(er_ref, cnt_ref, off_ref, pair_ref, offs, *, n_pairs):
    def prefix(e, run):
        offs[e] = run
        off_ref[e] = run
        return run + cnt_ref[e]

    total = lax.fori_loop(0, N_EXPERTS, prefix, jnp.int32(0))
    for e in range(N_EXPERTS, off_ref.shape[0]):
        off_ref[e] = total

    def pad(i, c):
        pair_ref[n_pairs + i] = 0
        return c

    lax.fori_loop(0, MOE_TILE, pad, 0)

    def place(i8, c):
        for ii in range(8):
            i = i8 * 8 + ii
            er = er_ref[i]
            pos = offs[jnp.bitwise_and(er, N_EXPERTS - 1)] + lax.shift_right_logical(er, RANK_SHIFT)
            pair_ref[pos] = i
        return c

    lax.fori_loop(0, n_pairs // 8, place, 0)


def _route(er_flat, cnt):
    n = er_flat.shape[0]
    smem = pl.BlockSpec(memory_space=pltpu.SMEM)
    return pl.pallas_call(
        functools.partial(_route_kernel, n_pairs=n),
        in_specs=[smem, smem],
        out_specs=[smem, smem],
        out_shape=[jax.ShapeDtypeStruct((N_EXPERTS + 8,), jnp.int32),
                   jax.ShapeDtypeStruct((n + MOE_TILE,), jnp.int32)],
        scratch_shapes=[pltpu.SMEM((N_EXPERTS,), jnp.int32)],
        name="moe_route",
    )(er_flat, cnt)


def kernel(x_prompt, x_sample, state_ssm, c, c_ctx, w_ada, b_ada, norm_mix, norm_ffn, w_in, ssm_conv_w, ssm_conv_b, dt_bias, a_log, d_skip, ssm_norm_w, w_ssd_out, conf_dw_w, conf_dw_b, conf_ln_w, conf_ln_b, w_conf_out, b_conf_out, b_gate, w_o, w_router, b_router, w_gu, b_gu, w_down, b_down, norm_final):
    n_ctx, seq_ctx, _ = x_prompt.shape
    n_lat, seq_lat, _ = x_sample.shape
    depth = w_in.shape[0]
    t_ctx, t_lat = n_ctx * seq_ctx, n_lat * seq_lat
    t_all = t_ctx + t_lat
    assert seq_ctx == UNIT and seq_lat % 1024 == 0 and t_ctx % 1024 == 0
    assert n_lat + 1 <= 8 and seq_lat % GRID_W == 0 and UNIT % GRID_W == 0

    x_all = jnp.concatenate([x_prompt.reshape(t_ctx, D_MODEL), x_sample.reshape(t_lat, D_MODEL)], axis=0)
    cvec = jnp.concatenate([c_ctx[None, :], c, jnp.zeros((8 - 1 - n_lat, D_MODEL), F32)], axis=0)
    unit_row = jnp.concatenate([jnp.zeros((t_ctx // UNIT,), jnp.int32),
                                1 + jnp.arange(t_lat // UNIT, dtype=jnp.int32) // (seq_lat // UNIT)])
    assert XW == TOP_K and N_EXPERTS == 1 << RANK_SHIFT

    new_states = []
    for l in range(depth):
        mod = _ada_mod(cvec, w_ada[l], b_ada[l])
        modr = mod[unit_row].reshape(t_all // UNIT, 6, 1, D_MODEL)

        wl = w_in[l]
        w_main = jnp.concatenate([wl[:, D_INNER:D_INNER + D_XBC], wl[:, :D_INNER],
                                  wl[:, D_INNER + D_XBC + 2 * N_HEADS:]], axis=1).astype(BF16)
        dt0 = D_INNER + D_XBC
        zpad = jnp.zeros((D_MODEL, LANES - N_HEADS), F32)
        w_dt = jnp.concatenate([wl[:, dt0:dt0 + N_HEADS], zpad,
                                wl[:, dt0 + N_HEADS:dt0 + 2 * N_HEADS], zpad], axis=1).astype(BF16)
        proj, dt_raw = _in_proj(x_all, norm_mix[l], modr, w_main, w_dt)

        pad_h = lambda v: jnp.pad(v.reshape(2, 1, N_HEADS), ((0, 0), (0, 0), (0, LANES - N_HEADS)))
        dtb, alog = pad_h(dt_bias[l]), pad_h(a_log[l])
        dskip = jnp.repeat(d_skip[l], HEAD_DIM).reshape(1, D_INNER)
        nw = ssm_norm_w[l].reshape(1, D_INNER)
        h0t = jnp.swapaxes(state_ssm[:, l].astype(F32).reshape(n_lat, 2, D_INNER, D_STATE), 2, 3)

        xbc_ctx = _ssm_conv(proj, ssm_conv_w[l], ssm_conv_b[l], seq=seq_ctx, nseq=n_ctx, row0=0)
        xbc_lat = _ssm_conv(proj, ssm_conv_w[l], ssm_conv_b[l], seq=seq_lat, nseq=n_lat, row0=t_ctx)
        y_ctx, st_ctx = _ssd(xbc_ctx, proj, dt_raw, dtb, alog, dskip, nw, None,
                             seq=seq_ctx, nseq=n_ctx, row0=0, write_state=True)
        (y_lat,) = _ssd(xbc_lat, proj, dt_raw, dtb, alog, dskip, nw, h0t,
                        seq=seq_lat, nseq=n_lat, row0=t_ctx, write_state=False)
        u_ctx = _conformer(proj, conf_dw_w[l], conf_dw_b[l], conf_ln_w[l], conf_ln_b[l],
                           seg=seq_ctx, ntok=t_ctx, row0=0)
        u_lat = _conformer(proj, conf_dw_w[l], conf_dw_b[l], conf_ln_w[l], conf_ln_b[l],
                           seg=GRID_W, ntok=t_lat, row0=t_ctx)
        y_n = jnp.concatenate([y_ctx, y_lat], axis=0)
        u_c = jnp.concatenate([u_ctx, u_lat], axis=0)

        wr = jnp.pad(w_router[l], ((0, 0), (0, LANES - N_EXPERTS)))
        br = jnp.pad(b_router[l], (0, LANES - N_EXPERTS)).reshape(1, LANES)
        x1, h2i, er, topp, cnt = _mix_out(
            x_all, y_n, u_c, proj, modr, w_ssd_out[l].astype(BF16), w_conf_out[l].astype(BF16),
            b_conf_out[l], b_gate[l], w_o[l].astype(BF16), norm_ffn[l], wr, br)

        off, row_pair = _route(er[:, :TOP_K].reshape(-1), cnt[0, :N_EXPERTS])
        pbits = lax.bitcast_convert_type(topp[:, :TOP_K], jnp.int32).reshape(-1)
        moe_i = _moe_grouped(h2i, off, row_pair, pbits, w_gu[l], b_gu[l], w_down[l], b_down[l])
        last = l + 1 == depth
        x_all = _residual(x1, moe_i, modr, norm_final if last else norm_ffn[l], normalize=last)
        new_states.append(st_ctx.reshape(n_ctx, 2, N_HEADS, HEAD_DIM, D_STATE))

    y_all = x_all
    y_prompt = y_all[:t_ctx].reshape(n_ctx, seq_ctx, D_MODEL)
    y_sample = y_all[t_ctx:].reshape(n_lat, seq_lat, D_MODEL)
    return (y_prompt, y_sample, jnp.stack(new_states, axis=1))
```

```python
import functools

import jax
import jax.numpy as jnp
from jax import lax
from jax.experimental import pallas as pl
from jax.experimental.pallas import tpu as pltpu

F32 = jnp.float32
BF16 = jnp.bfloat16
U32 = jnp.uint32

D_MODEL = 1024
GRID_W = 64
D_INNER = 2 * D_MODEL
HEAD_DIM = 64
N_HEADS = D_INNER // HEAD_DIM
N_GROUPS = 8
HEADS_PER_GROUP = N_HEADS // N_GROUPS
D_STATE = 128
SSM_CONV = 5
CHUNK = 128
D_XBC = D_INNER + 2 * N_GROUPS * D_STATE
D_CONF = D_MODEL
CONF_KERNEL = 31
N_EXPERTS = 32
TOP_K = 4
D_EXPERT = D_MODEL
SWIGLU_LIMIT = 7.0
SWIGLU_ALPHA = 1.702
EPS = 1e-6

LANES = 128
UNIT = 256
GROUP_W = D_INNER // N_GROUPS
SSD_GROUP_UNROLL = 4
MOE_TILE = 256
ROUTE_PAD = 2 * MOE_TILE
MOE_HALVES = 2
MOE_DH = D_EXPERT // MOE_HALVES
XW = D_MODEL // (2 * LANES)
AW = D_MODEL // LANES
RANK_SHIFT = 5
VMEM_LIMIT = 62 * 1024 * 1024

COL_XBC = 0
COL_Z = D_XBC
COL_GLU = D_XBC + D_INNER
COL_GATE = D_XBC + D_INNER + 2 * D_CONF
N_MAIN = D_XBC + D_INNER + 2 * D_CONF + 2 * D_MODEL


def _sigmoid(x):
    return 1.0 / (1.0 + jnp.exp(-x))


def _silu(x):
    return x * _sigmoid(x)


def _cparams(n_axes, vmem=None):
    return pltpu.CompilerParams(
        dimension_semantics=("arbitrary",) * n_axes,
        vmem_limit_bytes=vmem)


def _ada_kernel(c_ref, w_ref, b_ref, o_ref):
    s = _silu(c_ref[...])
    o_ref[...] = jnp.dot(s.astype(BF16), w_ref[...].astype(BF16),
                         preferred_element_type=F32) + b_ref[...]


def _ada_mod(cvec, w_ada, b_ada):
    n = w_ada.shape[1]
    tn = 1536
    return pl.pallas_call(
        _ada_kernel,
        grid=(n // tn,),
        in_specs=[pl.BlockSpec((8, D_MODEL), lambda j: (0, 0)),
                  pl.BlockSpec((D_MODEL, tn), lambda j: (0, j)),
                  pl.BlockSpec((1, tn), lambda j: (0, j))],
        out_specs=pl.BlockSpec((8, tn), lambda j: (0, j)),
        out_shape=jax.ShapeDtypeStruct((8, n), F32),
        compiler_params=_cparams(1, 40 * 1024 * 1024),
        name="ada_mod",
    )(cvec, w_ada, b_ada.reshape(1, n))


def _inproj_kernel(xc_ref, xl_ref, nw_ref, sc_ref, sh_ref, w_ref, wdt_ref, o_ref, dt_ref, h_scr, *, ctx_tiles):
    @pl.when(pl.program_id(1) == 0)
    def _():
        xf = jnp.where(pl.program_id(0) < ctx_tiles, xc_ref[...], xl_ref[...])
        ms = jnp.mean(xf * xf, axis=-1, keepdims=True)
        hn = (xf * lax.rsqrt(ms + EPS)) * nw_ref[...]
        hn = hn * (1.0 + sc_ref[...]) + sh_ref[...]
        hb = hn.astype(BF16)
        h_scr[...] = hb
        dt_ref[...] = jnp.dot(hb, wdt_ref[...], preferred_element_type=F32)

    o_ref[...] = jnp.dot(h_scr[...], w_ref[...], preferred_element_type=F32).astype(BF16)


def _two_source_specs(tm, width, ctx_tiles, nargs):
    if nargs == 1:
        return [pl.BlockSpec((tm, width), lambda m: (jnp.minimum(m, ctx_tiles - 1), 0)),
                pl.BlockSpec((tm, width), lambda m: (jnp.maximum(m - ctx_tiles, 0), 0))]
    return [pl.BlockSpec((tm, width), lambda m, n: (jnp.minimum(m, ctx_tiles - 1), 0)),
            pl.BlockSpec((tm, width), lambda m, n: (jnp.maximum(m - ctx_tiles, 0), 0))]


def _in_proj(x_ctx, x_lat, norm_w, modr, w_main, w_dt):
    t_ctx = x_ctx.shape[0]
    t = t_ctx + x_lat.shape[0]
    tm, tn = 1024, 2048
    upt = tm // UNIT
    return pl.pallas_call(
        functools.partial(_inproj_kernel, ctx_tiles=t_ctx // tm),
        grid=(t // tm, N_MAIN // tn),
        in_specs=_two_source_specs(tm, D_MODEL, t_ctx // tm, 2) + [
                  pl.BlockSpec((1, D_MODEL), lambda m, n: (0, 0)),
                  pl.BlockSpec((None, None, 1, D_MODEL), lambda m, n: (m * upt, 1, 0, 0)),
                  pl.BlockSpec((None, None, 1, D_MODEL), lambda m, n: (m * upt, 0, 0, 0)),
                  pl.BlockSpec((D_MODEL, tn), lambda m, n: (0, n)),
                  pl.BlockSpec((D_MODEL, 2 * LANES), lambda m, n: (0, 0))],
        out_specs=[pl.BlockSpec((tm, tn), lambda m, n: (m, n)),
                   pl.BlockSpec((tm, 2 * LANES), lambda m, n: (m, 0))],
        out_shape=[jax.ShapeDtypeStruct((t, N_MAIN), BF16),
                   jax.ShapeDtypeStruct((t, 2 * LANES), F32)],
        scratch_shapes=[pltpu.VMEM((tm, D_MODEL), BF16)],
        compiler_params=_cparams(2, 48 * 1024 * 1024),
        name="in_proj",
    )(x_ctx, x_lat, norm_w.reshape(1, D_MODEL), modr, modr, w_main, w_dt)


def _ssm_conv_kernel(x_ref, w_ref, b_ref, o_ref, pad_scr, *, seq):
    cb = x_ref.shape[1]
    pad_scr[0:8, :] = jnp.zeros((8, cb), F32)
    pad_scr[8 + seq:16 + seq, :] = jnp.zeros((8, cb), F32)
    pad_scr[8:8 + seq, :] = x_ref[...].astype(F32)
    half = SSM_CONV // 2
    rows, cw = 256, 512
    for c0 in range(0, cb, cw):
        cs = slice(c0, c0 + cw)
        for r0 in range(0, seq, rows):
            acc = jnp.broadcast_to(b_ref[:, cs], (rows, cw))
            for k in range(SSM_CONV):
                acc = acc + w_ref[k:k + 1, cs] * pad_scr[8 + r0 + k - half:8 + r0 + k - half + rows, cs]
            o_ref[r0:r0 + rows, cs] = _silu(acc).astype(BF16)


def _ssm_conv(proj, conv_w, conv_b, *, seq, nseq, row0):
    cb = 2048
    blk0 = row0 // seq
    return pl.pallas_call(
        functools.partial(_ssm_conv_kernel, seq=seq),
        grid=(nseq, D_XBC // cb),
        in_specs=[pl.BlockSpec((seq, cb), lambda s, j: (blk0 + s, j)),
                  pl.BlockSpec((SSM_CONV, cb), lambda s, j: (0, j)),
                  pl.BlockSpec((1, cb), lambda s, j: (0, j))],
        out_specs=pl.BlockSpec((seq, cb), lambda s, j: (s, j)),
        out_shape=jax.ShapeDtypeStruct((nseq * seq, D_XBC), BF16),
        scratch_shapes=[pltpu.VMEM((seq + 16, cb), F32)],
        compiler_params=_cparams(2, 40 * 1024 * 1024),
        name="ssm_conv",
    )(proj, conv_w, conv_b.reshape(1, D_XBC))


def _split_bf16(v):
    hi = v.astype(BF16)
    lo = (v - hi.astype(F32)).astype(BF16)
    return jnp.concatenate([hi, lo], axis=1)


def _head_select_matrices():
    j = jnp.arange(2 * LANES, dtype=jnp.int32)[:, None] % LANES
    full = (j == (jnp.arange(N_HEADS * LANES, dtype=jnp.int32)[None, :] // LANES)).astype(BF16)
    exp = (j == (jnp.arange(D_INNER, dtype=jnp.int32)[None, :] // HEAD_DIM)).astype(BF16)
    return full, exp


def _ssd_kernel(*refs, nc, has_h0, write_state):
    (xs_ref, b_ref, c_ref, z_ref, dtr_ref, dtb_ref, alog_ref, dskip_ref, nw_ref, self_ref, sele_ref), rest = \
        refs[:11], refs[11:]
    if has_h0:
        h0_ref, rest = rest[0], rest[1:]
    y_ref, rest = rest[0], rest[1:]
    if write_state:
        st_ref, rest = rest[0], rest[1:]
    h_scr, ybuf, ychunk, colb_scr, wexp_scr, eexp_scr, texp_scr, rowq_scr = rest

    q = CHUNK
    phase = pl.program_id(1)
    c = pl.program_id(2)
    is_fwd = phase == 1
    c_eff = jnp.where(is_fwd, c, nc - 1 - c)

    @pl.when(c == 0)
    def _():
        if has_h0:
            h_scr[...] = h0_ref[...]
        else:
            h_scr[...] = jnp.zeros(h_scr.shape, F32)

    xdt = dtr_ref[...] + dtb_ref[...]
    dt = jnp.maximum(xdt, 0.0) + jnp.log(1.0 + jnp.exp(-jnp.abs(xdt)))
    a = -jnp.exp(alog_ref[...])
    adt = dt * a
    row = lax.broadcasted_iota(jnp.int32, (q, q), 0)
    col = lax.broadcasted_iota(jnp.int32, (q, q), 1)
    tri = (col - row) * jnp.where(is_fwd, 1, -1) <= 0
    cum2 = jnp.dot(tri.astype(BF16), _split_bf16(adt), preferred_element_type=F32)
    cum = cum2[:, :LANES] + cum2[:, LANES:]
    tot = jnp.where(is_fwd, cum[q - 1:q, :], cum[0:1, :])
    rowq_scr[...] = (cum - jnp.log(dt)).T
    wdec = dt * jnp.exp(tot - cum)
    eo = jnp.exp(cum)
    etot = jnp.broadcast_to(jnp.exp(tot), (8, LANES))
    colb_scr[...] = jnp.dot(_split_bf16(cum), self_ref[...], preferred_element_type=F32)
    expanded = jnp.dot(jnp.concatenate([_split_bf16(wdec), _split_bf16(eo), _split_bf16(etot)], axis=0),
                       sele_ref[...], preferred_element_type=F32)
    wexp_scr[...] = expanded[0:q]
    eexp_scr[...] = expanded[q:2 * q]
    texp_scr[...] = expanded[2 * q:2 * q + 8]
    lane_g = lax.broadcasted_iota(jnp.int32, (1, GROUP_W), 1)
    head_mask = [(lane_g // HEAD_DIM == hh).astype(BF16) for hh in range(HEADS_PER_GROUP)]
    neg_inf = jnp.float32(-jnp.inf)

    def group_body(g, carry):
        gs = pl.ds(pl.multiple_of(g * GROUP_W, GROUP_W), GROUP_W)
        ns = pl.ds(pl.multiple_of(g * D_STATE, D_STATE), D_STATE)
        bg = b_ref[:, ns]
        cg = c_ref[:, ns]
        scores = lax.dot_general(cg, bg, (((1,), (1,)), ((), ())), preferred_element_type=F32)
        xs_g = xs_ref[:, gs]
        ms, xb = [], []
        for hh in range(HEADS_PER_GROUP):
            h = HEADS_PER_GROUP * g + hh
            colb = colb_scr[:, pl.ds(pl.multiple_of(h * LANES, LANES), LANES)]
            seg = jnp.where(tri, colb - rowq_scr[pl.ds(h, 1), :], neg_inf)
            ms.append((scores * jnp.exp(seg)).astype(BF16))
            xb.append(xs_g * head_mask[hh])
        y_diag = jnp.dot(jnp.concatenate(ms, axis=1), jnp.concatenate(xb, axis=0),
                         preferred_element_type=F32)
        xd = (xs_g.astype(F32) * wexp_scr[:, gs]).astype(BF16)
        bt = bg.astype(F32).T.astype(BF16)
        st = jnp.dot(bt, xd, preferred_element_type=F32)
        hg = h_scr[:, gs]
        y_off = jnp.dot(cg, hg.astype(BF16), preferred_element_type=F32) * eexp_scr[:, gs]
        ychunk[:, gs] = y_diag + y_off
        h_scr[:, gs] = hg * texp_scr[0:1, gs] + st
        return carry

    lax.fori_loop(0, N_GROUPS, group_body, 0, unroll=SSD_GROUP_UNROLL)

    rows = pl.ds(pl.multiple_of(c_eff * q, q), q)

    @pl.when(jnp.logical_not(is_fwd))
    def _():
        ybuf[rows, :] = ychunk[...]

    @pl.when(is_fwd)
    def _():
        zf = z_ref[...].astype(F32)
        yt = ychunk[...] + ybuf[rows, :] + xs_ref[...].astype(F32) * dskip_ref[...]
        yz = yt * _silu(zf)
        for g in range(N_GROUPS):
            gs = slice(g * GROUP_W, (g + 1) * GROUP_W)
            blk = yz[:, gs]
            ms = jnp.mean(blk * blk, axis=-1, keepdims=True)
            y_ref[:, gs] = (blk * lax.rsqrt(ms + EPS) * nw_ref[:, gs]).astype(BF16)

    if write_state:
        @pl.when(c == nc - 1)
        def _():
            for j in range(D_INNER // LANES):
                st_ref[j * LANES:(j + 1) * LANES, :] = h_scr[:, j * LANES:(j + 1) * LANES].T


def _ssd(xbc_c, proj, dt_raw, dt_bias, a_log, d_skip, norm_w, h0t, *, seq, nseq, row0, write_state):
    nc = seq // CHUNK
    blk0 = row0 // CHUNK
    has_h0 = h0t is not None

    def tok(s, p, c):
        return s * nc + p * c + (1 - p) * (nc - 1 - c)

    in_specs = [
        pl.BlockSpec((CHUNK, D_INNER), lambda s, p, c: (tok(s, p, c), 0)),
        pl.BlockSpec((CHUNK, N_GROUPS * D_STATE), lambda s, p, c: (tok(s, p, c), 2)),
        pl.BlockSpec((CHUNK, N_GROUPS * D_STATE), lambda s, p, c: (tok(s, p, c), 3)),
        pl.BlockSpec((CHUNK, D_INNER), lambda s, p, c: (blk0 + s * nc + p * c, COL_Z // D_INNER)),
        pl.BlockSpec((CHUNK, LANES), lambda s, p, c: (blk0 + tok(s, p, c), 1 - p)),
        pl.BlockSpec((None, 1, LANES), lambda s, p, c: (1 - p, 0, 0)),
        pl.BlockSpec((None, 1, LANES), lambda s, p, c: (1 - p, 0, 0)),
        pl.BlockSpec((1, D_INNER), lambda s, p, c: (0, 0)),
        pl.BlockSpec((1, D_INNER), lambda s, p, c: (0, 0)),
        pl.BlockSpec((2 * LANES, N_HEADS * LANES), lambda s, p, c: (0, 0)),
        pl.BlockSpec((2 * LANES, D_INNER), lambda s, p, c: (0, 0)),
    ]
    sel_full, sel_exp = _head_select_matrices()
    args = [xbc_c, xbc_c, xbc_c, proj, dt_raw, dt_bias, a_log, d_skip, norm_w, sel_full, sel_exp]
    if has_h0:
        in_specs.append(pl.BlockSpec((None, None, D_STATE, D_INNER), lambda s, p, c: (s, 1 - p, 0, 0)))
        args.append(h0t)
    out_specs = [pl.BlockSpec((CHUNK, D_INNER), lambda s, p, c: (s * nc + p * c, 0))]
    out_shape = [jax.ShapeDtypeStruct((nseq * seq, D_INNER), BF16)]
    if write_state:
        out_specs.append(pl.BlockSpec((None, None, D_INNER, D_STATE), lambda s, p, c: (s, 1 - p, 0, 0)))
        out_shape.append(jax.ShapeDtypeStruct((nseq, 2, D_INNER, D_STATE), F32))
    return pl.pallas_call(
        functools.partial(_ssd_kernel, nc=nc, has_h0=has_h0, write_state=write_state),
        grid=(nseq, 2, nc),
        in_specs=in_specs,
        out_specs=out_specs,
        out_shape=out_shape,
        scratch_shapes=[pltpu.VMEM((D_STATE, D_INNER), F32),
                        pltpu.VMEM((seq, D_INNER), F32),
                        pltpu.VMEM((CHUNK, D_INNER), F32),
                        pltpu.VMEM((CHUNK, N_HEADS * LANES), F32),
                        pltpu.VMEM((CHUNK, D_INNER), F32),
                        pltpu.VMEM((CHUNK, D_INNER), F32),
                        pltpu.VMEM((8, D_INNER), F32),
                        pltpu.VMEM((LANES, CHUNK), F32)],
        compiler_params=_cparams(3, 48 * 1024 * 1024),
        name="ssd_scan",
    )(*args)


def _conf_kernel(glu_ref, w_ref, b_ref, lnw_ref, lnb_ref, o_ref, pad_scr, sh_scr, conv_scr, *, seg):
    rows = glu_ref.shape[0]
    nseg = rows // seg
    half = CONF_KERNEL // 2
    front = 16
    span = seg + 24
    a = glu_ref[:, :D_CONF].astype(F32)
    b = glu_ref[:, D_CONF:].astype(F32)
    u = a * _sigmoid(b)
    for i in range(nseg):
        pad_scr[i, 0:front, :] = jnp.zeros((front, D_CONF), F32)
        pad_scr[i, front + seg:front + seg + 16, :] = jnp.zeros((16, D_CONF), F32)
        pad_scr[i, front:front + seg, :] = u[i * seg:(i + 1) * seg, :]
    for s in range(8):
        for i in range(nseg):
            for cbi in range(D_CONF // LANES):
                cs = slice(cbi * LANES, (cbi + 1) * LANES)
                for r0 in range(0, span, 56):
                    n = min(56, span - r0)
                    sh_scr[s, i, r0:r0 + n, cs] = pad_scr[i, s + r0:s + r0 + n, cs]
    rb = 64
    for i in range(nseg):
        for cbi in range(D_CONF // LANES):
            cs = slice(cbi * LANES, (cbi + 1) * LANES)
            for r0 in range(0, seg, rb):
                acc = jnp.broadcast_to(b_ref[:, cs], (rb, LANES))
                for k in range(CONF_KERNEL):
                    start = front + r0 + k - half
                    al = start - start % 8
                    acc = acc + w_ref[k:k + 1, cs] * sh_scr[start % 8, i, al:al + rb, cs]
                conv_scr[i * seg + r0:i * seg + r0 + rb, cs] = acc
    v = conv_scr[...]
    mu = jnp.mean(v, axis=-1, keepdims=True)
    vc = v - mu
    var = jnp.mean(vc * vc, axis=-1, keepdims=True)
    ln = (vc * lax.rsqrt(var + EPS)) * lnw_ref[...] + lnb_ref[...]
    o_ref[...] = _silu(ln).astype(BF16)


def _conformer(proj, dw_w, dw_b, ln_w, ln_b, *, seg, ntok, row0):
    rows = UNIT
    blk0 = row0 // rows
    return pl.pallas_call(
        functools.partial(_conf_kernel, seg=seg),
        grid=(ntok // rows,),
        in_specs=[pl.BlockSpec((rows, 2 * D_CONF), lambda i: (blk0 + i, COL_GLU // (2 * D_CONF))),
                  pl.BlockSpec((CONF_KERNEL, D_CONF), lambda i: (0, 0)),
                  pl.BlockSpec((1, D_CONF), lambda i: (0, 0)),
                  pl.BlockSpec((1, D_CONF), lambda i: (0, 0)),
                  pl.BlockSpec((1, D_CONF), lambda i: (0, 0))],
        out_specs=pl.BlockSpec((rows, D_CONF), lambda i: (i, 0)),
        out_shape=jax.ShapeDtypeStruct((ntok, D_CONF), BF16),
        scratch_shapes=[pltpu.VMEM((rows // seg, seg + 32, D_CONF), F32),
                        pltpu.VMEM((8, rows // seg, seg + 24, D_CONF), F32),
                        pltpu.VMEM((rows, D_CONF), F32)],
        compiler_params=_cparams(1, 40 * 1024 * 1024),
        name="conformer_conv",
    )(proj, dw_w, dw_b.reshape(1, D_CONF), ln_w.reshape(1, D_CONF), ln_b.reshape(1, D_CONF))


def _pack_halves(x):
    outs = []
    for cb in range(x.shape[1] // (2 * LANES)):
        hi = x[:, cb * 2 * LANES:cb * 2 * LANES + LANES].astype(BF16).astype(F32)
        lo = x[:, cb * 2 * LANES + LANES:(cb + 1) * 2 * LANES].astype(BF16).astype(F32)
        hw = lax.bitcast_convert_type(hi, U32)
        lw = jnp.right_shift(lax.bitcast_convert_type(lo, U32), jnp.uint32(16))
        outs.append(jnp.bitwise_or(hw, lw))
    return jnp.concatenate(outs, axis=1)


def _mix_kernel(xc_ref, xl_ref, yc_ref, yl_ref, uc_ref, ul_ref, gate_ref, g1_ref, sc2_ref, sh2_ref,
                wssd_ref, wconf_ref, bconf_ref, bgate_ref, wo_ref, nffn_ref, wr_ref, br_ref,
                x1_ref, h2i_ref, er_ref, topp_ref, cnt_ref, cnt_scr, *, ctx_tiles):
    @pl.when(pl.program_id(0) == 0)
    def _():
        cnt_scr[...] = jnp.zeros(cnt_scr.shape, F32)

    is_ctx = pl.program_id(0) < ctx_tiles
    y_in = jnp.where(is_ctx, yc_ref[...], yl_ref[...])
    u_in = jnp.where(is_ctx, uc_ref[...], ul_ref[...])
    o_ssd = jnp.dot(y_in, wssd_ref[...], preferred_element_type=F32)
    o_conf = jnp.dot(u_in, wconf_ref[...], preferred_element_type=F32) + bconf_ref[...]
    gates = _sigmoid(gate_ref[...].astype(F32) + bgate_ref[...])
    merged = gates[:, :D_MODEL] * o_ssd + gates[:, D_MODEL:] * o_conf
    out = jnp.dot(merged.astype(BF16), wo_ref[...], preferred_element_type=F32)
    x1 = jnp.where(is_ctx, xc_ref[...], xl_ref[...]) + g1_ref[...] * out
    x1_ref[...] = x1
    ms = jnp.mean(x1 * x1, axis=-1, keepdims=True)
    h2 = (x1 * lax.rsqrt(ms + EPS)) * nffn_ref[...]
    h2 = h2 * (1.0 + sc2_ref[...]) + sh2_ref[...]
    packed = _pack_halves(h2)
    for qd in range(XW):
        h2i_ref[pl.ds(qd, packed.shape[0], stride=XW), :] = packed[:, qd * LANES:(qd + 1) * LANES]
    h_hi = h2.astype(BF16)
    h_lo = (h2 - h_hi.astype(F32)).astype(BF16)
    logits = jnp.dot(jnp.concatenate([h_hi, h_hi, h_lo], axis=1), wr_ref[...],
                     preferred_element_type=F32) + br_ref[...]
    rows = logits.shape[0]
    lane = lax.broadcasted_iota(jnp.int32, (rows, LANES), 1)
    lane_f = lane.astype(F32)
    neg_inf = jnp.float32(-jnp.inf)
    work = jnp.where(lane < N_EXPERTS, logits, neg_inf)
    vals, idxs = [], []
    for _ in range(TOP_K):
        m = jnp.max(work, axis=-1, keepdims=True)
        idx = jnp.min(jnp.where(work == m, lane_f, jnp.float32(LANES)), axis=-1, keepdims=True)
        vals.append(m)
        idxs.append(idx)
        work = jnp.where(lane_f == idx, neg_inf, work)
    es = [jnp.exp(v - vals[0]) for v in vals]
    denom = es[0] + es[1] + es[2] + es[3]
    member = jnp.zeros((rows, LANES), F32)
    for k in range(TOP_K):
        member = member + jnp.where(lane_f == idxs[k], 1.0, 0.0)
    r_i = lax.broadcasted_iota(jnp.int32, (rows, rows), 0)
    c_i = lax.broadcasted_iota(jnp.int32, (rows, rows), 1)
    earlier = jnp.where(c_i < r_i, 1.0, 0.0).astype(BF16)
    rank = jnp.dot(earlier, member.astype(BF16), preferred_element_type=F32) + cnt_scr[...]
    cnt = cnt_scr[...] + jnp.sum(member, axis=0, keepdims=True)
    cnt_scr[...] = cnt
    cnt_ref[...] = jnp.broadcast_to(cnt, cnt_ref.shape).astype(jnp.int32)
    er = jnp.zeros((rows, LANES), F32)
    topp = jnp.zeros((rows, LANES), F32)
    for k in range(TOP_K):
        rank_k = jnp.sum(jnp.where(lane_f == idxs[k], rank, 0.0), axis=-1, keepdims=True)
        er = jnp.where(lane == k, idxs[k] + N_EXPERTS * rank_k, er)
        topp = jnp.where(lane == k, es[k] / denom, topp)
    er_ref[...] = er.astype(jnp.int32)
    topp_ref[...] = topp


def _mix_out(x_ctx, x_lat, y_ctx, y_lat, u_ctx, u_lat, proj, modr, w_ssd, w_conf, b_conf, b_gate, w_o, norm_ffn,
             w_router3, b_router):
    t_ctx = x_ctx.shape[0]
    t = t_ctx + x_lat.shape[0]
    tm = 512
    upt = tm // UNIT
    ctx_tiles = t_ctx // tm
    full = lambda shape: pl.BlockSpec(shape, lambda m: (0,) * len(shape), pipeline_mode=pl.Buffered(1))
    mod = lambda which: pl.BlockSpec((None, None, 1, D_MODEL), lambda m: (m * upt, which, 0, 0))
    return pl.pallas_call(
        functools.partial(_mix_kernel, ctx_tiles=ctx_tiles),
        grid=(t // tm,),
        in_specs=_two_source_specs(tm, D_MODEL, ctx_tiles, 1) + _two_source_specs(tm, D_INNER, ctx_tiles, 1)
                 + _two_source_specs(tm, D_CONF, ctx_tiles, 1) + [
                  pl.BlockSpec((tm, 2 * D_MODEL), lambda m: (m, COL_GATE // (2 * D_MODEL))),
                  mod(2), mod(4), mod(3),
                  full((D_INNER, D_MODEL)), full((D_CONF, D_MODEL)), full((1, D_MODEL)),
                  full((1, 2 * D_MODEL)), full((D_MODEL, D_MODEL)), full((1, D_MODEL)),
                  full((3 * D_MODEL, LANES)), full((1, LANES))],
        out_specs=[pl.BlockSpec((tm, D_MODEL), lambda m: (m, 0)),
                   pl.BlockSpec((tm * XW, LANES), lambda m: (m, 0)),
                   pl.BlockSpec((tm, LANES), lambda m: (m, 0)),
                   pl.BlockSpec((tm, LANES), lambda m: (m, 0)),
                   pl.BlockSpec((8, LANES), lambda m: (0, 0))],
        out_shape=[jax.ShapeDtypeStruct((t, D_MODEL), F32),
                   jax.ShapeDtypeStruct((t * XW, LANES), U32),
                   jax.ShapeDtypeStruct((t, LANES), jnp.int32),
                   jax.ShapeDtypeStruct((t, LANES), F32),
                   jax.ShapeDtypeStruct((8, LANES), jnp.int32)],
        scratch_shapes=[pltpu.VMEM((1, LANES), F32)],
        compiler_params=_cparams(1, 56 * 1024 * 1024),
        name="mix_out_router",
    )(x_ctx, x_lat, y_ctx, y_lat, u_ctx, u_lat, proj, modr, modr, modr, w_ssd, w_conf, b_conf.reshape(1, D_MODEL),
      b_gate.reshape(1, 2 * D_MODEL), w_o, norm_ffn.reshape(1, D_MODEL), w_router3, b_router)


def _moe_kernel(off_ref, pair_ref, pb_ref, h2i_ref, wg_ref, wu_ref, wd_ref, bgu_ref, bd_ref, out_ref,
                acc, x_scr, y_scr, wg_b, wu_b, wd_b, sem):
    e = pl.program_id(0)
    j = pl.program_id(1)
    tr = MOE_TILE

    @pl.when(jnp.logical_and(e == 0, j == 0))
    def _():
        acc[...] = jnp.zeros(acc.shape, F32)
        y_scr[...] = jnp.zeros(y_scr.shape, F32)

    wg_b[...] = wg_ref[...].astype(BF16)
    wu_b[...] = wu_ref[...].astype(BF16)
    wd_b[...] = wd_ref[...].astype(BF16)

    start = off_ref[e]
    end = off_ref[e + 1]
    ntiles = (end - start + tr - 1) // tr
    bg = bgu_ref[:, pl.ds(pl.multiple_of(j * MOE_DH, MOE_DH), MOE_DH)]
    bu = bgu_ref[:, pl.ds(pl.multiple_of(D_EXPERT + j * MOE_DH, MOE_DH), MOE_DH)]
    bd = bd_ref[...] * jnp.where(j == 0, 1.0, 0.0)

    def gather_rows(base, slot, r0, n):
        for r in range(r0, r0 + n):
            src = jnp.bitwise_and(pair_ref[base + r], -TOP_K)
            x_scr[slot, pl.ds(r * XW, XW), :] = h2i_ref[pl.ds(pl.multiple_of(src, XW), XW), :]

    def scatter_rows(base, slot, r0, live):
        dsts, ps = [], []
        for rr in range(8):
            r = base + r0 + rr
            pair = pair_ref[jnp.maximum(r, 0)]
            dsts.append(pl.multiple_of(jnp.bitwise_and(pair, -TOP_K) * (AW // XW), AW))
            pw = lax.bitcast_convert_type(pb_ref[pair], F32)
            ps.append(jnp.where(jnp.logical_and(live, r < end), pw, 0.0))
        olds = [acc[pl.ds(dsts[rr], AW), :] for rr in range(8)]
        news = [olds[rr] + ps[rr] * y_scr[slot, pl.ds((r0 + rr) * AW, AW), :] for rr in range(8)]
        for rr in range(8):
            acc[pl.ds(dsts[rr], AW), :] = news[rr]

    @pl.when(ntiles > 0)
    def _():
        def first_gather(r8, c2):
            for rr in range(8):
                r = r8 * 8 + rr
                src = jnp.bitwise_and(pair_ref[start + r], -TOP_K)
                x_scr[0, pl.ds(pl.multiple_of(r * XW, XW), XW), :] = h2i_ref[pl.ds(pl.multiple_of(src, XW), XW), :]
            return c2

        lax.fori_loop(0, tr // 8, first_gather, 0)

    def tile_body(ti, carry):
        slot = jnp.bitwise_and(ti, 1)
        other = 1 - slot
        base = start + ti * tr
        parts = []
        for qd in range(XW):
            w = x_scr[slot, pl.ds(qd, tr, stride=XW), :]
            parts.append(lax.bitcast_convert_type(jnp.bitwise_and(w, jnp.uint32(0xFFFF0000)), F32))
            parts.append(lax.bitcast_convert_type(jnp.left_shift(w, jnp.uint32(16)), F32))
        x = jnp.concatenate(parts, axis=1).astype(BF16)
        for r0 in range(0, tr, 8):
            scatter_rows(base - tr, other, r0, ti > 0)
        gather_rows(base + tr, other, 0, tr)
        g = jnp.dot(x, wg_b[...], preferred_element_type=F32) + bg
        u = jnp.dot(x, wu_b[...], preferred_element_type=F32) + bu
        g = jnp.minimum(g, SWIGLU_LIMIT)
        u = jnp.clip(u, -SWIGLU_LIMIT, SWIGLU_LIMIT)
        act = (u + 1.0) * g * _sigmoid(SWIGLU_ALPHA * g)
        y = jnp.dot(act.astype(BF16), wd_b[...], preferred_element_type=F32) + bd
        for qd in range(AW):
            y_scr[slot, pl.ds(qd, tr, stride=AW), :] = y[:, qd * LANES:(qd + 1) * LANES]
        return carry

    lax.fori_loop(0, ntiles, tile_body, 0)

    @pl.when(ntiles > 0)
    def _():
        last = ntiles - 1
        lslot = jnp.bitwise_and(last, 1)
        lbase = start + last * tr

        def last_scatter(r8, c2):
            dsts, ps = [], []
            for rr in range(8):
                r = lbase + r8 * 8 + rr
                pair = pair_ref[r]
                dsts.append(pl.multiple_of(jnp.bitwise_and(pair, -TOP_K) * (AW // XW), AW))
                pw = lax.bitcast_convert_type(pb_ref[pair], F32)
                ps.append(jnp.where(r < end, pw, 0.0))
            olds = [acc[pl.ds(dsts[rr], AW), :] for rr in range(8)]
            news = [olds[rr] + ps[rr] * y_scr[lslot, pl.ds(pl.multiple_of((r8 * 8 + rr) * AW, AW), AW), :]
                    for rr in range(8)]
            for rr in range(8):
                acc[pl.ds(dsts[rr], AW), :] = news[rr]
            return c2

        lax.fori_loop(0, tr // 8, last_scatter, 0)

    @pl.when(jnp.logical_and(e == N_EXPERTS - 1, j == MOE_HALVES - 1))
    def _():
        cp = pltpu.make_async_copy(acc, out_ref, sem)
        cp.start()
        cp.wait()


def _moe_grouped(h2i, off, row_pair, pbits, w_gu, b_gu, w_down, b_down):
    t = h2i.shape[0] // XW
    grid_spec = pltpu.PrefetchScalarGridSpec(
        num_scalar_prefetch=3,
        grid=(N_EXPERTS, MOE_HALVES),
        in_specs=[pl.BlockSpec((t * XW, LANES), lambda e, j, *_: (0, 0), pipeline_mode=pl.Buffered(1)),
                  pl.BlockSpec((None, D_MODEL, MOE_DH), lambda e, j, *_: (e, 0, j)),
                  pl.BlockSpec((None, D_MODEL, MOE_DH), lambda e, j, *_: (e, 0, MOE_HALVES + j)),
                  pl.BlockSpec((None, MOE_DH, D_MODEL), lambda e, j, *_: (e, j, 0)),
                  pl.BlockSpec((None, 1, 2 * D_EXPERT), lambda e, j, *_: (e, 0, 0)),
                  pl.BlockSpec((None, 1, D_MODEL), lambda e, j, *_: (e, 0, 0))],
        out_specs=pl.BlockSpec(memory_space=pl.ANY),
        scratch_shapes=[pltpu.VMEM((t * AW, LANES), F32),
                        pltpu.VMEM((2, MOE_TILE * XW, LANES), U32),
                        pltpu.VMEM((2, MOE_TILE * AW, LANES), F32),
                        pltpu.VMEM((D_MODEL, MOE_DH), BF16),
                        pltpu.VMEM((D_MODEL, MOE_DH), BF16),
                        pltpu.VMEM((MOE_DH, D_MODEL), BF16),
                        pltpu.SemaphoreType.DMA(())])
    return pl.pallas_call(
        _moe_kernel,
        grid_spec=grid_spec,
        out_shape=jax.ShapeDtypeStruct((t * AW, LANES), F32),
        compiler_params=_cparams(2, VMEM_LIMIT),
        name="moe_grouped",
    )(off, row_pair, pbits, h2i, w_gu, w_gu, w_down, b_gu.reshape(N_EXPERTS, 1, 2 * D_EXPERT),
      b_down.reshape(N_EXPERTS, 1, D_MODEL))


def _residual_kernel(x1_ref, moe_ref, g2_ref, nw_ref, oc_ref, ol_ref, *, normalize, ctx_tiles):
    tm = x1_ref.shape[0]
    moe = jnp.concatenate([moe_ref[pl.ds(qd, tm, stride=AW), :] for qd in range(AW)], axis=1)
    x2 = x1_ref[...] + g2_ref[...] * moe
    if normalize:
        ms = jnp.mean(x2 * x2, axis=-1, keepdims=True)
        x2 = (x2 * lax.rsqrt(ms + EPS)) * nw_ref[...]

    @pl.when(pl.program_id(0) < ctx_tiles)
    def _():
        oc_ref[...] = x2

    @pl.when(pl.program_id(0) >= ctx_tiles)
    def _():
        ol_ref[...] = x2


def _residual(x1, moe_i, modr, norm_w, *, normalize, t_ctx):
    t = x1.shape[0]
    tm = 512
    upt = tm // UNIT
    ctx_tiles = t_ctx // tm
    return pl.pallas_call(
        functools.partial(_residual_kernel, normalize=normalize, ctx_tiles=ctx_tiles),
        grid=(t // tm,),
        in_specs=[pl.BlockSpec((tm, D_MODEL), lambda m: (m, 0)),
                  pl.BlockSpec((tm * AW, LANES), lambda m: (m, 0)),
                  pl.BlockSpec((None, None, 1, D_MODEL), lambda m: (m * upt, 5, 0, 0)),
                  pl.BlockSpec((1, D_MODEL), lambda m: (0, 0))],
        out_specs=_two_source_specs(tm, D_MODEL, ctx_tiles, 1),
        out_shape=[jax.ShapeDtypeStruct((t_ctx, D_MODEL), F32),
                   jax.ShapeDtypeStruct((t - t_ctx, D_MODEL), F32)],
        compiler_params=_cparams(1),
        name="ffn_residual_norm",
    )(x1, moe_i, modr, norm_w.reshape(1, D_MODEL))


def _route_kernel(er_ref, cnt_ref, off_ref, pair_ref, offs, *, n_pairs):
    def prefix(e, run):
        offs[e] = run
        off_ref[e] = run
        return run + cnt_ref[e]

    total = lax.fori_loop(0, N_EXPERTS, prefix, jnp.int32(0))
    for e in range(N_EXPERTS, off_ref.shape[0]):
        off_ref[e] = total

    def pad(i, c):
        pair_ref[n_pairs + i] = 0
        return c

    lax.fori_loop(0, ROUTE_PAD, pad, 0)

    def place(i8, c):
        for ii in range(8):
            i = i8 * 8 + ii
            er = er_ref[i]
            pos = offs[jnp.bitwise_and(er, N_EXPERTS - 1)] + lax.shift_right_logical(er, RANK_SHIFT)
            pair_ref[pos] = i
        return c

    lax.fori_loop(0, n_pairs // 8, place, 0)


def _route(er_flat, cnt):
    n = er_flat.shape[0]
    smem = pl.BlockSpec(memory_space=pltpu.SMEM)
    return pl.pallas_call(
        functools.partial(_route_kernel, n_pairs=n),
        in_specs=[smem, smem],
        out_specs=[smem, smem],
        out_shape=[jax.ShapeDtypeStruct((N_EXPERTS + 8,), jnp.int32),
                   jax.ShapeDtypeStruct((n + ROUTE_PAD,), jnp.int32)],
        scratch_shapes=[pltpu.SMEM((N_EXPERTS,), jnp.int32)],
        name="moe_route",
    )(er_flat, cnt)


def kernel(x_prompt, x_sample, state_ssm, c, c_ctx, w_ada, b_ada, norm_mix, norm_ffn, w_in, ssm_conv_w, ssm_conv_b, dt_bias, a_log, d_skip, ssm_norm_w, w_ssd_out, conf_dw_w, conf_dw_b, conf_ln_w, conf_ln_b, w_conf_out, b_conf_out, b_gate, w_o, w_router, b_router, w_gu, b_gu, w_down, b_down, norm_final):
    n_ctx, seq_ctx, _ = x_prompt.shape
    n_lat, seq_lat, _ = x_sample.shape
    depth = w_in.shape[0]
    t_ctx, t_lat = n_ctx * seq_ctx, n_lat * seq_lat
    t_all = t_ctx + t_lat
    assert seq_ctx == UNIT and seq_lat % 1024 == 0 and t_ctx % 1024 == 0
    assert n_lat + 1 <= 8 and seq_lat % GRID_W == 0 and UNIT % GRID_W == 0

    x_ctx, x_lat = x_prompt.reshape(t_ctx, D_MODEL), x_sample.reshape(t_lat, D_MODEL)
    cvec =jnp.concatenate([c_ctx[None, :], c, jnp.zeros((8 - 1 - n_lat, D_MODEL), F32)], axis=0)
    unit_row = jnp.concatenate([jnp.zeros((t_ctx // UNIT,), jnp.int32),
                                1 + jnp.arange(t_lat // UNIT, dtype=jnp.int32) // (seq_lat // UNIT)])
    assert XW == TOP_K and N_EXPERTS == 1 << RANK_SHIFT

    new_states = []
    for l in range(depth):
        mod = _ada_mod(cvec, w_ada[l], b_ada[l])
        modr = mod[unit_row].reshape(t_all // UNIT, 6, 1, D_MODEL)

        wl = w_in[l]
        w_main = jnp.concatenate([wl[:, D_INNER:D_INNER + D_XBC], wl[:, :D_INNER],
                                  wl[:, D_INNER + D_XBC + 2 * N_HEADS:]], axis=1).astype(BF16)
        dt0 = D_INNER + D_XBC
        zpad = jnp.zeros((D_MODEL, LANES - N_HEADS), F32)
        w_dt = jnp.concatenate([wl[:, dt0:dt0 + N_HEADS], zpad,
                                wl[:, dt0 + N_HEADS:dt0 + 2 * N_HEADS], zpad], axis=1).astype(BF16)
        proj, dt_raw = _in_proj(x_ctx, x_lat, norm_mix[l], modr, w_main, w_dt)

        pad_h = lambda v: jnp.pad(v.reshape(2, 1, N_HEADS), ((0, 0), (0, 0), (0, LANES - N_HEADS)))
        dtb, alog = pad_h(dt_bias[l]), pad_h(a_log[l])
        dskip = jnp.repeat(d_skip[l], HEAD_DIM).reshape(1, D_INNER)
        nw = ssm_norm_w[l].reshape(1, D_INNER)
        h0t = jnp.swapaxes(state_ssm[:, l].astype(F32).reshape(n_lat, 2, D_INNER, D_STATE), 2, 3)

        xbc_ctx = _ssm_conv(proj, ssm_conv_w[l], ssm_conv_b[l], seq=seq_ctx, nseq=n_ctx, row0=0)
        xbc_lat = _ssm_conv(proj, ssm_conv_w[l], ssm_conv_b[l], seq=seq_lat, nseq=n_lat, row0=t_ctx)
        y_ctx, st_ctx = _ssd(xbc_ctx, proj, dt_raw, dtb, alog, dskip, nw, None,
                             seq=seq_ctx, nseq=n_ctx, row0=0, write_state=True)
        (y_lat,) = _ssd(xbc_lat, proj, dt_raw, dtb, alog, dskip, nw, h0t,
                        seq=seq_lat, nseq=n_lat, row0=t_ctx, write_state=False)
        u_ctx = _conformer(proj, conf_dw_w[l], conf_dw_b[l], conf_ln_w[l], conf_ln_b[l],
                           seg=seq_ctx, ntok=t_ctx, row0=0)
        u_lat = _conformer(proj, conf_dw_w[l], conf_dw_b[l], conf_ln_w[l], conf_ln_b[l],
                           seg=GRID_W, ntok=t_lat, row0=t_ctx)
        wr = jnp.pad(w_router[l], ((0, 0), (0, LANES - N_EXPERTS)))
        wr_hi = wr.astype(BF16)
        wr_lo = (wr - wr_hi.astype(F32)).astype(BF16)
        br = jnp.pad(b_router[l], (0, LANES - N_EXPERTS)).reshape(1, LANES)
        x1, h2i, er, topp, cnt = _mix_out(
            x_ctx, x_lat, y_ctx, y_lat, u_ctx, u_lat, proj, modr, w_ssd_out[l].astype(BF16),
            w_conf_out[l].astype(BF16), b_conf_out[l], b_gate[l], w_o[l].astype(BF16), norm_ffn[l],
            jnp.concatenate([wr_hi, wr_lo, wr_hi], axis=0), br)

        off, row_pair = _route(er[:, :TOP_K].reshape(-1), cnt[0, :N_EXPERTS])
        pbits = lax.bitcast_convert_type(topp[:, :TOP_K], jnp.int32).reshape(-1)
        moe_i = _moe_grouped(h2i, off, row_pair, pbits, w_gu[l], b_gu[l], w_down[l], b_down[l])
        last = l + 1 == depth
        x_ctx, x_lat = _residual(x1, moe_i, modr, norm_final if last else norm_ffn[l], normalize=last, t_ctx=t_ctx)
        new_states.append(st_ctx.reshape(n_ctx, 1, 2, N_HEADS, HEAD_DIM, D_STATE))

    y_prompt = x_ctx.reshape(n_ctx, seq_ctx, D_MODEL)
    y_sample = x_lat.reshape(n_lat, seq_lat, D_MODEL)
    new_state_ssm = new_states[0] if depth == 1 else jnp.concatenate(new_states, axis=1)
    return (y_prompt, y_sample, new_state_ssm)
```

```python
import functools

import jax
import jax.numpy as jnp
from jax import lax
from jax.experimental import pallas as pl
from jax.experimental.pallas import tpu as pltpu

F32 = jnp.float32
BF16 = jnp.bfloat16
U32 = jnp.uint32

D_MODEL = 1024
GRID_W = 64
D_INNER = 2 * D_MODEL
HEAD_DIM = 64
N_HEADS = D_INNER // HEAD_DIM
N_GROUPS = 8
HEADS_PER_GROUP = N_HEADS // N_GROUPS
D_STATE = 128
SSM_CONV = 5
CHUNK = 128
D_XBC = D_INNER + 2 * N_GROUPS * D_STATE
D_CONF = D_MODEL
CONF_KERNEL = 31
N_EXPERTS = 32
TOP_K = 4
D_EXPERT = D_MODEL
SWIGLU_LIMIT = 7.0
SWIGLU_ALPHA = 1.702
EPS = 1e-6

LANES = 128
UNIT = 256
GROUP_W = D_INNER // N_GROUPS
SSD_GROUP_UNROLL = 4
MOE_TILE = 128
ROUTE_PAD = 2 * MOE_TILE
MOE_CAST_ROWS = 16
XW = D_MODEL // (2 * LANES)
AW = D_MODEL // LANES
RANK_SHIFT = 5
VMEM_LIMIT = 62 * 1024 * 1024

COL_Z = 0
COL_XBC = D_INNER
COL_GLU = D_XBC + D_INNER
COL_GATE = D_XBC + D_INNER + 2 * D_CONF
N_MAIN = D_XBC + D_INNER + 2 * D_CONF + 2 * D_MODEL


def _sigmoid(x):
    return 1.0 / (1.0 + jnp.exp(-x))


def _silu(x):
    return x * _sigmoid(x)


def _cparams(n_axes, vmem=None):
    return pltpu.CompilerParams(
        dimension_semantics=("arbitrary",) * n_axes,
        vmem_limit_bytes=vmem)


def _ada_kernel(c_ref, w_ref, b_ref, o_ref):
    s = _silu(c_ref[...])
    o_ref[...] = jnp.dot(s.astype(BF16), w_ref[...].astype(BF16),
                         preferred_element_type=F32) + b_ref[...]


def _ada_mod(cvec, w_ada, b_ada):
    n = w_ada.shape[1]
    tn = 1536
    return pl.pallas_call(
        _ada_kernel,
        grid=(n // tn,),
        in_specs=[pl.BlockSpec((8, D_MODEL), lambda j: (0, 0)),
                  pl.BlockSpec((D_MODEL, tn), lambda j: (0, j)),
                  pl.BlockSpec((1, tn), lambda j: (0, j))],
        out_specs=pl.BlockSpec((8, tn), lambda j: (0, j)),
        out_shape=jax.ShapeDtypeStruct((8, n), F32),
        compiler_params=_cparams(1, 40 * 1024 * 1024),
        name="ada_mod",
    )(cvec, w_ada, b_ada.reshape(1, n))


def _inproj_kernel(xc_ref, xl_ref, nw_ref, sc_ref, sh_ref, wa_ref, wb_ref, wdt_ref, o_ref, dt_ref, h_scr, *,
                   ctx_tiles, a_tiles):
    @pl.when(pl.program_id(1) == 0)
    def _():
        xf = jnp.where(pl.program_id(0) < ctx_tiles, xc_ref[...], xl_ref[...])
        ms = jnp.mean(xf * xf, axis=-1, keepdims=True)
        hn = (xf * lax.rsqrt(ms + EPS)) * nw_ref[...]
        hn = hn * (1.0 + sc_ref[...]) + sh_ref[...]
        hb = hn.astype(BF16)
        h_scr[...] = hb
        dt_ref[...] = jnp.dot(hb, wdt_ref[...], preferred_element_type=F32)

    @pl.when(pl.program_id(1) < a_tiles)
    def _():
        o_ref[...] = jnp.dot(h_scr[...], wa_ref[...], preferred_element_type=F32).astype(BF16)

    @pl.when(pl.program_id(1) >= a_tiles)
    def _():
        o_ref[...] = jnp.dot(h_scr[...], wb_ref[...], preferred_element_type=F32).astype(BF16)


def _two_source_specs(tm, width, ctx_tiles, nargs):
    if nargs == 1:
        return [pl.BlockSpec((tm, width), lambda m: (jnp.minimum(m, ctx_tiles - 1), 0)),
                pl.BlockSpec((tm, width), lambda m: (jnp.maximum(m - ctx_tiles, 0), 0))]
    return [pl.BlockSpec((tm, width), lambda m, n: (jnp.minimum(m, ctx_tiles - 1), 0)),
            pl.BlockSpec((tm, width), lambda m, n: (jnp.maximum(m - ctx_tiles, 0), 0))]


def _in_proj(x_ctx, x_lat, norm_w, modr, w_a, w_b, w_dt):
    t_ctx = x_ctx.shape[0]
    t = t_ctx + x_lat.shape[0]
    tm, tn = 1024, 2048
    upt = tm // UNIT
    a_tiles = w_a.shape[1] // tn
    return pl.pallas_call(
        functools.partial(_inproj_kernel, ctx_tiles=t_ctx // tm, a_tiles=a_tiles),
        grid=(t // tm, N_MAIN // tn),
        in_specs=_two_source_specs(tm, D_MODEL, t_ctx // tm, 2) + [
                  pl.BlockSpec((1, D_MODEL), lambda m, n: (0, 0)),
                  pl.BlockSpec((None, None, 1, D_MODEL), lambda m, n: (m * upt, 1, 0, 0)),
                  pl.BlockSpec((None, None, 1, D_MODEL), lambda m, n: (m * upt, 0, 0, 0)),
                  pl.BlockSpec((D_MODEL, tn), lambda m, n: (0, jnp.minimum(n, a_tiles - 1))),
                  pl.BlockSpec((D_MODEL, tn), lambda m, n: (0, jnp.maximum(n - a_tiles, 0))),
                  pl.BlockSpec((D_MODEL, 2 * LANES), lambda m, n: (0, 0))],
        out_specs=[pl.BlockSpec((tm, tn), lambda m, n: (m, n)),
                   pl.BlockSpec((tm, 2 * LANES), lambda m, n: (m, 0))],
        out_shape=[jax.ShapeDtypeStruct((t, N_MAIN), BF16),
                   jax.ShapeDtypeStruct((t, 2 * LANES), F32)],
        scratch_shapes=[pltpu.VMEM((tm, D_MODEL), BF16)],
        compiler_params=_cparams(2, 48 * 1024 * 1024),
        name="in_proj",
    )(x_ctx, x_lat, norm_w.reshape(1, D_MODEL), modr, modr, w_a, w_b, w_dt)


def _ssm_conv_kernel(x_ref, w_ref, b_ref, o_ref, pad_scr, *, seq):
    cb = x_ref.shape[1]
    pad_scr[0:8, :] = jnp.zeros((8, cb), F32)
    pad_scr[8 + seq:16 + seq, :] = jnp.zeros((8, cb), F32)
    pad_scr[8:8 + seq, :] = x_ref[...].astype(F32)
    half = SSM_CONV // 2
    rows, cw = 256, 512
    for c0 in range(0, cb, cw):
        cs = slice(c0, c0 + cw)
        for r0 in range(0, seq, rows):
            acc = jnp.broadcast_to(b_ref[:, cs], (rows, cw))
            for k in range(SSM_CONV):
                acc = acc + w_ref[k:k + 1, cs] * pad_scr[8 + r0 + k - half:8 + r0 + k - half + rows, cs]
            o_ref[r0:r0 + rows, cs] = _silu(acc).astype(BF16)


def _ssm_conv(proj, conv_w, conv_b, *, seq, nseq, row0):
    cb = 2048
    blk0 = row0 // seq
    return pl.pallas_call(
        functools.partial(_ssm_conv_kernel, seq=seq),
        grid=(nseq, D_XBC // cb),
        in_specs=[pl.BlockSpec((seq, cb), lambda s, j: (blk0 + s, COL_XBC // cb + j)),
                  pl.BlockSpec((SSM_CONV, cb), lambda s, j: (0, j)),
                  pl.BlockSpec((1, cb), lambda s, j: (0, j))],
        out_specs=pl.BlockSpec((seq, cb), lambda s, j: (s, j)),
        out_shape=jax.ShapeDtypeStruct((nseq * seq, D_XBC), BF16),
        scratch_shapes=[pltpu.VMEM((seq + 16, cb), F32)],
        compiler_params=_cparams(2, 40 * 1024 * 1024),
        name="ssm_conv",
    )(proj, conv_w, conv_b.reshape(1, D_XBC))


def _split_bf16(v):
    hi = v.astype(BF16)
    lo = (v - hi.astype(F32)).astype(BF16)
    return jnp.concatenate([hi, lo], axis=1)


def _head_select_matrices():
    j = jnp.arange(2 * LANES, dtype=jnp.int32)[:, None] % LANES
    full = (j == (jnp.arange(N_HEADS * LANES, dtype=jnp.int32)[None, :] // LANES)).astype(BF16)
    exp = (j == (jnp.arange(D_INNER, dtype=jnp.int32)[None, :] // HEAD_DIM)).astype(BF16)
    return full, exp


def _ssd_kernel(*refs, nc, has_h0, write_state):
    (xs_ref, b_ref, c_ref, z_ref, dtr_ref, dtb_ref, alog_ref, dskip_ref, nw_ref, self_ref, sele_ref), rest = \
        refs[:11], refs[11:]
    if has_h0:
        h0_ref, rest = rest[0], rest[1:]
    y_ref, rest = rest[0], rest[1:]
    if write_state:
        st_ref, rest = rest[0], rest[1:]
    h_scr, ybuf, ychunk, colb_scr, wexp_scr, eexp_scr, texp_scr, rowq_scr = rest

    q = CHUNK
    phase = pl.program_id(1)
    c = pl.program_id(2)
    is_fwd = phase == 1
    c_eff = jnp.where(is_fwd, c, nc - 1 - c)

    @pl.when(c == 0)
    def _():
        if has_h0:
            for j in range(D_INNER // LANES):
                h_scr[:, j * LANES:(j + 1) * LANES] = h0_ref[j * LANES:(j + 1) * LANES, :].T
        else:
            h_scr[...] = jnp.zeros(h_scr.shape, F32)

    xdt = dtr_ref[...] + dtb_ref[...]
    dt = jnp.maximum(xdt, 0.0) + jnp.log(1.0 + jnp.exp(-jnp.abs(xdt)))
    a = -jnp.exp(alog_ref[...])
    adt = dt * a
    row = lax.broadcasted_iota(jnp.int32, (q, q), 0)
    col = lax.broadcasted_iota(jnp.int32, (q, q), 1)
    tri = (col - row) * jnp.where(is_fwd, 1, -1) <= 0
    cum2 = jnp.dot(tri.astype(BF16), _split_bf16(adt), preferred_element_type=F32)
    cum = cum2[:, :LANES] + cum2[:, LANES:]
    tot = jnp.where(is_fwd, cum[q - 1:q, :], cum[0:1, :])
    rowq_scr[...] = (cum - jnp.log(dt)).T
    wdec = dt * jnp.exp(tot - cum)
    eo = jnp.exp(cum)
    etot = jnp.broadcast_to(jnp.exp(tot), (8, LANES))
    colb_scr[...] = jnp.dot(_split_bf16(cum), self_ref[...], preferred_element_type=F32)
    expanded = jnp.dot(jnp.concatenate([_split_bf16(wdec), _split_bf16(eo), _split_bf16(etot)], axis=0),
                       sele_ref[...], preferred_element_type=F32)
    wexp_scr[...] = expanded[0:q]
    eexp_scr[...] = expanded[q:2 * q]
    texp_scr[...] = expanded[2 * q:2 * q + 8]
    lane_g = lax.broadcasted_iota(jnp.int32, (1, GROUP_W), 1)
    head_mask = [(lane_g // HEAD_DIM == hh).astype(BF16) for hh in range(HEADS_PER_GROUP)]
    neg_inf = jnp.float32(-jnp.inf)

    def group_body(g, carry):
        gs = pl.ds(pl.multiple_of(g * GROUP_W, GROUP_W), GROUP_W)
        ns = pl.ds(pl.multiple_of(g * D_STATE, D_STATE), D_STATE)
        bg = b_ref[:, ns]
        cg = c_ref[:, ns]
        scores = lax.dot_general(cg, bg, (((1,), (1,)), ((), ())), preferred_element_type=F32)
        xs_g = xs_ref[:, gs]
        ms, xb = [], []
        for hh in range(HEADS_PER_GROUP):
            h = HEADS_PER_GROUP * g + hh
            colb = colb_scr[:, pl.ds(pl.multiple_of(h * LANES, LANES), LANES)]
            seg = jnp.where(tri, colb - rowq_scr[pl.ds(h, 1), :], neg_inf)
            ms.append((scores * jnp.exp(seg)).astype(BF16))
            xb.append(xs_g * head_mask[hh])
        y_diag = jnp.dot(jnp.concatenate(ms, axis=1), jnp.concatenate(xb, axis=0),
                         preferred_element_type=F32)
        xd = (xs_g.astype(F32) * wexp_scr[:, gs]).astype(BF16)
        bt = bg.astype(F32).T.astype(BF16)
        st = jnp.dot(bt, xd, preferred_element_type=F32)
        hg = h_scr[:, gs]
        y_off = jnp.dot(cg, hg.astype(BF16), preferred_element_type=F32) * eexp_scr[:, gs]
        ychunk[:, gs] = y_diag + y_off
        h_scr[:, gs] = hg * texp_scr[0:1, gs] + st
        return carry

    lax.fori_loop(0, N_GROUPS, group_body, 0, unroll=SSD_GROUP_UNROLL)

    rows = pl.ds(pl.multiple_of(c_eff * q, q), q)

    @pl.when(jnp.logical_not(is_fwd))
    def _():
        ybuf[rows, :] = ychunk[...]

    @pl.when(is_fwd)
    def _():
        zf = z_ref[...].astype(F32)
        yt = ychunk[...] + ybuf[rows, :] + xs_ref[...].astype(F32) * dskip_ref[...]
        yz = yt * _silu(zf)
        for g in range(N_GROUPS):
            gs = slice(g * GROUP_W, (g + 1) * GROUP_W)
            blk = yz[:, gs]
            ms = jnp.mean(blk * blk, axis=-1, keepdims=True)
            y_ref[:, gs] = (blk * lax.rsqrt(ms + EPS) * nw_ref[:, gs]).astype(BF16)

    if write_state:
        @pl.when(c == nc - 1)
        def _():
            for j in range(D_INNER // LANES):
                st_ref[j * LANES:(j + 1) * LANES, :] = h_scr[:, j * LANES:(j + 1) * LANES].T


def _ssd(xbc_c, proj, dt_raw, dt_bias, a_log, d_skip, norm_w, h0t, *, seq, nseq, row0, write_state):
    nc = seq // CHUNK
    blk0 = row0 // CHUNK
    has_h0 = h0t is not None

    def tok(s, p, c):
        return s * nc + p * c + (1 - p) * (nc - 1 - c)

    in_specs = [
        pl.BlockSpec((CHUNK, D_INNER), lambda s, p, c: (tok(s, p, c), 0)),
        pl.BlockSpec((CHUNK, N_GROUPS * D_STATE), lambda s, p, c: (tok(s, p, c), 2)),
        pl.BlockSpec((CHUNK, N_GROUPS * D_STATE), lambda s, p, c: (tok(s, p, c), 3)),
        pl.BlockSpec((CHUNK, D_INNER), lambda s, p, c: (blk0 + s * nc + p * c, COL_Z // D_INNER)),
        pl.BlockSpec((CHUNK, LANES), lambda s, p, c: (blk0 + tok(s, p, c), 1 - p)),
        pl.BlockSpec((None, 1, LANES), lambda s, p, c: (1 - p, 0, 0)),
        pl.BlockSpec((None, 1, LANES), lambda s, p, c: (1 - p, 0, 0)),
        pl.BlockSpec((1, D_INNER), lambda s, p, c: (0, 0)),
        pl.BlockSpec((1, D_INNER), lambda s, p, c: (0, 0)),
        pl.BlockSpec((2 * LANES, N_HEADS * LANES), lambda s, p, c: (0, 0)),
        pl.BlockSpec((2 * LANES, D_INNER), lambda s, p, c: (0, 0)),
    ]
    sel_full, sel_exp = _head_select_matrices()
    args = [xbc_c, xbc_c, xbc_c, proj, dt_raw, dt_bias, a_log, d_skip, norm_w, sel_full, sel_exp]
    if has_h0:
        in_specs.append(pl.BlockSpec((None, None, D_INNER, D_STATE), lambda s, p, c: (s, 1 - p, 0, 0)))
        args.append(h0t)
    out_specs = [pl.BlockSpec((CHUNK, D_INNER), lambda s, p, c: (s * nc + p * c, 0))]
    out_shape = [jax.ShapeDtypeStruct((nseq * seq, D_INNER), BF16)]
    if write_state:
        out_specs.append(pl.BlockSpec((None, None, D_INNER, D_STATE), lambda s, p, c: (s, 1 - p, 0, 0)))
        out_shape.append(jax.ShapeDtypeStruct((nseq, 2, D_INNER, D_STATE), F32))
    return pl.pallas_call(
        functools.partial(_ssd_kernel, nc=nc, has_h0=has_h0, write_state=write_state),
        grid=(nseq, 2, nc),
        in_specs=in_specs,
        out_specs=out_specs,
        out_shape=out_shape,
        scratch_shapes=[pltpu.VMEM((D_STATE, D_INNER), F32),
                        pltpu.VMEM((seq, D_INNER), F32),
                        pltpu.VMEM((CHUNK, D_INNER), F32),
                        pltpu.VMEM((CHUNK, N_HEADS * LANES), F32),
                        pltpu.VMEM((CHUNK, D_INNER), F32),
                        pltpu.VMEM((CHUNK, D_INNER), F32),
                        pltpu.VMEM((8, D_INNER), F32),
                        pltpu.VMEM((LANES, CHUNK), F32)],
        compiler_params=_cparams(3, 48 * 1024 * 1024),
        name="ssd_scan",
    )(*args)


def _conf_kernel(glu_ref, w_ref, b_ref, lnw_ref, lnb_ref, o_ref, pad_scr, sh_scr, conv_scr, *, seg):
    rows = glu_ref.shape[0]
    nseg = rows // seg
    half = CONF_KERNEL // 2
    front = 16
    span = seg + 24
    a = glu_ref[:, :D_CONF].astype(F32)
    b = glu_ref[:, D_CONF:].astype(F32)
    u = a * _sigmoid(b)
    for i in range(nseg):
        pad_scr[i, 0:front, :] = jnp.zeros((front, D_CONF), F32)
        pad_scr[i, front + seg:front + seg + 16, :] = jnp.zeros((16, D_CONF), F32)
        pad_scr[i, front:front + seg, :] = u[i * seg:(i + 1) * seg, :]
    for s in range(8):
        for i in range(nseg):
            for cbi in range(D_CONF // LANES):
                cs = slice(cbi * LANES, (cbi + 1) * LANES)
                for r0 in range(0, span, 56):
                    n = min(56, span - r0)
                    sh_scr[s, i, r0:r0 + n, cs] = pad_scr[i, s + r0:s + r0 + n, cs]
    rb = 64
    for i in range(nseg):
        for cbi in range(D_CONF // LANES):
            cs = slice(cbi * LANES, (cbi + 1) * LANES)
            for r0 in range(0, seg, rb):
                acc = jnp.broadcast_to(b_ref[:, cs], (rb, LANES))
                for k in range(CONF_KERNEL):
                    start = front + r0 + k - half
                    al = start - start % 8
                    acc = acc + w_ref[k:k + 1, cs] * sh_scr[start % 8, i, al:al + rb, cs]
                conv_scr[i * seg + r0:i * seg + r0 + rb, cs] = acc
    v = conv_scr[...]
    mu = jnp.mean(v, axis=-1, keepdims=True)
    vc = v - mu
    var = jnp.mean(vc * vc, axis=-1, keepdims=True)
    ln = (vc * lax.rsqrt(var + EPS)) * lnw_ref[...] + lnb_ref[...]
    o_ref[...] = _silu(ln).astype(BF16)


def _conformer(proj, dw_w, dw_b, ln_w, ln_b, *, seg, ntok, row0):
    rows = UNIT
    blk0 = row0 // rows
    return pl.pallas_call(
        functools.partial(_conf_kernel, seg=seg),
        grid=(ntok // rows,),
        in_specs=[pl.BlockSpec((rows, 2 * D_CONF), lambda i: (blk0 + i, COL_GLU // (2 * D_CONF))),
                  pl.BlockSpec((CONF_KERNEL, D_CONF), lambda i: (0, 0)),
                  pl.BlockSpec((1, D_CONF), lambda i: (0, 0)),
                  pl.BlockSpec((1, D_CONF), lambda i: (0, 0)),
                  pl.BlockSpec((1, D_CONF), lambda i: (0, 0))],
        out_specs=pl.BlockSpec((rows, D_CONF), lambda i: (i, 0)),
        out_shape=jax.ShapeDtypeStruct((ntok, D_CONF), BF16),
        scratch_shapes=[pltpu.VMEM((rows // seg, seg + 32, D_CONF), F32),
                        pltpu.VMEM((8, rows // seg, seg + 24, D_CONF), F32),
                        pltpu.VMEM((rows, D_CONF), F32)],
        compiler_params=_cparams(1, 40 * 1024 * 1024),
        name="conformer_conv",
    )(proj, dw_w, dw_b.reshape(1, D_CONF), ln_w.reshape(1, D_CONF), ln_b.reshape(1, D_CONF))


def _pack_halves(x):
    outs = []
    for cb in range(x.shape[1] // (2 * LANES)):
        hi = x[:, cb * 2 * LANES:cb * 2 * LANES + LANES].astype(BF16).astype(F32)
        lo = x[:, cb * 2 * LANES + LANES:(cb + 1) * 2 * LANES].astype(BF16).astype(F32)
        hw = lax.bitcast_convert_type(hi, U32)
        lw = jnp.right_shift(lax.bitcast_convert_type(lo, U32), jnp.uint32(16))
        outs.append(jnp.bitwise_or(hw, lw))
    return jnp.concatenate(outs, axis=1)


def _mix_kernel(xc_ref, xl_ref, yc_ref, yl_ref, uc_ref, ul_ref, gate_ref, g1_ref, sc2_ref, sh2_ref,
                wssd_ref, wconf_ref, bconf_ref, bgate_ref, wo_ref, nffn_ref, wr_ref, br_ref,
                x1_ref, h2i_ref, er_ref, topp_ref, cnt_ref, cnt_scr, *, ctx_tiles):
    @pl.when(pl.program_id(0) == 0)
    def _():
        cnt_scr[...] = jnp.zeros(cnt_scr.shape, F32)

    is_ctx = pl.program_id(0) < ctx_tiles
    y_in = jnp.where(is_ctx, yc_ref[...], yl_ref[...])
    u_in = jnp.where(is_ctx, uc_ref[...], ul_ref[...])
    o_ssd = jnp.dot(y_in, wssd_ref[...], preferred_element_type=F32)
    o_conf = jnp.dot(u_in, wconf_ref[...], preferred_element_type=F32) + bconf_ref[...]
    gates = _sigmoid(gate_ref[...].astype(F32) + bgate_ref[...])
    merged = gates[:, :D_MODEL] * o_ssd + gates[:, D_MODEL:] * o_conf
    out = jnp.dot(merged.astype(BF16), wo_ref[...], preferred_element_type=F32)
    x1 = jnp.where(is_ctx, xc_ref[...], xl_ref[...]) + g1_ref[...] * out
    x1_ref[...] = x1
    ms = jnp.mean(x1 * x1, axis=-1, keepdims=True)
    h2 = (x1 * lax.rsqrt(ms + EPS)) * nffn_ref[...]
    h2 = h2 * (1.0 + sc2_ref[...]) + sh2_ref[...]
    packed = _pack_halves(h2)
    for qd in range(XW):
        h2i_ref[pl.ds(qd, packed.shape[0], stride=XW), :] = packed[:, qd * LANES:(qd + 1) * LANES]
    h_hi = h2.astype(BF16)
    h_lo = (h2 - h_hi.astype(F32)).astype(BF16)
    logits = jnp.dot(jnp.concatenate([h_hi, h_hi, h_lo], axis=1), wr_ref[...],
                     preferred_element_type=F32) + br_ref[...]
    rows = logits.shape[0]
    lane = lax.broadcasted_iota(jnp.int32, (rows, LANES), 1)
    lane_f = lane.astype(F32)
    neg_inf = jnp.float32(-jnp.inf)
    work = jnp.where(lane < N_EXPERTS, logits, neg_inf)
    vals, idxs = [], []
    for _ in range(TOP_K):
        m = jnp.max(work, axis=-1, keepdims=True)
        idx = jnp.min(jnp.where(work == m, lane_f, jnp.float32(LANES)), axis=-1, keepdims=True)
        vals.append(m)
        idxs.append(idx)
        work = jnp.where(lane_f == idx, neg_inf, work)
    es = [jnp.exp(v - vals[0]) for v in vals]
    denom = es[0] + es[1] + es[2] + es[3]
    member = jnp.zeros((rows, LANES), F32)
    for k in range(TOP_K):
        member = member + jnp.where(lane_f == idxs[k], 1.0, 0.0)
    r_i = lax.broadcasted_iota(jnp.int32, (rows, rows), 0)
    c_i = lax.broadcasted_iota(jnp.int32, (rows, rows), 1)
    earlier = jnp.where(c_i < r_i, 1.0, 0.0).astype(BF16)
    rank = jnp.dot(earlier, member.astype(BF16), preferred_element_type=F32) + cnt_scr[...]
    cnt = cnt_scr[...] + jnp.sum(member, axis=0, keepdims=True)
    cnt_scr[...] = cnt
    cnt_ref[...] = jnp.broadcast_to(cnt, cnt_ref.shape).astype(jnp.int32)
    er = jnp.zeros((rows, LANES), F32)
    topp = jnp.zeros((rows, LANES), F32)
    for k in range(TOP_K):
        rank_k = jnp.sum(jnp.where(lane_f == idxs[k], rank, 0.0), axis=-1, keepdims=True)
        er = jnp.where(lane == k, idxs[k] + N_EXPERTS * rank_k, er)
        topp = jnp.where(lane == k, es[k] / denom, topp)
    er_ref[...] = er.astype(jnp.int32)
    topp_ref[...] = topp


def _mix_out(x_ctx, x_lat, y_ctx, y_lat, u_ctx, u_lat, proj, modr, w_ssd, w_conf, b_conf, b_gate, w_o, norm_ffn,
             w_router3, b_router):
    t_ctx = x_ctx.shape[0]
    t = t_ctx + x_lat.shape[0]
    tm = 512
    upt = tm // UNIT
    ctx_tiles = t_ctx // tm
    full = lambda shape: pl.BlockSpec(shape, lambda m: (0,) * len(shape), pipeline_mode=pl.Buffered(1))
    mod = lambda which: pl.BlockSpec((None, None, 1, D_MODEL), lambda m: (m * upt, which, 0, 0))
    return pl.pallas_call(
        functools.partial(_mix_kernel, ctx_tiles=ctx_tiles),
        grid=(t // tm,),
        in_specs=_two_source_specs(tm, D_MODEL, ctx_tiles, 1) + _two_source_specs(tm, D_INNER, ctx_tiles, 1)
                 + _two_source_specs(tm, D_CONF, ctx_tiles, 1) + [
                  pl.BlockSpec((tm, 2 * D_MODEL), lambda m: (m, COL_GATE // (2 * D_MODEL))),
                  mod(2), mod(4), mod(3),
                  full((D_INNER, D_MODEL)), full((D_CONF, D_MODEL)), full((1, D_MODEL)),
                  full((1, 2 * D_MODEL)), full((D_MODEL, D_MODEL)), full((1, D_MODEL)),
                  full((3 * D_MODEL, LANES)), full((1, LANES))],
        out_specs=[pl.BlockSpec((tm, D_MODEL), lambda m: (m, 0)),
                   pl.BlockSpec((tm * XW, LANES), lambda m: (m, 0)),
                   pl.BlockSpec((tm, LANES), lambda m: (m, 0)),
                   pl.BlockSpec((tm, LANES), lambda m: (m, 0)),
                   pl.BlockSpec((8, LANES), lambda m: (0, 0))],
        out_shape=[jax.ShapeDtypeStruct((t, D_MODEL), F32),
                   jax.ShapeDtypeStruct((t * XW, LANES), U32),
                   jax.ShapeDtypeStruct((t, LANES), jnp.int32),
                   jax.ShapeDtypeStruct((t, LANES), F32),
                   jax.ShapeDtypeStruct((8, LANES), jnp.int32)],
        scratch_shapes=[pltpu.VMEM((1, LANES), F32)],
        compiler_params=_cparams(1, 56 * 1024 * 1024),
        name="mix_out_router",
    )(x_ctx, x_lat, y_ctx, y_lat, u_ctx, u_lat, proj, modr, modr, modr, w_ssd, w_conf, b_conf.reshape(1, D_MODEL),
      b_gate.reshape(1, 2 * D_MODEL), w_o, norm_ffn.reshape(1, D_MODEL), w_router3, b_router)


def _moe_kernel(off_ref, pair_ref, pb_ref, h2i_ref, wgu_hbm, wd_hbm, bgu_ref, bd_ref, out_ref,
                acc, x_scr, y_scr, stage_gu, stage_d, wgu_b, wd_b, wsem, osem):
    e = pl.program_id(0)
    tr = MOE_TILE

    def weight_copies(ex):
        return (pltpu.make_async_copy(wgu_hbm.at[ex], stage_gu, wsem.at[0]),
                pltpu.make_async_copy(wd_hbm.at[ex], stage_d, wsem.at[1]))

    @pl.when(e == 0)
    def _():
        acc[...] = jnp.zeros(acc.shape, F32)
        y_scr[...] = jnp.zeros(y_scr.shape, F32)
        for cp in weight_copies(0):
            cp.start()

    for cp in weight_copies(e):
        cp.wait()

    def to_bf16(i, c):
        rows = pl.ds(pl.multiple_of(i * MOE_CAST_ROWS, MOE_CAST_ROWS), MOE_CAST_ROWS)
        wgu_b[rows, :] = stage_gu[rows, :].astype(BF16)
        wd_b[rows, :] = stage_d[rows, :].astype(BF16)
        return c

    lax.fori_loop(0, D_MODEL // MOE_CAST_ROWS, to_bf16, 0, unroll=2)

    @pl.when(e + 1 < N_EXPERTS)
    def _():
        for cp in weight_copies(e + 1):
            cp.start()

    start = off_ref[e]
    end = off_ref[e + 1]
    ntiles = (end - start + tr - 1) // tr
    bgu = bgu_ref[...]
    bd = bd_ref[...]

    def gather_rows(base, slot, r0, n):
        for r in range(r0, r0 + n):
            src = jnp.bitwise_and(pair_ref[base + r], -TOP_K)
            x_scr[slot, pl.ds(r * XW, XW), :] = h2i_ref[pl.ds(pl.multiple_of(src, XW), XW), :]

    def scatter_rows(base, slot, r0, live):
        dsts, ps = [], []
        for rr in range(8):
            r = base + r0 + rr
            pair = pair_ref[jnp.maximum(r, 0)]
            dsts.append(pl.multiple_of(jnp.bitwise_and(pair, -TOP_K) * (AW // XW), AW))
            pw = lax.bitcast_convert_type(pb_ref[pair], F32)
            ps.append(jnp.where(jnp.logical_and(live, r < end), pw, 0.0))
        olds = [acc[pl.ds(dsts[rr], AW), :] for rr in range(8)]
        news = [olds[rr] + ps[rr] * y_scr[slot, pl.ds((r0 + rr) * AW, AW), :] for rr in range(8)]
        for rr in range(8):
            acc[pl.ds(dsts[rr], AW), :] = news[rr]

    @pl.when(ntiles > 0)
    def _():
        def first_gather(r8, c2):
            for rr in range(8):
                r = r8 * 8 + rr
                src = jnp.bitwise_and(pair_ref[start + r], -TOP_K)
                x_scr[0, pl.ds(pl.multiple_of(r * XW, XW), XW), :] = h2i_ref[pl.ds(pl.multiple_of(src, XW), XW), :]
            return c2

        lax.fori_loop(0, tr // 8, first_gather, 0)

    def tile_body(ti, carry):
        slot = jnp.bitwise_and(ti, 1)
        other = 1 - slot
        base = start + ti * tr
        parts = []
        for qd in range(XW):
            w = x_scr[slot, pl.ds(qd, tr, stride=XW), :]
            parts.append(lax.bitcast_convert_type(jnp.bitwise_and(w, jnp.uint32(0xFFFF0000)), F32))
            parts.append(lax.bitcast_convert_type(jnp.left_shift(w, jnp.uint32(16)), F32))
        x = jnp.concatenate(parts, axis=1).astype(BF16)
        for r0 in range(0, tr, 8):
            scatter_rows(base - tr, other, r0, ti > 0)
        gather_rows(base + tr, other, 0, tr)
        gu = jnp.dot(x, wgu_b[...], preferred_element_type=F32) + bgu
        g = jnp.minimum(gu[:, :D_EXPERT], SWIGLU_LIMIT)
        u = jnp.clip(gu[:, D_EXPERT:], -SWIGLU_LIMIT, SWIGLU_LIMIT)
        act = (u + 1.0) * g * _sigmoid(SWIGLU_ALPHA * g)
        y = jnp.dot(act.astype(BF16), wd_b[...], preferred_element_type=F32) + bd
        for qd in range(AW):
            y_scr[slot, pl.ds(qd, tr, stride=AW), :] = y[:, qd * LANES:(qd + 1) * LANES]
        return carry

    lax.fori_loop(0, ntiles, tile_body, 0)

    @pl.when(ntiles > 0)
    def _():
        last = ntiles - 1
        lslot = jnp.bitwise_and(last, 1)
        lbase = start + last * tr

        def last_scatter(r8, c2):
            dsts, ps = [], []
            for rr in range(8):
                r = lbase + r8 * 8 + rr
                pair = pair_ref[r]
                dsts.append(pl.multiple_of(jnp.bitwise_and(pair, -TOP_K) * (AW // XW), AW))
                pw = lax.bitcast_convert_type(pb_ref[pair], F32)
                ps.append(jnp.where(r < end, pw, 0.0))
            olds = [acc[pl.ds(dsts[rr], AW), :] for rr in range(8)]
            news = [olds[rr] + ps[rr] * y_scr[lslot, pl.ds(pl.multiple_of((r8 * 8 + rr) * AW, AW), AW), :]
                    for rr in range(8)]
            for rr in range(8):
                acc[pl.ds(dsts[rr], AW), :] = news[rr]
            return c2

        lax.fori_loop(0, tr // 8, last_scatter, 0)

    @pl.when(e == N_EXPERTS - 1)
    def _():
        cp = pltpu.make_async_copy(acc, out_ref, osem)
        cp.start()
        cp.wait()


def _moe_grouped(h2i, off, row_pair, pbits, w_gu, b_gu, w_down, b_down):
    t = h2i.shape[0] // XW
    assert D_EXPERT == D_MODEL
    grid_spec = pltpu.PrefetchScalarGridSpec(
        num_scalar_prefetch=3,
        grid=(N_EXPERTS,),
        in_specs=[pl.BlockSpec((t * XW, LANES), lambda e, *_: (0, 0), pipeline_mode=pl.Buffered(1)),
                  pl.BlockSpec(memory_space=pl.ANY),
                  pl.BlockSpec(memory_space=pl.ANY),
                  pl.BlockSpec((None, 1, 2 * D_EXPERT), lambda e, *_: (e, 0, 0)),
                  pl.BlockSpec((None, 1, D_MODEL), lambda e, *_: (e, 0, 0))],
        out_specs=pl.BlockSpec(memory_space=pl.ANY),
        scratch_shapes=[pltpu.VMEM((t * AW, LANES), F32),
                        pltpu.VMEM((2, MOE_TILE * XW, LANES), U32),
                        pltpu.VMEM((2, MOE_TILE * AW, LANES), F32),
                        pltpu.VMEM((D_MODEL, 2 * D_EXPERT), F32),
                        pltpu.VMEM((D_EXPERT, D_MODEL), F32),
                        pltpu.VMEM((D_MODEL, 2 * D_EXPERT), BF16),
                        pltpu.VMEM((D_EXPERT, D_MODEL), BF16),
                        pltpu.SemaphoreType.DMA((2,)),
                        pltpu.SemaphoreType.DMA(())])
    return pl.pallas_call(
        _moe_kernel,
        grid_spec=grid_spec,
        out_shape=jax.ShapeDtypeStruct((t * AW, LANES), F32),
        compiler_params=_cparams(1, VMEM_LIMIT),
        name="moe_grouped",
    )(off, row_pair, pbits, h2i, w_gu, w_down, b_gu.reshape(N_EXPERTS, 1, 2 * D_EXPERT),
      b_down.reshape(N_EXPERTS, 1, D_MODEL))


def _residual_kernel(x1_ref, moe_ref, g2_ref, nw_ref, oc_ref, ol_ref, *, normalize, ctx_tiles):
    tm = x1_ref.shape[0]
    moe = jnp.concatenate([moe_ref[pl.ds(qd, tm, stride=AW), :] for qd in range(AW)], axis=1)
    x2 = x1_ref[...] + g2_ref[...] * moe
    if normalize:
        ms = jnp.mean(x2 * x2, axis=-1, keepdims=True)
        x2 = (x2 * lax.rsqrt(ms + EPS)) * nw_ref[...]

    @pl.when(pl.program_id(0) < ctx_tiles)
    def _():
        oc_ref[...] = x2

    @pl.when(pl.program_id(0) >= ctx_tiles)
    def _():
        ol_ref[...] = x2


def _residual(x1, moe_i, modr, norm_w, *, normalize, t_ctx):
    t = x1.shape[0]
    tm = 512
    upt = tm // UNIT
    ctx_tiles = t_ctx // tm
    return pl.pallas_call(
        functools.partial(_residual_kernel, normalize=normalize, ctx_tiles=ctx_tiles),
        grid=(t // tm,),
        in_specs=[pl.BlockSpec((tm, D_MODEL), lambda m: (m, 0)),
                  pl.BlockSpec((tm * AW, LANES), lambda m: (m, 0)),
                  pl.BlockSpec((None, None, 1, D_MODEL), lambda m: (m * upt, 5, 0, 0)),
                  pl.BlockSpec((1, D_MODEL), lambda m: (0, 0))],
        out_specs=_two_source_specs(tm, D_MODEL, ctx_tiles, 1),
        out_shape=[jax.ShapeDtypeStruct((t_ctx, D_MODEL), F32),
                   jax.ShapeDtypeStruct((t - t_ctx, D_MODEL), F32)],
        compiler_params=_cparams(1),
        name="ffn_residual_norm",
    )(x1, moe_i, modr, norm_w.reshape(1, D_MODEL))


def _route_kernel(er_ref, cnt_ref, off_ref, pair_ref, offs, *, n_pairs):
    def prefix(e, run):
        offs[e] = run
        off_ref[e] = run
        return run + cnt_ref[e]

    total = lax.fori_loop(0, N_EXPERTS, prefix, jnp.int32(0))
    for e in range(N_EXPERTS, off_ref.shape[0]):
        off_ref[e] = total

    def pad(i, c):
        pair_ref[n_pairs + i] = 0
        return c

    lax.fori_loop(0, ROUTE_PAD, pad, 0)

    def place(i8, c):
        for ii in range(8):
            i = i8 * 8 + ii
            er = er_ref[i]
            pos = offs[jnp.bitwise_and(er, N_EXPERTS - 1)] + lax.shift_right_logical(er, RANK_SHIFT)
            pair_ref[pos] = i
        return c

    lax.fori_loop(0, n_pairs // 8, place, 0)


def _route(er_flat, cnt):
    n = er_flat.shape[0]
    smem = pl.BlockSpec(memory_space=pltpu.SMEM)
    return pl.pallas_call(
        functools.partial(_route_kernel, n_pairs=n),
        in_specs=[smem, smem],
        out_specs=[smem, smem],
        out_shape=[jax.ShapeDtypeStruct((N_EXPERTS + 8,), jnp.int32),
                   jax.ShapeDtypeStruct((n + ROUTE_PAD,), jnp.int32)],
        scratch_shapes=[pltpu.SMEM((N_EXPERTS,), jnp.int32)],
        name="moe_route",
    )(er_flat, cnt)


def kernel(x_prompt, x_sample, state_ssm, c, c_ctx, w_ada, b_ada, norm_mix, norm_ffn, w_in, ssm_conv_w, ssm_conv_b, dt_bias, a_log, d_skip, ssm_norm_w, w_ssd_out, conf_dw_w, conf_dw_b, conf_ln_w, conf_ln_b, w_conf_out, b_conf_out, b_gate, w_o, w_router, b_router, w_gu, b_gu, w_down, b_down, norm_final):
    n_ctx, seq_ctx, _ = x_prompt.shape
    n_lat, seq_lat, _ = x_sample.shape
    depth = w_in.shape[0]
    t_ctx, t_lat = n_ctx * seq_ctx, n_lat * seq_lat
    t_all = t_ctx + t_lat
    assert seq_ctx == UNIT and seq_lat % 1024 == 0 and t_ctx % 1024 == 0
    assert n_lat + 1 <= 8 and seq_lat % GRID_W == 0 and UNIT % GRID_W == 0

    x_ctx, x_lat = x_prompt.reshape(t_ctx, D_MODEL), x_sample.reshape(t_lat, D_MODEL)
    cvec =jnp.concatenate([c_ctx[None, :], c, jnp.zeros((8 - 1 - n_lat, D_MODEL), F32)], axis=0)
    unit_row = jnp.concatenate([jnp.zeros((t_ctx // UNIT,), jnp.int32),
                                1 + jnp.arange(t_lat // UNIT, dtype=jnp.int32) // (seq_lat // UNIT)])
    assert XW == TOP_K and N_EXPERTS == 1 << RANK_SHIFT

    new_states = []
    for l in range(depth):
        mod = _ada_mod(cvec, w_ada[l], b_ada[l])
        modr = mod[unit_row].reshape(t_all // UNIT, 6, 1, D_MODEL)

        wl = w_in[l]
        dt0 = D_INNER + D_XBC
        w_a = wl[:, :dt0].astype(BF16)
        w_b = wl[:, dt0 + 2 * N_HEADS:].astype(BF16)
        zpad = jnp.zeros((D_MODEL, LANES - N_HEADS), F32)
        w_dt = jnp.concatenate([wl[:, dt0:dt0 + N_HEADS], zpad,
                                wl[:, dt0 + N_HEADS:dt0 + 2 * N_HEADS], zpad], axis=1).astype(BF16)
        proj, dt_raw = _in_proj(x_ctx, x_lat, norm_mix[l], modr, w_a, w_b, w_dt)

        pad_h = lambda v: jnp.pad(v.reshape(2, 1, N_HEADS), ((0, 0), (0, 0), (0, LANES - N_HEADS)))
        dtb, alog = pad_h(dt_bias[l]), pad_h(a_log[l])
        dskip = jnp.repeat(d_skip[l], HEAD_DIM).reshape(1, D_INNER)
        nw = ssm_norm_w[l].reshape(1, D_INNER)
        h0t = state_ssm[:, l].astype(F32).reshape(n_lat, 2, D_INNER, D_STATE)

        xbc_ctx = _ssm_conv(proj, ssm_conv_w[l], ssm_conv_b[l], seq=seq_ctx, nseq=n_ctx, row0=0)
        xbc_lat = _ssm_conv(proj, ssm_conv_w[l], ssm_conv_b[l], seq=seq_lat, nseq=n_lat, row0=t_ctx)
        y_ctx, st_ctx = _ssd(xbc_ctx, proj, dt_raw, dtb, alog, dskip, nw, None,
                             seq=seq_ctx, nseq=n_ctx, row0=0, write_state=True)
        (y_lat,) = _ssd(xbc_lat, proj, dt_raw, dtb, alog, dskip, nw, h0t,
                        seq=seq_lat, nseq=n_lat, row0=t_ctx, write_state=False)
        u_ctx = _conformer(proj, conf_dw_w[l], conf_dw_b[l], conf_ln_w[l], conf_ln_b[l],
                           seg=seq_ctx, ntok=t_ctx, row0=0)
        u_lat = _conformer(proj, conf_dw_w[l], conf_dw_b[l], conf_ln_w[l], conf_ln_b[l],
                           seg=GRID_W, ntok=t_lat, row0=t_ctx)
        wr = jnp.pad(w_router[l], ((0, 0), (0, LANES - N_EXPERTS)))
        wr_hi = wr.astype(BF16)
        wr_lo = (wr - wr_hi.astype(F32)).astype(BF16)
        br = jnp.pad(b_router[l], (0, LANES - N_EXPERTS)).reshape(1, LANES)
        x1, h2i, er, topp, cnt = _mix_out(
            x_ctx, x_lat, y_ctx, y_lat, u_ctx, u_lat, proj, modr, w_ssd_out[l].astype(BF16),
            w_conf_out[l].astype(BF16), b_conf_out[l], b_gate[l], w_o[l].astype(BF16), norm_ffn[l],
            jnp.concatenate([wr_hi, wr_lo, wr_hi], axis=0), br)

        off, row_pair = _route(er[:, :TOP_K].reshape(-1), cnt[0, :N_EXPERTS])
        pbits = lax.bitcast_convert_type(topp[:, :TOP_K], jnp.int32).reshape(-1)
        moe_i = _moe_grouped(h2i, off, row_pair, pbits, w_gu[l], b_gu[l], w_down[l], b_down[l])
        last = l + 1 == depth
        x_ctx, x_lat = _residual(x1, moe_i, modr, norm_final if last else norm_ffn[l], normalize=last, t_ctx=t_ctx)
        new_states.append(st_ctx.reshape(n_ctx, 1, 2, N_HEADS, HEAD_DIM, D_STATE))

    y_prompt = x_ctx.reshape(n_ctx, seq_ctx, D_MODEL)
    y_sample = x_lat.reshape(n_lat, seq_lat, D_MODEL)
    new_state_ssm = new_states[0] if depth == 1 else jnp.concatenate(new_states, axis=1)
    return (y_prompt, y_sample, new_state_ssm)
```

```python
import functools

import jax
import jax.numpy as jnp
from jax import lax
from jax.experimental import pallas as pl
from jax.experimental.pallas import tpu as pltpu

F32 = jnp.float32
BF16 = jnp.bfloat16
U32 = jnp.uint32

D_MODEL = 1024
GRID_W = 64
D_INNER = 2 * D_MODEL
HEAD_DIM = 64
N_HEADS = D_INNER // HEAD_DIM
N_GROUPS = 8
HEADS_PER_GROUP = N_HEADS // N_GROUPS
D_STATE = 128
SSM_CONV = 5
CHUNK = 128
D_XBC = D_INNER + 2 * N_GROUPS * D_STATE
D_CONF = D_MODEL
CONF_KERNEL = 31
N_EXPERTS = 32
TOP_K = 4
D_EXPERT = D_MODEL
SWIGLU_LIMIT = 7.0
SWIGLU_ALPHA = 1.702
EPS = 1e-6

LANES = 128
UNIT = 256
GROUP_W = D_INNER // N_GROUPS
SSD_GROUP_UNROLL = 4
MIX_CHAINS = 2
MOE_TILE = 128
ROUTE_PAD = 2 * MOE_TILE
MOE_CAST_ROWS = 16
XW = D_MODEL // (2 * LANES)
AW = D_MODEL // LANES
RANK_SHIFT = 5
VMEM_LIMIT = 62 * 1024 * 1024

COL_Z = 0
COL_XBC = D_INNER
COL_GLU = D_XBC + D_INNER
COL_GATE = D_XBC + D_INNER + 2 * D_CONF
N_MAIN = D_XBC + D_INNER + 2 * D_CONF + 2 * D_MODEL


def _sigmoid(x):
    return 1.0 / (1.0 + jnp.exp(-x))


def _silu(x):
    return x * _sigmoid(x)


def _cparams(n_axes, vmem=None):
    return pltpu.CompilerParams(
        dimension_semantics=("arbitrary",) * n_axes,
        vmem_limit_bytes=vmem)


def _ada_kernel(c_ref, w_ref, b_ref, o_ref):
    s = _silu(c_ref[...])
    o_ref[...] = jnp.dot(s.astype(BF16), w_ref[...].astype(BF16),
                         preferred_element_type=F32) + b_ref[...]


def _ada_mod(cvec, w_ada, b_ada):
    n = w_ada.shape[1]
    tn = 1536
    return pl.pallas_call(
        _ada_kernel,
        grid=(n // tn,),
        in_specs=[pl.BlockSpec((8, D_MODEL), lambda j: (0, 0)),
                  pl.BlockSpec((D_MODEL, tn), lambda j: (0, j)),
                  pl.BlockSpec((1, tn), lambda j: (0, j))],
        out_specs=pl.BlockSpec((8, tn), lambda j: (0, j)),
        out_shape=jax.ShapeDtypeStruct((8, n), F32),
        compiler_params=_cparams(1, 40 * 1024 * 1024),
        name="ada_mod",
    )(cvec, w_ada, b_ada.reshape(1, n))


def _wprep_kernel(w_ref, wn_ref, tail_ref, oa_ref, ob_ref, odt_ref, *, a_tiles):
    n = pl.program_id(0)
    tn = w_ref.shape[1]
    shift = 2 * N_HEADS

    @pl.when(n < a_tiles)
    def _():
        oa_ref[...] = w_ref[...].astype(BF16)

    @pl.when(n >= a_tiles)
    def _():
        lane = lax.broadcasted_iota(jnp.int32, (1, LANES), 1)
        nxt = jnp.where(n == pl.num_programs(0) - 1, tail_ref[...], wn_ref[...])
        for c in range(tn // LANES):
            lo = w_ref[:, c * LANES:(c + 1) * LANES]
            hi = w_ref[:, (c + 1) * LANES:(c + 2) * LANES] if c + 1 < tn // LANES else nxt
            ob_ref[:, c * LANES:(c + 1) * LANES] = jnp.where(
                lane < LANES - shift, pltpu.roll(lo, LANES - shift, 1), pltpu.roll(hi, LANES - shift, 1)).astype(BF16)

    @pl.when(n == a_tiles)
    def _():
        lane = lax.broadcasted_iota(jnp.int32, (1, LANES), 1)
        first = w_ref[:, 0:LANES]
        odt_ref[:, 0:LANES] = jnp.where(lane < N_HEADS, first, 0.0).astype(BF16)
        odt_ref[:, LANES:2 * LANES] = jnp.where(lane < N_HEADS, pltpu.roll(first, LANES - N_HEADS, 1), 0.0).astype(BF16)


def _in_proj_weights(w_in):
    tn = 2048
    n_a = D_INNER + D_XBC
    n_b = 2 * D_CONF + 2 * D_MODEL
    a_tiles = n_a // tn
    assert n_a % tn == 0 and n_b % tn == 0 and 2 * N_HEADS < LANES
    n_tiles = a_tiles + n_b // tn
    tail = jnp.pad(w_in[:, n_tiles * tn:], ((0, 0), (0, LANES - 2 * N_HEADS)))
    return pl.pallas_call(
        functools.partial(_wprep_kernel, a_tiles=a_tiles),
        grid=(n_tiles,),
        in_specs=[pl.BlockSpec((D_MODEL, tn), lambda n: (0, n)),
                  pl.BlockSpec((D_MODEL, LANES), lambda n: (0, jnp.minimum(n + 1, n_tiles - 1) * (tn // LANES))),
                  pl.BlockSpec((D_MODEL, LANES), lambda n: (0, 0))],
        out_specs=[pl.BlockSpec((D_MODEL, tn), lambda n: (0, jnp.minimum(n, a_tiles - 1))),
                   pl.BlockSpec((D_MODEL, tn), lambda n: (0, jnp.maximum(n - a_tiles, 0))),
                   pl.BlockSpec((D_MODEL, 2 * LANES), lambda n: (0, 0))],
        out_shape=[jax.ShapeDtypeStruct((D_MODEL, n_a), BF16),
                   jax.ShapeDtypeStruct((D_MODEL, n_b), BF16),
                   jax.ShapeDtypeStruct((D_MODEL, 2 * LANES), BF16)],
        compiler_params=_cparams(1, 48 * 1024 * 1024),
        name="in_proj_weights",
    )(w_in, w_in, tail)


def _inproj_kernel(xc_ref, xl_ref, nw_ref, sc_ref, sh_ref, wa_ref, wb_ref, wdt_ref, o_ref, dt_ref, h_scr, *,
                   ctx_tiles, a_tiles):
    @pl.when(pl.program_id(1) == 0)
    def _():
        xf = jnp.where(pl.program_id(0) < ctx_tiles, xc_ref[...], xl_ref[...])
        ms = jnp.mean(xf * xf, axis=-1, keepdims=True)
        hn = (xf * lax.rsqrt(ms + EPS)) * nw_ref[...]
        hn = hn * (1.0 + sc_ref[...]) + sh_ref[...]
        hb = hn.astype(BF16)
        h_scr[...] = hb
        dt_ref[...] = jnp.dot(hb, wdt_ref[...], preferred_element_type=F32)

    @pl.when(pl.program_id(1) < a_tiles)
    def _():
        o_ref[...] = jnp.dot(h_scr[...], wa_ref[...], preferred_element_type=F32).astype(BF16)

    @pl.when(pl.program_id(1) >= a_tiles)
    def _():
        o_ref[...] = jnp.dot(h_scr[...], wb_ref[...], preferred_element_type=F32).astype(BF16)


def _two_source_specs(tm, width, ctx_tiles, nargs):
    if nargs == 1:
        return [pl.BlockSpec((tm, width), lambda m: (jnp.minimum(m, ctx_tiles - 1), 0)),
                pl.BlockSpec((tm, width), lambda m: (jnp.maximum(m - ctx_tiles, 0), 0))]
    return [pl.BlockSpec((tm, width), lambda m, n: (jnp.minimum(m, ctx_tiles - 1), 0)),
            pl.BlockSpec((tm, width), lambda m, n: (jnp.maximum(m - ctx_tiles, 0), 0))]


def _in_proj(x_ctx, x_lat, norm_w, modr, w_a, w_b, w_dt):
    t_ctx = x_ctx.shape[0]
    t = t_ctx + x_lat.shape[0]
    tm, tn = 1024, 2048
    upt = tm // UNIT
    a_tiles = w_a.shape[1] // tn
    return pl.pallas_call(
        functools.partial(_inproj_kernel, ctx_tiles=t_ctx // tm, a_tiles=a_tiles),
        grid=(t // tm, N_MAIN // tn),
        in_specs=_two_source_specs(tm, D_MODEL, t_ctx // tm, 2) + [
                  pl.BlockSpec((1, D_MODEL), lambda m, n: (0, 0)),
                  pl.BlockSpec((None, None, 1, D_MODEL), lambda m, n: (m * upt, 1, 0, 0)),
                  pl.BlockSpec((None, None, 1, D_MODEL), lambda m, n: (m * upt, 0, 0, 0)),
                  pl.BlockSpec((D_MODEL, tn), lambda m, n: (0, jnp.minimum(n, a_tiles - 1))),
                  pl.BlockSpec((D_MODEL, tn), lambda m, n: (0, jnp.maximum(n - a_tiles, 0))),
                  pl.BlockSpec((D_MODEL, 2 * LANES), lambda m, n: (0, 0))],
        out_specs=[pl.BlockSpec((tm, tn), lambda m, n: (m, n)),
                   pl.BlockSpec((tm, 2 * LANES), lambda m, n: (m, 0))],
        out_shape=[jax.ShapeDtypeStruct((t, N_MAIN), BF16),
                   jax.ShapeDtypeStruct((t, 2 * LANES), F32)],
        scratch_shapes=[pltpu.VMEM((tm, D_MODEL), BF16)],
        compiler_params=_cparams(2, 48 * 1024 * 1024),
        name="in_proj",
    )(x_ctx, x_lat, norm_w.reshape(1, D_MODEL), modr, modr, w_a, w_b, w_dt)


def _ssm_conv_kernel(x_ref, w_ref, b_ref, o_ref, pad_scr, *, seq):
    cb = x_ref.shape[1]
    pad_scr[0:8, :] = jnp.zeros((8, cb), F32)
    pad_scr[8 + seq:16 + seq, :] = jnp.zeros((8, cb), F32)
    pad_scr[8:8 + seq, :] = x_ref[...].astype(F32)
    half = SSM_CONV // 2
    rows, cw = 256, 512
    for c0 in range(0, cb, cw):
        cs = slice(c0, c0 + cw)
        for r0 in range(0, seq, rows):
            acc = jnp.broadcast_to(b_ref[:, cs], (rows, cw))
            for k in range(SSM_CONV):
                acc = acc + w_ref[k:k + 1, cs] * pad_scr[8 + r0 + k - half:8 + r0 + k - half + rows, cs]
            o_ref[r0:r0 + rows, cs] = _silu(acc).astype(BF16)


def _ssm_conv(proj, conv_w, conv_b, *, seq, nseq, row0):
    cb = 2048
    blk0 = row0 // seq
    return pl.pallas_call(
        functools.partial(_ssm_conv_kernel, seq=seq),
        grid=(nseq, D_XBC // cb),
        in_specs=[pl.BlockSpec((seq, cb), lambda s, j: (blk0 + s, COL_XBC // cb + j)),
                  pl.BlockSpec((SSM_CONV, cb), lambda s, j: (0, j)),
                  pl.BlockSpec((1, cb), lambda s, j: (0, j))],
        out_specs=pl.BlockSpec((seq, cb), lambda s, j: (s, j)),
        out_shape=jax.ShapeDtypeStruct((nseq * seq, D_XBC), BF16),
        scratch_shapes=[pltpu.VMEM((seq + 16, cb), F32)],
        compiler_params=_cparams(2, 40 * 1024 * 1024),
        name="ssm_conv",
    )(proj, conv_w, conv_b.reshape(1, D_XBC))


def _split_bf16(v):
    hi = v.astype(BF16)
    lo = (v - hi.astype(F32)).astype(BF16)
    return jnp.concatenate([hi, lo], axis=1)


def _head_select_matrices():
    j = jnp.arange(2 * LANES, dtype=jnp.int32)[:, None] % LANES
    full = (j == (jnp.arange(N_HEADS * LANES, dtype=jnp.int32)[None, :] // LANES)).astype(BF16)
    exp = (j == (jnp.arange(D_INNER, dtype=jnp.int32)[None, :] // HEAD_DIM)).astype(BF16)
    return full, exp


def _ssd_kernel(*refs, nc, has_h0, write_state):
    (xs_ref, b_ref, c_ref, z_ref, dtr_ref, dtb_ref, alog_ref, dskip_ref, nw_ref, self_ref, sele_ref), rest = \
        refs[:11], refs[11:]
    if has_h0:
        h0_ref, rest = rest[0], rest[1:]
    y_ref, rest = rest[0], rest[1:]
    if write_state:
        st_ref, rest = rest[0], rest[1:]
    h_scr, ybuf, ychunk, colb_scr, wexp_scr, eexp_scr, texp_scr, rowq_scr = rest

    q = CHUNK
    phase = pl.program_id(1)
    c = pl.program_id(2)
    is_fwd = phase == 1
    c_eff = jnp.where(is_fwd, c, nc - 1 - c)

    @pl.when(c == 0)
    def _():
        if has_h0:
            for j in range(D_INNER // LANES):
                h_scr[:, j * LANES:(j + 1) * LANES] = h0_ref[j * LANES:(j + 1) * LANES, :].T
        else:
            h_scr[...] = jnp.zeros(h_scr.shape, F32)

    xdt = dtr_ref[...] + dtb_ref[...]
    dt = jnp.maximum(xdt, 0.0) + jnp.log(1.0 + jnp.exp(-jnp.abs(xdt)))
    a = -jnp.exp(alog_ref[...])
    adt = dt * a
    row = lax.broadcasted_iota(jnp.int32, (q, q), 0)
    col = lax.broadcasted_iota(jnp.int32, (q, q), 1)
    tri = (col - row) * jnp.where(is_fwd, 1, -1) <= 0
    cum2 = jnp.dot(tri.astype(BF16), _split_bf16(adt), preferred_element_type=F32)
    cum = cum2[:, :LANES] + cum2[:, LANES:]
    tot = jnp.where(is_fwd, cum[q - 1:q, :], cum[0:1, :])
    rowq_scr[...] = (cum - jnp.log(dt)).T
    wdec = dt * jnp.exp(tot - cum)
    eo = jnp.exp(cum)
    etot = jnp.broadcast_to(jnp.exp(tot), (8, LANES))
    colb_scr[...] = jnp.dot(_split_bf16(cum), self_ref[...], preferred_element_type=F32)
    expanded = jnp.dot(jnp.concatenate([_split_bf16(wdec), _split_bf16(eo), _split_bf16(etot)], axis=0),
                       sele_ref[...], preferred_element_type=F32)
    wexp_scr[...] = expanded[0:q]
    eexp_scr[...] = expanded[q:2 * q]
    texp_scr[...] = expanded[2 * q:2 * q + 8]
    lane_g = lax.broadcasted_iota(jnp.int32, (1, GROUP_W), 1)
    head_mask = [(lane_g // HEAD_DIM == hh).astype(BF16) for hh in range(HEADS_PER_GROUP)]
    neg_inf = jnp.float32(-jnp.inf)

    def group_body(g, carry):
        gs = pl.ds(pl.multiple_of(g * GROUP_W, GROUP_W), GROUP_W)
        ns = pl.ds(pl.multiple_of(g * D_STATE, D_STATE), D_STATE)
        bg = b_ref[:, ns]
        cg = c_ref[:, ns]
        scores = lax.dot_general(cg, bg, (((1,), (1,)), ((), ())), preferred_element_type=F32)
        xs_g = xs_ref[:, gs]
        ms, xb = [], []
        for hh in range(HEADS_PER_GROUP):
            h = HEADS_PER_GROUP * g + hh
            colb = colb_scr[:, pl.ds(pl.multiple_of(h * LANES, LANES), LANES)]
            seg = jnp.where(tri, colb - rowq_scr[pl.ds(h, 1), :], neg_inf)
            ms.append((scores * jnp.exp(seg)).astype(BF16))
            xb.append(xs_g * head_mask[hh])
        y_diag = jnp.dot(jnp.concatenate(ms, axis=1), jnp.concatenate(xb, axis=0),
                         preferred_element_type=F32)
        xd = (xs_g.astype(F32) * wexp_scr[:, gs]).astype(BF16)
        bt = bg.astype(F32).T.astype(BF16)
        st = jnp.dot(bt, xd, preferred_element_type=F32)
        hg = h_scr[:, gs]
        y_off = jnp.dot(cg, hg.astype(BF16), preferred_element_type=F32) * eexp_scr[:, gs]
        ychunk[:, gs] = y_diag + y_off
        h_scr[:, gs] = hg * texp_scr[0:1, gs] + st
        return carry

    lax.fori_loop(0, N_GROUPS, group_body, 0, unroll=SSD_GROUP_UNROLL)

    rows = pl.ds(pl.multiple_of(c_eff * q, q), q)

    @pl.when(jnp.logical_not(is_fwd))
    def _():
        ybuf[rows, :] = ychunk[...]

    @pl.when(is_fwd)
    def _():
        zf = z_ref[...].astype(F32)
        yt = ychunk[...] + ybuf[rows, :] + xs_ref[...].astype(F32) * dskip_ref[...]
        yz = yt * _silu(zf)
        for g in range(N_GROUPS):
            gs = slice(g * GROUP_W, (g + 1) * GROUP_W)
            blk = yz[:, gs]
            ms = jnp.mean(blk * blk, axis=-1, keepdims=True)
            y_ref[:, gs] = (blk * lax.rsqrt(ms + EPS) * nw_ref[:, gs]).astype(BF16)

    if write_state:
        @pl.when(c == nc - 1)
        def _():
            for j in range(D_INNER // LANES):
                st_ref[j * LANES:(j + 1) * LANES, :] = h_scr[:, j * LANES:(j + 1) * LANES].T


def _ssd(xbc_c, proj, dt_raw, dt_bias, a_log, d_skip, norm_w, h0t, *, seq, nseq, row0, write_state):
    nc = seq // CHUNK
    blk0 = row0 // CHUNK
    has_h0 = h0t is not None

    def tok(s, p, c):
        return s * nc + p * c + (1 - p) * (nc - 1 - c)

    in_specs = [
        pl.BlockSpec((CHUNK, D_INNER), lambda s, p, c: (tok(s, p, c), 0)),
        pl.BlockSpec((CHUNK, N_GROUPS * D_STATE), lambda s, p, c: (tok(s, p, c), 2)),
        pl.BlockSpec((CHUNK, N_GROUPS * D_STATE), lambda s, p, c: (tok(s, p, c), 3)),
        pl.BlockSpec((CHUNK, D_INNER), lambda s, p, c: (blk0 + s * nc + p * c, COL_Z // D_INNER)),
        pl.BlockSpec((CHUNK, LANES), lambda s, p, c: (blk0 + tok(s, p, c), 1 - p)),
        pl.BlockSpec((None, 1, LANES), lambda s, p, c: (1 - p, 0, 0)),
        pl.BlockSpec((None, 1, LANES), lambda s, p, c: (1 - p, 0, 0)),
        pl.BlockSpec((1, D_INNER), lambda s, p, c: (0, 0)),
        pl.BlockSpec((1, D_INNER), lambda s, p, c: (0, 0)),
        pl.BlockSpec((2 * LANES, N_HEADS * LANES), lambda s, p, c: (0, 0)),
        pl.BlockSpec((2 * LANES, D_INNER), lambda s, p, c: (0, 0)),
    ]
    sel_full, sel_exp = _head_select_matrices()
    args = [xbc_c, xbc_c, xbc_c, proj, dt_raw, dt_bias, a_log, d_skip, norm_w, sel_full, sel_exp]
    if has_h0:
        in_specs.append(pl.BlockSpec((None, None, D_INNER, D_STATE), lambda s, p, c: (s, 1 - p, 0, 0)))
        args.append(h0t)
    out_specs = [pl.BlockSpec((CHUNK, D_INNER), lambda s, p, c: (s * nc + p * c, 0))]
    out_shape = [jax.ShapeDtypeStruct((nseq * seq, D_INNER), BF16)]
    if write_state:
        out_specs.append(pl.BlockSpec((None, None, D_INNER, D_STATE), lambda s, p, c: (s, 1 - p, 0, 0)))
        out_shape.append(jax.ShapeDtypeStruct((nseq, 2, D_INNER, D_STATE), F32))
    return pl.pallas_call(
        functools.partial(_ssd_kernel, nc=nc, has_h0=has_h0, write_state=write_state),
        grid=(nseq, 2, nc),
        in_specs=in_specs,
        out_specs=out_specs,
        out_shape=out_shape,
        scratch_shapes=[pltpu.VMEM((D_STATE, D_INNER), F32),
                        pltpu.VMEM((seq, D_INNER), F32),
                        pltpu.VMEM((CHUNK, D_INNER), F32),
                        pltpu.VMEM((CHUNK, N_HEADS * LANES), F32),
                        pltpu.VMEM((CHUNK, D_INNER), F32),
                        pltpu.VMEM((CHUNK, D_INNER), F32),
                        pltpu.VMEM((8, D_INNER), F32),
                        pltpu.VMEM((LANES, CHUNK), F32)],
        compiler_params=_cparams(3, 48 * 1024 * 1024),
        name="ssd_scan",
    )(*args)


def _conf_kernel(glu_ref, w_ref, b_ref, lnw_ref, lnb_ref, o_ref, pad_scr, sh_scr, conv_scr, *, seg):
    rows = glu_ref.shape[0]
    nseg = rows // seg
    half = CONF_KERNEL // 2
    front = 16
    span = seg + 24
    a = glu_ref[:, :D_CONF].astype(F32)
    b = glu_ref[:, D_CONF:].astype(F32)
    u = a * _sigmoid(b)
    for i in range(nseg):
        pad_scr[i, 0:front, :] = jnp.zeros((front, D_CONF), F32)
        pad_scr[i, front + seg:front + seg + 16, :] = jnp.zeros((16, D_CONF), F32)
        pad_scr[i, front:front + seg, :] = u[i * seg:(i + 1) * seg, :]
    for s in range(8):
        for i in range(nseg):
            for cbi in range(D_CONF // LANES):
                cs = slice(cbi * LANES, (cbi + 1) * LANES)
                for r0 in range(0, span, 56):
                    n = min(56, span - r0)
                    sh_scr[s, i, r0:r0 + n, cs] = pad_scr[i, s + r0:s + r0 + n, cs]
    rb = 64
    for i in range(nseg):
        for cbi in range(D_CONF // LANES):
            cs = slice(cbi * LANES, (cbi + 1) * LANES)
            for r0 in range(0, seg, rb):
                acc = jnp.broadcast_to(b_ref[:, cs], (rb, LANES))
                for k in range(CONF_KERNEL):
                    start = front + r0 + k - half
                    al = start - start % 8
                    acc = acc + w_ref[k:k + 1, cs] * sh_scr[start % 8, i, al:al + rb, cs]
                conv_scr[i * seg + r0:i * seg + r0 + rb, cs] = acc
    v = conv_scr[...]
    mu = jnp.mean(v, axis=-1, keepdims=True)
    vc = v - mu
    var = jnp.mean(vc * vc, axis=-1, keepdims=True)
    ln = (vc * lax.rsqrt(var + EPS)) * lnw_ref[...] + lnb_ref[...]
    o_ref[...] = _silu(ln).astype(BF16)


def _conformer(proj, dw_w, dw_b, ln_w, ln_b, *, seg, ntok, row0):
    rows = UNIT
    blk0 = row0 // rows
    return pl.pallas_call(
        functools.partial(_conf_kernel, seg=seg),
        grid=(ntok // rows,),
        in_specs=[pl.BlockSpec((rows, 2 * D_CONF), lambda i: (blk0 + i, COL_GLU // (2 * D_CONF))),
                  pl.BlockSpec((CONF_KERNEL, D_CONF), lambda i: (0, 0)),
                  pl.BlockSpec((1, D_CONF), lambda i: (0, 0)),
                  pl.BlockSpec((1, D_CONF), lambda i: (0, 0)),
                  pl.BlockSpec((1, D_CONF), lambda i: (0, 0))],
        out_specs=pl.BlockSpec((rows, D_CONF), lambda i: (i, 0)),
        out_shape=jax.ShapeDtypeStruct((ntok, D_CONF), BF16),
        scratch_shapes=[pltpu.VMEM((rows // seg, seg + 32, D_CONF), F32),
                        pltpu.VMEM((8, rows // seg, seg + 24, D_CONF), F32),
                        pltpu.VMEM((rows, D_CONF), F32)],
        compiler_params=_cparams(1, 40 * 1024 * 1024),
        name="conformer_conv",
    )(proj, dw_w, dw_b.reshape(1, D_CONF), ln_w.reshape(1, D_CONF), ln_b.reshape(1, D_CONF))


def _pack_halves(x):
    outs = []
    for cb in range(x.shape[1] // (2 * LANES)):
        hi = x[:, cb * 2 * LANES:cb * 2 * LANES + LANES].astype(BF16).astype(F32)
        lo = x[:, cb * 2 * LANES + LANES:(cb + 1) * 2 * LANES].astype(BF16).astype(F32)
        hw = lax.bitcast_convert_type(hi, U32)
        lw = jnp.right_shift(lax.bitcast_convert_type(lo, U32), jnp.uint32(16))
        outs.append(jnp.bitwise_or(hw, lw))
    return jnp.concatenate(outs, axis=1)


def _mix_kernel(xc_ref, xl_ref, yc_ref, yl_ref, uc_ref, ul_ref, gate_ref, g1_ref, sc2_ref, sh2_ref,
                wssd_ref, wconf_ref, bconf_ref, bgate_ref, wo_ref, nffn_ref, wr_ref, br_ref,
                x1_ref, h2i_ref, er_ref, topp_ref, cnt_ref, cnt_scr, *, ctx_tiles):
    @pl.when(pl.program_id(0) == 0)
    def _():
        cnt_scr[...] = jnp.zeros(cnt_scr.shape, F32)

    is_ctx = pl.program_id(0) < ctx_tiles
    rows = x1_ref.shape[0]
    sub = rows // MIX_CHAINS
    lane = lax.broadcasted_iota(jnp.int32, (sub, LANES), 1)
    lane_f = lane.astype(F32)
    neg_inf = jnp.float32(-jnp.inf)

    def chain(c):
        rs = slice(c * sub, (c + 1) * sub)
        y_in = jnp.where(is_ctx, yc_ref[rs, :], yl_ref[rs, :])
        u_in = jnp.where(is_ctx, uc_ref[rs, :], ul_ref[rs, :])
        o_ssd = jnp.dot(y_in, wssd_ref[...], preferred_element_type=F32)
        o_conf = jnp.dot(u_in, wconf_ref[...], preferred_element_type=F32) + bconf_ref[...]
        gates = _sigmoid(gate_ref[rs, :].astype(F32) + bgate_ref[...])
        merged = gates[:, :D_MODEL] * o_ssd + gates[:, D_MODEL:] * o_conf
        out = jnp.dot(merged.astype(BF16), wo_ref[...], preferred_element_type=F32)
        x1 = jnp.where(is_ctx, xc_ref[rs, :], xl_ref[rs, :]) + g1_ref[...] * out
        x1_ref[rs, :] = x1
        ms = jnp.mean(x1 * x1, axis=-1, keepdims=True)
        h2 = (x1 * lax.rsqrt(ms + EPS)) * nffn_ref[...]
        h2 = h2 * (1.0 + sc2_ref[...]) + sh2_ref[...]
        packed = _pack_halves(h2)
        for qd in range(XW):
            h2i_ref[pl.ds(c * sub * XW + qd, sub, stride=XW), :] = packed[:, qd * LANES:(qd + 1) * LANES]
        h_hi = h2.astype(BF16)
        h_lo = (h2 - h_hi.astype(F32)).astype(BF16)
        logits = jnp.dot(jnp.concatenate([h_hi, h_hi, h_lo], axis=1), wr_ref[...],
                         preferred_element_type=F32) + br_ref[...]
        work = jnp.where(lane < N_EXPERTS, logits, neg_inf)
        vals, idxs = [], []
        for _ in range(TOP_K):
            m = jnp.max(work, axis=-1, keepdims=True)
            idx = jnp.min(jnp.where(work == m, lane_f, jnp.float32(LANES)), axis=-1, keepdims=True)
            vals.append(m)
            idxs.append(idx)
            work = jnp.where(lane_f == idx, neg_inf, work)
        es = [jnp.exp(v - vals[0]) for v in vals]
        denom = es[0] + es[1] + es[2] + es[3]
        member = jnp.zeros((sub, LANES), F32)
        topp = jnp.zeros((sub, LANES), F32)
        for k in range(TOP_K):
            member = member + jnp.where(lane_f == idxs[k], 1.0, 0.0)
            topp = jnp.where(lane == k, es[k] / denom, topp)
        topp_ref[rs, :] = topp
        return idxs, member

    results = [chain(c) for c in range(MIX_CHAINS)]
    member = jnp.concatenate([m for _, m in results], axis=0)
    r_i = lax.broadcasted_iota(jnp.int32, (rows, rows), 0)
    c_i = lax.broadcasted_iota(jnp.int32, (rows, rows), 1)
    earlier = jnp.where(c_i < r_i, 1.0, 0.0).astype(BF16)
    rank = jnp.dot(earlier, member.astype(BF16), preferred_element_type=F32) + cnt_scr[...]
    cnt = cnt_scr[...] + jnp.sum(member, axis=0, keepdims=True)
    cnt_scr[...] = cnt
    cnt_ref[...] = jnp.broadcast_to(cnt, cnt_ref.shape).astype(jnp.int32)
    for c, (idxs, _) in enumerate(results):
        rs = slice(c * sub, (c + 1) * sub)
        er = jnp.zeros((sub, LANES), F32)
        for k in range(TOP_K):
            rank_k = jnp.sum(jnp.where(lane_f == idxs[k], rank[rs, :], 0.0), axis=-1, keepdims=True)
            er = jnp.where(lane == k, idxs[k] + N_EXPERTS * rank_k, er)
        er_ref[rs, :] = er.astype(jnp.int32)


def _mix_out(x_ctx, x_lat, y_ctx, y_lat, u_ctx, u_lat, proj, modr, w_ssd, w_conf, b_conf, b_gate, w_o, norm_ffn,
             w_router3, b_router):
    t_ctx = x_ctx.shape[0]
    t = t_ctx + x_lat.shape[0]
    tm = 512
    upt = tm // UNIT
    ctx_tiles = t_ctx // tm
    full = lambda shape: pl.BlockSpec(shape, lambda m: (0,) * len(shape), pipeline_mode=pl.Buffered(1))
    mod = lambda which: pl.BlockSpec((None, None, 1, D_MODEL), lambda m: (m * upt, which, 0, 0))
    return pl.pallas_call(
        functools.partial(_mix_kernel, ctx_tiles=ctx_tiles),
        grid=(t // tm,),
        in_specs=_two_source_specs(tm, D_MODEL, ctx_tiles, 1) + _two_source_specs(tm, D_INNER, ctx_tiles, 1)
                 + _two_source_specs(tm, D_CONF, ctx_tiles, 1) + [
                  pl.BlockSpec((tm, 2 * D_MODEL), lambda m: (m, COL_GATE // (2 * D_MODEL))),
                  mod(2), mod(4), mod(3),
                  full((D_INNER, D_MODEL)), full((D_CONF, D_MODEL)), full((1, D_MODEL)),
                  full((1, 2 * D_MODEL)), full((D_MODEL, D_MODEL)), full((1, D_MODEL)),
                  full((3 * D_MODEL, LANES)), full((1, LANES))],
        out_specs=[pl.BlockSpec((tm, D_MODEL), lambda m: (m, 0)),
                   pl.BlockSpec((tm * XW, LANES), lambda m: (m, 0)),
                   pl.BlockSpec((tm, LANES), lambda m: (m, 0)),
                   pl.BlockSpec((tm, LANES), lambda m: (m, 0)),
                   pl.BlockSpec((8, LANES), lambda m: (0, 0))],
        out_shape=[jax.ShapeDtypeStruct((t, D_MODEL), F32),
                   jax.ShapeDtypeStruct((t * XW, LANES), U32),
                   jax.ShapeDtypeStruct((t, LANES), jnp.int32),
                   jax.ShapeDtypeStruct((t, LANES), F32),
                   jax.ShapeDtypeStruct((8, LANES), jnp.int32)],
        scratch_shapes=[pltpu.VMEM((1, LANES), F32)],
        compiler_params=_cparams(1, 56 * 1024 * 1024),
        name="mix_out_router",
    )(x_ctx, x_lat, y_ctx, y_lat, u_ctx, u_lat, proj, modr, modr, modr, w_ssd, w_conf, b_conf.reshape(1, D_MODEL),
      b_gate.reshape(1, 2 * D_MODEL), w_o, norm_ffn.reshape(1, D_MODEL), w_router3, b_router)


def _moe_kernel(off_ref, pair_ref, pb_ref, h2i_ref, wgu_hbm, wd_hbm, bgu_ref, bd_ref, out_ref,
                acc, x_scr, y_scr, stage_gu, stage_d, wgu_b, wd_b, wsem, osem):
    e = pl.program_id(0)
    tr = MOE_TILE

    def weight_copies(ex):
        return (pltpu.make_async_copy(wgu_hbm.at[ex], stage_gu, wsem.at[0]),
                pltpu.make_async_copy(wd_hbm.at[ex], stage_d, wsem.at[1]))

    @pl.when(e == 0)
    def _():
        acc[...] = jnp.zeros(acc.shape, F32)
        y_scr[...] = jnp.zeros(y_scr.shape, F32)
        for cp in weight_copies(0):
            cp.start()

    for cp in weight_copies(e):
        cp.wait()

    def to_bf16(i, c):
        rows = pl.ds(pl.multiple_of(i * MOE_CAST_ROWS, MOE_CAST_ROWS), MOE_CAST_ROWS)
        wgu_b[rows, :] = stage_gu[rows, :].astype(BF16)
        wd_b[rows, :] = stage_d[rows, :].astype(BF16)
        return c

    lax.fori_loop(0, D_MODEL // MOE_CAST_ROWS, to_bf16, 0, unroll=2)

    @pl.when(e + 1 < N_EXPERTS)
    def _():
        for cp in weight_copies(e + 1):
            cp.start()

    start = off_ref[e]
    end = off_ref[e + 1]
    ntiles = (end - start + tr - 1) // tr
    bgu = bgu_ref[...]
    bd = bd_ref[...]

    def gather_rows(base, slot, r0, n):
        for r in range(r0, r0 + n):
            src = jnp.bitwise_and(pair_ref[base + r], -TOP_K)
            x_scr[slot, pl.ds(r * XW, XW), :] = h2i_ref[pl.ds(pl.multiple_of(src, XW), XW), :]

    def scatter_rows(base, slot, r0, live):
        dsts, ps = [], []
        for rr in range(8):
            r = base + r0 + rr
            pair = pair_ref[jnp.maximum(r, 0)]
            dsts.append(pl.multiple_of(jnp.bitwise_and(pair, -TOP_K) * (AW // XW), AW))
            pw = lax.bitcast_convert_type(pb_ref[pair], F32)
            ps.append(jnp.where(jnp.logical_and(live, r < end), pw, 0.0))
        olds = [acc[pl.ds(dsts[rr], AW), :] for rr in range(8)]
        news = [olds[rr] + ps[rr] * y_scr[slot, pl.ds((r0 + rr) * AW, AW), :] for rr in range(8)]
        for rr in range(8):
            acc[pl.ds(dsts[rr], AW), :] = news[rr]

    @pl.when(ntiles > 0)
    def _():
        def first_gather(r8, c2):
            for rr in range(8):
                r = r8 * 8 + rr
                src = jnp.bitwise_and(pair_ref[start + r], -TOP_K)
                x_scr[0, pl.ds(pl.multiple_of(r * XW, XW), XW), :] = h2i_ref[pl.ds(pl.multiple_of(src, XW), XW), :]
            return c2

        lax.fori_loop(0, tr // 8, first_gather, 0)

    def tile_body(ti, carry):
        slot = jnp.bitwise_and(ti, 1)
        other = 1 - slot
        base = start + ti * tr
        parts = []
        for qd in range(XW):
            w = x_scr[slot, pl.ds(qd, tr, stride=XW), :]
            parts.append(lax.bitcast_convert_type(jnp.bitwise_and(w, jnp.uint32(0xFFFF0000)), F32))
            parts.append(lax.bitcast_convert_type(jnp.left_shift(w, jnp.uint32(16)), F32))
        x = jnp.concatenate(parts, axis=1).astype(BF16)
        for r0 in range(0, tr, 8):
            scatter_rows(base - tr, other, r0, ti > 0)
        gather_rows(base + tr, other, 0, tr)
        gu = jnp.dot(x, wgu_b[...], preferred_element_type=F32) + bgu
        g = jnp.minimum(gu[:, :D_EXPERT], SWIGLU_LIMIT)
        u = jnp.clip(gu[:, D_EXPERT:], -SWIGLU_LIMIT, SWIGLU_LIMIT)
        act = (u + 1.0) * g * _sigmoid(SWIGLU_ALPHA * g)
        y = jnp.dot(act.astype(BF16), wd_b[...], preferred_element_type=F32) + bd
        for qd in range(AW):
            y_scr[slot, pl.ds(qd, tr, stride=AW), :] = y[:, qd * LANES:(qd + 1) * LANES]
        return carry

    lax.fori_loop(0, ntiles, tile_body, 0)

    @pl.when(ntiles > 0)
    def _():
        last = ntiles - 1
        lslot = jnp.bitwise_and(last, 1)
        lbase = start + last * tr

        def last_scatter(r8, c2):
            dsts, ps = [], []
            for rr in range(8):
                r = lbase + r8 * 8 + rr
                pair = pair_ref[r]
                dsts.append(pl.multiple_of(jnp.bitwise_and(pair, -TOP_K) * (AW // XW), AW))
                pw = lax.bitcast_convert_type(pb_ref[pair], F32)
                ps.append(jnp.where(r < end, pw, 0.0))
            olds = [acc[pl.ds(dsts[rr], AW), :] for rr in range(8)]
            news = [olds[rr] + ps[rr] * y_scr[lslot, pl.ds(pl.multiple_of((r8 * 8 + rr) * AW, AW), AW), :]
                    for rr in range(8)]
            for rr in range(8):
                acc[pl.ds(dsts[rr], AW), :] = news[rr]
            return c2

        lax.fori_loop(0, tr // 8, last_scatter, 0)

    @pl.when(e == N_EXPERTS - 1)
    def _():
        cp = pltpu.make_async_copy(acc, out_ref, osem)
        cp.start()
        cp.wait()


def _moe_grouped(h2i, off, row_pair, pbits, w_gu, b_gu, w_down, b_down):
    t = h2i.shape[0] // XW
    assert D_EXPERT == D_MODEL
    grid_spec = pltpu.PrefetchScalarGridSpec(
        num_scalar_prefetch=3,
        grid=(N_EXPERTS,),
        in_specs=[pl.BlockSpec((t * XW, LANES), lambda e, *_: (0, 0), pipeline_mode=pl.Buffered(1)),
                  pl.BlockSpec(memory_space=pl.ANY),
                  pl.BlockSpec(memory_space=pl.ANY),
                  pl.BlockSpec((None, 1, 2 * D_EXPERT), lambda e, *_: (e, 0, 0)),
                  pl.BlockSpec((None, 1, D_MODEL), lambda e, *_: (e, 0, 0))],
        out_specs=pl.BlockSpec(memory_space=pl.ANY),
        scratch_shapes=[pltpu.VMEM((t * AW, LANES), F32),
                        pltpu.VMEM((2, MOE_TILE * XW, LANES), U32),
                        pltpu.VMEM((2, MOE_TILE * AW, LANES), F32),
                        pltpu.VMEM((D_MODEL, 2 * D_EXPERT), F32),
                        pltpu.VMEM((D_EXPERT, D_MODEL), F32),
                        pltpu.VMEM((D_MODEL, 2 * D_EXPERT), BF16),
                        pltpu.VMEM((D_EXPERT, D_MODEL), BF16),
                        pltpu.SemaphoreType.DMA((2,)),
                        pltpu.SemaphoreType.DMA(())])
    return pl.pallas_call(
        _moe_kernel,
        grid_spec=grid_spec,
        out_shape=jax.ShapeDtypeStruct((t * AW, LANES), F32),
        compiler_params=_cparams(1, VMEM_LIMIT),
        name="moe_grouped",
    )(off, row_pair, pbits, h2i, w_gu, w_down, b_gu.reshape(N_EXPERTS, 1, 2 * D_EXPERT),
      b_down.reshape(N_EXPERTS, 1, D_MODEL))


def _residual_kernel(x1_ref, moe_ref, g2_ref, nw_ref, oc_ref, ol_ref, *, normalize, ctx_tiles):
    tm = x1_ref.shape[0]
    moe = jnp.concatenate([moe_ref[pl.ds(qd, tm, stride=AW), :] for qd in range(AW)], axis=1)
    x2 = x1_ref[...] + g2_ref[...] * moe
    if normalize:
        ms = jnp.mean(x2 * x2, axis=-1, keepdims=True)
        x2 = (x2 * lax.rsqrt(ms + EPS)) * nw_ref[...]

    @pl.when(pl.program_id(0) < ctx_tiles)
    def _():
        oc_ref[...] = x2

    @pl.when(pl.program_id(0) >= ctx_tiles)
    def _():
        ol_ref[...] = x2


def _residual(x1, moe_i, modr, norm_w, *, normalize, t_ctx):
    t = x1.shape[0]
    tm = 512
    upt = tm // UNIT
    ctx_tiles = t_ctx // tm
    return pl.pallas_call(
        functools.partial(_residual_kernel, normalize=normalize, ctx_tiles=ctx_tiles),
        grid=(t // tm,),
        in_specs=[pl.BlockSpec((tm, D_MODEL), lambda m: (m, 0)),
                  pl.BlockSpec((tm * AW, LANES), lambda m: (m, 0)),
                  pl.BlockSpec((None, None, 1, D_MODEL), lambda m: (m * upt, 5, 0, 0)),
                  pl.BlockSpec((1, D_MODEL), lambda m: (0, 0))],
        out_specs=_two_source_specs(tm, D_MODEL, ctx_tiles, 1),
        out_shape=[jax.ShapeDtypeStruct((t_ctx, D_MODEL), F32),
                   jax.ShapeDtypeStruct((t - t_ctx, D_MODEL), F32)],
        compiler_params=_cparams(1),
        name="ffn_residual_norm",
    )(x1, moe_i, modr, norm_w.reshape(1, D_MODEL))


def _route_kernel(er_ref, cnt_ref, off_ref, pair_ref, offs, *, n_pairs):
    def prefix(e, run):
        offs[e] = run
        off_ref[e] = run
        return run + cnt_ref[e]

    total = lax.fori_loop(0, N_EXPERTS, prefix, jnp.int32(0))
    for e in range(N_EXPERTS, off_ref.shape[0]):
        off_ref[e] = total

    def pad(i, c):
        pair_ref[n_pairs + i] = 0
        return c

    lax.fori_loop(0, ROUTE_PAD, pad, 0)

    def place(i8, c):
        for ii in range(8):
            i = i8 * 8 + ii
            er = er_ref[i]
            pos = offs[jnp.bitwise_and(er, N_EXPERTS - 1)] + lax.shift_right_logical(er, RANK_SHIFT)
            pair_ref[pos] = i
        return c

    lax.fori_loop(0, n_pairs // 8, place, 0)


def _route(er_flat, cnt):
    n = er_flat.shape[0]
    smem = pl.BlockSpec(memory_space=pltpu.SMEM)
    return pl.pallas_call(
        functools.partial(_route_kernel, n_pairs=n),
        in_specs=[smem, smem],
        out_specs=[smem, smem],
        out_shape=[jax.ShapeDtypeStruct((N_EXPERTS + 8,), jnp.int32),
                   jax.ShapeDtypeStruct((n + ROUTE_PAD,), jnp.int32)],
        scratch_shapes=[pltpu.SMEM((N_EXPERTS,), jnp.int32)],
        name="moe_route",
    )(er_flat, cnt)


def kernel(x_prompt, x_sample, state_ssm, c, c_ctx, w_ada, b_ada, norm_mix, norm_ffn, w_in, ssm_conv_w, ssm_conv_b, dt_bias, a_log, d_skip, ssm_norm_w, w_ssd_out, conf_dw_w, conf_dw_b, conf_ln_w, conf_ln_b, w_conf_out, b_conf_out, b_gate, w_o, w_router, b_router, w_gu, b_gu, w_down, b_down, norm_final):
    n_ctx, seq_ctx, _ = x_prompt.shape
    n_lat, seq_lat, _ = x_sample.shape
    depth = w_in.shape[0]
    t_ctx, t_lat = n_ctx * seq_ctx, n_lat * seq_lat
    t_all = t_ctx + t_lat
    assert seq_ctx == UNIT and seq_lat % 1024 == 0 and t_ctx % 1024 == 0
    assert n_lat + 1 <= 8 and seq_lat % GRID_W == 0 and UNIT % GRID_W == 0

    x_ctx, x_lat = x_prompt.reshape(t_ctx, D_MODEL), x_sample.reshape(t_lat, D_MODEL)
    cvec =jnp.concatenate([c_ctx[None, :], c, jnp.zeros((8 - 1 - n_lat, D_MODEL), F32)], axis=0)
    unit_row = jnp.concatenate([jnp.zeros((t_ctx // UNIT,), jnp.int32),
                                1 + jnp.arange(t_lat // UNIT, dtype=jnp.int32) // (seq_lat // UNIT)])
    assert XW == TOP_K and N_EXPERTS == 1 << RANK_SHIFT

    new_states = []
    for l in range(depth):
        mod = _ada_mod(cvec, w_ada[l], b_ada[l])
        modr = mod[unit_row].reshape(t_all // UNIT, 6, 1, D_MODEL)

        w_a, w_b, w_dt = _in_proj_weights(w_in[l])
        proj, dt_raw = _in_proj(x_ctx, x_lat, norm_mix[l], modr, w_a, w_b, w_dt)

        pad_h = lambda v: jnp.pad(v.reshape(2, 1, N_HEADS), ((0, 0), (0, 0), (0, LANES - N_HEADS)))
        dtb, alog = pad_h(dt_bias[l]), pad_h(a_log[l])
        dskip = jnp.repeat(d_skip[l], HEAD_DIM).reshape(1, D_INNER)
        nw = ssm_norm_w[l].reshape(1, D_INNER)
        h0t = state_ssm[:, l].astype(F32).reshape(n_lat, 2, D_INNER, D_STATE)

        xbc_ctx = _ssm_conv(proj, ssm_conv_w[l], ssm_conv_b[l], seq=seq_ctx, nseq=n_ctx, row0=0)
        xbc_lat = _ssm_conv(proj, ssm_conv_w[l], ssm_conv_b[l], seq=seq_lat, nseq=n_lat, row0=t_ctx)
        y_ctx, st_ctx = _ssd(xbc_ctx, proj, dt_raw, dtb, alog, dskip, nw, None,
                             seq=seq_ctx, nseq=n_ctx, row0=0, write_state=True)
        (y_lat,) = _ssd(xbc_lat, proj, dt_raw, dtb, alog, dskip, nw, h0t,
                        seq=seq_lat, nseq=n_lat, row0=t_ctx, write_state=False)
        u_ctx = _conformer(proj, conf_dw_w[l], conf_dw_b[l], conf_ln_w[l], conf_ln_b[l],
                           seg=seq_ctx, ntok=t_ctx, row0=0)
        u_lat = _conformer(proj, conf_dw_w[l], conf_dw_b[l], conf_ln_w[l], conf_ln_b[l],
                           seg=GRID_W, ntok=t_lat, row0=t_ctx)
        wr = jnp.pad(w_router[l], ((0, 0), (0, LANES - N_EXPERTS)))
        wr_hi = wr.astype(BF16)
        wr_lo = (wr - wr_hi.astype(F32)).astype(BF16)
        br = jnp.pad(b_router[l], (0, LANES - N_EXPERTS)).reshape(1, LANES)
        x1, h2i, er, topp, cnt = _mix_out(
            x_ctx, x_lat, y_ctx, y_lat, u_ctx, u_lat, proj, modr, w_ssd_out[l].astype(BF16),
            w_conf_out[l].astype(BF16), b_conf_out[l], b_gate[l], w_o[l].astype(BF16), norm_ffn[l],
            jnp.concatenate([wr_hi, wr_lo, wr_hi], axis=0), br)

        off, row_pair = _route(er[:, :TOP_K].reshape(-1), cnt[0, :N_EXPERTS])
        pbits = lax.bitcast_convert_type(topp[:, :TOP_K], jnp.int32).reshape(-1)
        moe_i = _moe_grouped(h2i, off, row_pair, pbits, w_gu[l], b_gu[l], w_down[l], b_down[l])
        last = l + 1 == depth
        x_ctx, x_lat = _residual(x1, moe_i, modr, norm_final if last else norm_ffn[l], normalize=last, t_ctx=t_ctx)
        new_states.append(st_ctx.reshape(n_ctx, 1, 2, N_HEADS, HEAD_DIM, D_STATE))

    y_prompt = x_ctx.reshape(n_ctx, seq_ctx, D_MODEL)
    y_sample = x_lat.reshape(n_lat, seq_lat, D_MODEL)
    new_state_ssm = new_states[0] if depth == 1 else jnp.concatenate(new_states, axis=1)
    return (y_prompt, y_sample, new_state_ssm)
```

```python
import functools

import jax
import jax.numpy as jnp
from jax import lax
from jax.experimental import pallas as pl
from jax.experimental.pallas import tpu as pltpu

F32 = jnp.float32
BF16 = jnp.bfloat16
U32 = jnp.uint32

D_MODEL = 1024
GRID_W = 64
D_INNER = 2 * D_MODEL
HEAD_DIM = 64
N_HEADS = D_INNER // HEAD_DIM
N_GROUPS = 8
HEADS_PER_GROUP = N_HEADS // N_GROUPS
D_STATE = 128
SSM_CONV = 5
CHUNK = 128
D_XBC = D_INNER + 2 * N_GROUPS * D_STATE
D_CONF = D_MODEL
CONF_KERNEL = 31
N_EXPERTS = 32
TOP_K = 4
D_EXPERT = D_MODEL
SWIGLU_LIMIT = 7.0
SWIGLU_ALPHA = 1.702
EPS = 1e-6

LANES = 128
UNIT = 256
GROUP_W = D_INNER // N_GROUPS
SSD_CHUNKS = 2
SSD_GROUP_UNROLL = 2
MIX_CHAINS = 2
MOE_TILE = 128
ROUTE_PAD = 2 * MOE_TILE
MOE_CAST_ROWS = 16
XW = D_MODEL // (2 * LANES)
AW = D_MODEL // LANES
RANK_SHIFT = 5
VMEM_LIMIT = 62 * 1024 * 1024

COL_Z = 0
COL_XBC = D_INNER
COL_GLU = D_XBC + D_INNER
COL_GATE = D_XBC + D_INNER + 2 * D_CONF
N_MAIN = D_XBC + D_INNER + 2 * D_CONF + 2 * D_MODEL


def _sigmoid(x):
    return 1.0 / (1.0 + jnp.exp(-x))


def _silu(x):
    return x * _sigmoid(x)


def _cparams(n_axes, vmem=None):
    return pltpu.CompilerParams(
        dimension_semantics=("arbitrary",) * n_axes,
        vmem_limit_bytes=vmem)


def _ada_kernel(c_ref, w_ref, b_ref, o_ref):
    s = _silu(c_ref[...])
    o_ref[...] = jnp.dot(s.astype(BF16), w_ref[...].astype(BF16),
                         preferred_element_type=F32) + b_ref[...]


def _ada_mod(cvec, w_ada, b_ada):
    n = w_ada.shape[1]
    tn = 1536
    return pl.pallas_call(
        _ada_kernel,
        grid=(n // tn,),
        in_specs=[pl.BlockSpec((8, D_MODEL), lambda j: (0, 0)),
                  pl.BlockSpec((D_MODEL, tn), lambda j: (0, j)),
                  pl.BlockSpec((1, tn), lambda j: (0, j))],
        out_specs=pl.BlockSpec((8, tn), lambda j: (0, j)),
        out_shape=jax.ShapeDtypeStruct((8, n), F32),
        compiler_params=_cparams(1, 40 * 1024 * 1024),
        name="ada_mod",
    )(cvec, w_ada, b_ada.reshape(1, n))


def _wprep_kernel(w_ref, wn_ref, oa_ref, ob_ref, odt_ref, *, a_tiles):
    n = pl.program_id(0)
    tn = w_ref.shape[1]
    shift = 2 * N_HEADS

    @pl.when(n < a_tiles)
    def _():
        oa_ref[...] = w_ref[...].astype(BF16)

    @pl.when(n >= a_tiles)
    def _():
        lane = lax.broadcasted_iota(jnp.int32, (1, LANES), 1)
        for c in range(tn // LANES):
            lo = w_ref[:, c * LANES:(c + 1) * LANES]
            hi = w_ref[:, (c + 1) * LANES:(c + 2) * LANES] if c + 1 < tn // LANES else wn_ref[...]
            ob_ref[:, c * LANES:(c + 1) * LANES] = jnp.where(
                lane < LANES - shift, pltpu.roll(lo, LANES - shift, 1), pltpu.roll(hi, LANES - shift, 1)).astype(BF16)

    @pl.when(n == a_tiles)
    def _():
        lane = lax.broadcasted_iota(jnp.int32, (1, LANES), 1)
        first = w_ref[:, 0:LANES]
        odt_ref[:, 0:LANES] = jnp.where(lane < N_HEADS, first, 0.0).astype(BF16)
        odt_ref[:, LANES:2 * LANES] = jnp.where(lane < N_HEADS, pltpu.roll(first, LANES - N_HEADS, 1), 0.0).astype(BF16)


def _in_proj_weights(w_in):
    tn = 2048
    n_a = D_INNER + D_XBC
    n_b = 2 * D_CONF + 2 * D_MODEL
    a_tiles = n_a // tn
    assert n_a % tn == 0 and n_b % tn == 0 and 2 * N_HEADS < LANES
    n_tiles = a_tiles + n_b // tn
    nxt = [w_in[:, (n + 1) * tn:(n + 1) * tn + LANES] for n in range(a_tiles, n_tiles)]
    nxt[-1] = jnp.pad(nxt[-1], ((0, 0), (0, LANES - nxt[-1].shape[1])))
    return pl.pallas_call(
        functools.partial(_wprep_kernel, a_tiles=a_tiles),
        grid=(n_tiles,),
        in_specs=[pl.BlockSpec((D_MODEL, tn), lambda n: (0, n)),
                  pl.BlockSpec((D_MODEL, LANES), lambda n: (0, jnp.maximum(n - a_tiles, 0)))],
        out_specs=[pl.BlockSpec((D_MODEL, tn), lambda n: (0, jnp.minimum(n, a_tiles - 1))),
                   pl.BlockSpec((D_MODEL, tn), lambda n: (0, jnp.maximum(n - a_tiles, 0))),
                   pl.BlockSpec((D_MODEL, 2 * LANES), lambda n: (0, 0))],
        out_shape=[jax.ShapeDtypeStruct((D_MODEL, n_a), BF16),
                   jax.ShapeDtypeStruct((D_MODEL, n_b), BF16),
                   jax.ShapeDtypeStruct((D_MODEL, 2 * LANES), BF16)],
        compiler_params=_cparams(1, 48 * 1024 * 1024),
        name="in_proj_weights",
    )(w_in, jnp.concatenate(nxt, axis=1))


def _inproj_kernel(xc_ref, xl_ref, nw_ref, sc_ref, sh_ref, wa_ref, wb_ref, wdt_ref, o_ref, dt_ref, h_scr, *,
                   ctx_tiles, a_tiles):
    @pl.when(pl.program_id(1) == 0)
    def _():
        xf = jnp.where(pl.program_id(0) < ctx_tiles, xc_ref[...], xl_ref[...])
        ms = jnp.mean(xf * xf, axis=-1, keepdims=True)
        hn = (xf * lax.rsqrt(ms + EPS)) * nw_ref[...]
        hn = hn * (1.0 + sc_ref[...]) + sh_ref[...]
        hb = hn.astype(BF16)
        h_scr[...] = hb
        dt_ref[...] = jnp.dot(hb, wdt_ref[...], preferred_element_type=F32)

    @pl.when(pl.program_id(1) < a_tiles)
    def _():
        o_ref[...] = jnp.dot(h_scr[...], wa_ref[...], preferred_element_type=F32).astype(BF16)

    @pl.when(pl.program_id(1) >= a_tiles)
    def _():
        o_ref[...] = jnp.dot(h_scr[...], wb_ref[...], preferred_element_type=F32).astype(BF16)


def _two_source_specs(tm, width, ctx_tiles, nargs):
    if nargs == 1:
        return [pl.BlockSpec((tm, width), lambda m: (jnp.minimum(m, ctx_tiles - 1), 0)),
                pl.BlockSpec((tm, width), lambda m: (jnp.maximum(m - ctx_tiles, 0), 0))]
    return [pl.BlockSpec((tm, width), lambda m, n: (jnp.minimum(m, ctx_tiles - 1), 0)),
            pl.BlockSpec((tm, width), lambda m, n: (jnp.maximum(m - ctx_tiles, 0), 0))]


def _in_proj(x_ctx, x_lat, norm_w, modr, w_a, w_b, w_dt):
    t_ctx = x_ctx.shape[0]
    t = t_ctx + x_lat.shape[0]
    tm, tn = 1024, 2048
    upt = tm // UNIT
    a_tiles = w_a.shape[1] // tn
    return pl.pallas_call(
        functools.partial(_inproj_kernel, ctx_tiles=t_ctx // tm, a_tiles=a_tiles),
        grid=(t // tm, N_MAIN // tn),
        in_specs=_two_source_specs(tm, D_MODEL, t_ctx // tm, 2) + [
                  pl.BlockSpec((1, D_MODEL), lambda m, n: (0, 0)),
                  pl.BlockSpec((None, None, 1, D_MODEL), lambda m, n: (m * upt, 1, 0, 0)),
                  pl.BlockSpec((None, None, 1, D_MODEL), lambda m, n: (m * upt, 0, 0, 0)),
                  pl.BlockSpec((D_MODEL, tn), lambda m, n: (0, jnp.minimum(n, a_tiles - 1))),
                  pl.BlockSpec((D_MODEL, tn), lambda m, n: (0, jnp.maximum(n - a_tiles, 0))),
                  pl.BlockSpec((D_MODEL, 2 * LANES), lambda m, n: (0, 0))],
        out_specs=[pl.BlockSpec((tm, tn), lambda m, n: (m, n)),
                   pl.BlockSpec((tm, 2 * LANES), lambda m, n: (m, 0))],
        out_shape=[jax.ShapeDtypeStruct((t, N_MAIN), BF16),
                   jax.ShapeDtypeStruct((t, 2 * LANES), F32)],
        scratch_shapes=[pltpu.VMEM((tm, D_MODEL), BF16)],
        compiler_params=_cparams(2, 48 * 1024 * 1024),
        name="in_proj",
    )(x_ctx, x_lat, norm_w.reshape(1, D_MODEL), modr, modr, w_a, w_b, w_dt)


def _ssm_conv_kernel(x_ref, w_ref, b_ref, o_ref, pad_scr, *, seq):
    cb = x_ref.shape[1]
    pad_scr[0:8, :] = jnp.zeros((8, cb), F32)
    pad_scr[8 + seq:16 + seq, :] = jnp.zeros((8, cb), F32)
    pad_scr[8:8 + seq, :] = x_ref[...].astype(F32)
    half = SSM_CONV // 2
    rows, cw = 256, 512
    for c0 in range(0, cb, cw):
        cs = slice(c0, c0 + cw)
        for r0 in range(0, seq, rows):
            acc = jnp.broadcast_to(b_ref[:, cs], (rows, cw))
            for k in range(SSM_CONV):
                acc = acc + w_ref[k:k + 1, cs] * pad_scr[8 + r0 + k - half:8 + r0 + k - half + rows, cs]
            o_ref[r0:r0 + rows, cs] = _silu(acc).astype(BF16)


def _ssm_conv(proj, conv_w, conv_b, *, seq, nseq, row0):
    cb = 2048
    blk0 = row0 // seq
    return pl.pallas_call(
        functools.partial(_ssm_conv_kernel, seq=seq),
        grid=(nseq, D_XBC // cb),
        in_specs=[pl.BlockSpec((seq, cb), lambda s, j: (blk0 + s, COL_XBC // cb + j)),
                  pl.BlockSpec((SSM_CONV, cb), lambda s, j: (0, j)),
                  pl.BlockSpec((1, cb), lambda s, j: (0, j))],
        out_specs=pl.BlockSpec((seq, cb), lambda s, j: (s, j)),
        out_shape=jax.ShapeDtypeStruct((nseq * seq, D_XBC), BF16),
        scratch_shapes=[pltpu.VMEM((seq + 16, cb), F32)],
        compiler_params=_cparams(2, 40 * 1024 * 1024),
        name="ssm_conv",
    )(proj, conv_w, conv_b.reshape(1, D_XBC))


def _split_bf16(v):
    hi = v.astype(BF16)
    lo = (v - hi.astype(F32)).astype(BF16)
    return jnp.concatenate([hi, lo], axis=1)


def _head_select_matrices():
    j = jnp.arange(2 * LANES, dtype=jnp.int32)[:, None] % LANES
    full = (j == (jnp.arange(N_HEADS * LANES, dtype=jnp.int32)[None, :] // LANES)).astype(BF16)
    exp = (j == (jnp.arange(D_INNER, dtype=jnp.int32)[None, :] // HEAD_DIM)).astype(BF16)
    return full, exp


def _ssd_kernel(*refs, nc, has_h0, write_state):
    (xs_ref, b_ref, c_ref, z_ref, dtr_ref, dtb_ref, alog_ref, dskip_ref, nw_ref, self_ref, sele_ref), rest = \
        refs[:11], refs[11:]
    if has_h0:
        h0_ref, rest = rest[0], rest[1:]
    y_ref, rest = rest[0], rest[1:]
    if write_state:
        st_ref, rest = rest[0], rest[1:]
    h_scr, ybuf, ychunk, colb_scr, wexp_scr, eexp_scr, texp_scr, rowq_scr = rest

    q = CHUNK
    phase = pl.program_id(1)
    c = pl.program_id(2)
    is_fwd = phase == 1
    c_eff = jnp.where(is_fwd, c, nc - 1 - c)

    @pl.when(c == 0)
    def _():
        if has_h0:
            for j in range(D_INNER // LANES):
                h_scr[:, j * LANES:(j + 1) * LANES] = h0_ref[j * LANES:(j + 1) * LANES, :].T
        else:
            h_scr[...] = jnp.zeros(h_scr.shape, F32)

    nrow = SSD_CHUNKS * q
    xdt = dtr_ref[...] + dtb_ref[...]
    dt = jnp.maximum(xdt, 0.0) + jnp.log(1.0 + jnp.exp(-jnp.abs(xdt)))
    a = -jnp.exp(alog_ref[...])
    adt = dt * a
    sgn = jnp.where(is_fwd, 1, -1)
    row = lax.broadcasted_iota(jnp.int32, (nrow, nrow), 0)
    col = lax.broadcasted_iota(jnp.int32, (nrow, nrow), 1)
    tri_blk = jnp.logical_and(row // q == col // q, (col - row) * sgn <= 0)
    cum2 = jnp.dot(tri_blk.astype(BF16), _split_bf16(adt), preferred_element_type=F32)
    cum = cum2[:, :LANES] + cum2[:, LANES:]
    tots = [jnp.where(is_fwd, cum[k * q + q - 1:k * q + q, :], cum[k * q:k * q + 1, :]) for k in range(SSD_CHUNKS)]
    tot_rows = jnp.concatenate([jnp.broadcast_to(t, (q, LANES)) for t in tots], axis=0)
    rowq = (cum - jnp.log(dt)).T
    for k in range(SSD_CHUNKS):
        rowq_scr[k] = rowq[:, k * q:(k + 1) * q]
    wdec = dt * jnp.exp(tot_rows - cum)
    eo = jnp.exp(cum)
    etot = jnp.concatenate([jnp.broadcast_to(jnp.exp(t), (8, LANES)) for t in tots], axis=0)
    colb_scr[...] = jnp.dot(_split_bf16(cum), self_ref[...], preferred_element_type=F32)
    expanded = jnp.dot(jnp.concatenate([_split_bf16(wdec), _split_bf16(eo), _split_bf16(etot)], axis=0),
                       sele_ref[...], preferred_element_type=F32)
    wexp_scr[...] = expanded[0:nrow]
    eexp_scr[...] = expanded[nrow:2 * nrow]
    for k in range(SSD_CHUNKS):
        texp_scr[k] = expanded[2 * nrow + 8 * k:2 * nrow + 8 * (k + 1)]
    lrow = lax.broadcasted_iota(jnp.int32, (q, q), 0)
    lcol = lax.broadcasted_iota(jnp.int32, (q, q), 1)
    tri = (lcol - lrow) * sgn <= 0
    lane_g = lax.broadcasted_iota(jnp.int32, (1, GROUP_W), 1)
    head_mask = [(lane_g // HEAD_DIM == hh).astype(BF16) for hh in range(HEADS_PER_GROUP)]
    neg_inf = jnp.float32(-jnp.inf)

    def group_body(g, carry):
        gs = pl.ds(pl.multiple_of(g * GROUP_W, GROUP_W), GROUP_W)
        ns = pl.ds(pl.multiple_of(g * D_STATE, D_STATE), D_STATE)
        for k in range(SSD_CHUNKS):
            ci = jnp.where(is_fwd, k, SSD_CHUNKS - 1 - k)
            r0 = pl.multiple_of(ci * q, q)
            rs = pl.ds(r0, q)
            bg = b_ref[rs, ns]
            cg = c_ref[rs, ns]
            scores = lax.dot_general(cg, bg, (((1,), (1,)), ((), ())), preferred_element_type=F32)
            xs_g = xs_ref[rs, gs]
            ms, xb = [], []
            for hh in range(HEADS_PER_GROUP):
                h = HEADS_PER_GROUP * g + hh
                colb = colb_scr[rs, pl.ds(pl.multiple_of(h * LANES, LANES), LANES)]
                seg = jnp.where(tri, colb - rowq_scr[ci, pl.ds(h, 1), :], neg_inf)
                ms.append((scores * jnp.exp(seg)).astype(BF16))
                xb.append(xs_g * head_mask[hh])
            y_diag = jnp.dot(jnp.concatenate(ms, axis=1), jnp.concatenate(xb, axis=0),
                             preferred_element_type=F32)
            xd = (xs_g.astype(F32) * wexp_scr[rs, gs]).astype(BF16)
            bt = bg.astype(F32).T.astype(BF16)
            st = jnp.dot(bt, xd, preferred_element_type=F32)
            hg = h_scr[:, gs]
            y_off = jnp.dot(cg, hg.astype(BF16), preferred_element_type=F32) * eexp_scr[rs, gs]
            ychunk[rs, gs] = y_diag + y_off
            h_scr[:, gs] = hg * texp_scr[ci, 0:1, gs] + st
        return carry

    lax.fori_loop(0, N_GROUPS, group_body, 0, unroll=SSD_GROUP_UNROLL)

    rows = pl.ds(pl.multiple_of(c_eff * nrow, nrow), nrow)

    @pl.when(jnp.logical_not(is_fwd))
    def _():
        ybuf[rows, :] = ychunk[...]

    @pl.when(is_fwd)
    def _():
        zf = z_ref[...].astype(F32)
        yt = ychunk[...] + ybuf[rows, :] + xs_ref[...].astype(F32) * dskip_ref[...]
        yz = yt * _silu(zf)
        for g in range(N_GROUPS):
            gs = slice(g * GROUP_W, (g + 1) * GROUP_W)
            blk = yz[:, gs]
            ms = jnp.mean(blk * blk, axis=-1, keepdims=True)
            y_ref[:, gs] = (blk * lax.rsqrt(ms + EPS) * nw_ref[:, gs]).astype(BF16)

    if write_state:
        @pl.when(c == nc - 1)
        def _():
            for j in range(D_INNER // LANES):
                st_ref[j * LANES:(j + 1) * LANES, :] = h_scr[:, j * LANES:(j + 1) * LANES].T


def _ssd(xbc_c, proj, dt_raw, dt_bias, a_log, d_skip, norm_w, h0t, *, seq, nseq, row0, write_state):
    rb = SSD_CHUNKS * CHUNK
    nc = seq // rb
    blk0 = row0 // rb
    has_h0 = h0t is not None
    assert seq % rb == 0 and row0 % rb == 0

    def tok(s, p, c):
        return s * nc + p * c + (1 - p) * (nc - 1 - c)

    in_specs = [
        pl.BlockSpec((rb, D_INNER), lambda s, p, c: (tok(s, p, c), 0)),
        pl.BlockSpec((rb, N_GROUPS * D_STATE), lambda s, p, c: (tok(s, p, c), 2)),
        pl.BlockSpec((rb, N_GROUPS * D_STATE), lambda s, p, c: (tok(s, p, c), 3)),
        pl.BlockSpec((rb, D_INNER), lambda s, p, c: (blk0 + s * nc + p * c, COL_Z // D_INNER)),
        pl.BlockSpec((rb, LANES), lambda s, p, c: (blk0 + tok(s, p, c), 1 - p)),
        pl.BlockSpec((None, 1, LANES), lambda s, p, c: (1 - p, 0, 0)),
        pl.BlockSpec((None, 1, LANES), lambda s, p, c: (1 - p, 0, 0)),
        pl.BlockSpec((1, D_INNER), lambda s, p, c: (0, 0)),
        pl.BlockSpec((1, D_INNER), lambda s, p, c: (0, 0)),
        pl.BlockSpec((2 * LANES, N_HEADS * LANES), lambda s, p, c: (0, 0)),
        pl.BlockSpec((2 * LANES, D_INNER), lambda s, p, c: (0, 0)),
    ]
    sel_full, sel_exp = _head_select_matrices()
    args = [xbc_c, xbc_c, xbc_c, proj, dt_raw, dt_bias, a_log, d_skip, norm_w, sel_full, sel_exp]
    if has_h0:
        in_specs.append(pl.BlockSpec((None, None, D_INNER, D_STATE), lambda s, p, c: (s, 1 - p, 0, 0)))
        args.append(h0t)
    out_specs = [pl.BlockSpec((rb, D_INNER), lambda s, p, c: (s * nc + p * c, 0))]
    out_shape = [jax.ShapeDtypeStruct((nseq * seq, D_INNER), BF16)]
    if write_state:
        out_specs.append(pl.BlockSpec((None, None, D_INNER, D_STATE), lambda s, p, c: (s, 1 - p, 0, 0)))
        out_shape.append(jax.ShapeDtypeStruct((nseq, 2, D_INNER, D_STATE), F32))
    return pl.pallas_call(
        functools.partial(_ssd_kernel, nc=nc, has_h0=has_h0, write_state=write_state),
        grid=(nseq, 2, nc),
        in_specs=in_specs,
        out_specs=out_specs,
        out_shape=out_shape,
        scratch_shapes=[pltpu.VMEM((D_STATE, D_INNER), F32),
                        pltpu.VMEM((seq, D_INNER), F32),
                        pltpu.VMEM((rb, D_INNER), F32),
                        pltpu.VMEM((rb, N_HEADS * LANES), F32),
                        pltpu.VMEM((rb, D_INNER), F32),
                        pltpu.VMEM((rb, D_INNER), F32),
                        pltpu.VMEM((SSD_CHUNKS, 8, D_INNER), F32),
                        pltpu.VMEM((SSD_CHUNKS, LANES, CHUNK), F32)],
        compiler_params=_cparams(3, 48 * 1024 * 1024),
        name="ssd_scan",
    )(*args)


def _conf_kernel(glu_ref, w_ref, b_ref, lnw_ref, lnb_ref, o_ref, pad_scr, sh_scr, conv_scr, *, seg):
    rows = glu_ref.shape[0]
    nseg = rows // seg
    half = CONF_KERNEL // 2
    front = 16
    span = seg + 24
    a = glu_ref[:, :D_CONF].astype(F32)
    b = glu_ref[:, D_CONF:].astype(F32)
    u = a * _sigmoid(b)
    for i in range(nseg):
        pad_scr[i, 0:front, :] = jnp.zeros((front, D_CONF), F32)
        pad_scr[i, front + seg:front + seg + 16, :] = jnp.zeros((16, D_CONF), F32)
        pad_scr[i, front:front + seg, :] = u[i * seg:(i + 1) * seg, :]
    for s in range(8):
        for i in range(nseg):
            for cbi in range(D_CONF // LANES):
                cs = slice(cbi * LANES, (cbi + 1) * LANES)
                for r0 in range(0, span, 56):
                    n = min(56, span - r0)
                    sh_scr[s, i, r0:r0 + n, cs] = pad_scr[i, s + r0:s + r0 + n, cs]
    rb = 64
    for i in range(nseg):
        for cbi in range(D_CONF // LANES):
            cs = slice(cbi * LANES, (cbi + 1) * LANES)
            for r0 in range(0, seg, rb):
                acc = jnp.broadcast_to(b_ref[:, cs], (rb, LANES))
                for k in range(CONF_KERNEL):
                    start = front + r0 + k - half
                    al = start - start % 8
                    acc = acc + w_ref[k:k + 1, cs] * sh_scr[start % 8, i, al:al + rb, cs]
                conv_scr[i * seg + r0:i * seg + r0 + rb, cs] = acc
    v = conv_scr[...]
    mu = jnp.mean(v, axis=-1, keepdims=True)
    vc = v - mu
    var = jnp.mean(vc * vc, axis=-1, keepdims=True)
    ln = (vc * lax.rsqrt(var + EPS)) * lnw_ref[...] + lnb_ref[...]
    o_ref[...] = _silu(ln).astype(BF16)


def _conformer(proj, dw_w, dw_b, ln_w, ln_b, *, seg, ntok, row0):
    rows = UNIT
    blk0 = row0 // rows
    return pl.pallas_call(
        functools.partial(_conf_kernel, seg=seg),
        grid=(ntok // rows,),
        in_specs=[pl.BlockSpec((rows, 2 * D_CONF), lambda i: (blk0 + i, COL_GLU // (2 * D_CONF))),
                  pl.BlockSpec((CONF_KERNEL, D_CONF), lambda i: (0, 0)),
                  pl.BlockSpec((1, D_CONF), lambda i: (0, 0)),
                  pl.BlockSpec((1, D_CONF), lambda i: (0, 0)),
                  pl.BlockSpec((1, D_CONF), lambda i: (0, 0))],
        out_specs=pl.BlockSpec((rows, D_CONF), lambda i: (i, 0)),
        out_shape=jax.ShapeDtypeStruct((ntok, D_CONF), BF16),
        scratch_shapes=[pltpu.VMEM((rows // seg, seg + 32, D_CONF), F32),
                        pltpu.VMEM((8, rows // seg, seg + 24, D_CONF), F32),
                        pltpu.VMEM((rows, D_CONF), F32)],
        compiler_params=_cparams(1, 40 * 1024 * 1024),
        name="conformer_conv",
    )(proj, dw_w, dw_b.reshape(1, D_CONF), ln_w.reshape(1, D_CONF), ln_b.reshape(1, D_CONF))


def _pack_halves(x):
    outs = []
    for cb in range(x.shape[1] // (2 * LANES)):
        hi = x[:, cb * 2 * LANES:cb * 2 * LANES + LANES].astype(BF16).astype(F32)
        lo = x[:, cb * 2 * LANES + LANES:(cb + 1) * 2 * LANES].astype(BF16).astype(F32)
        hw = lax.bitcast_convert_type(hi, U32)
        lw = jnp.right_shift(lax.bitcast_convert_type(lo, U32), jnp.uint32(16))
        outs.append(jnp.bitwise_or(hw, lw))
    return jnp.concatenate(outs, axis=1)


def _mix_kernel(xc_ref, xl_ref, yc_ref, yl_ref, uc_ref, ul_ref, gate_ref, g1_ref, sc2_ref, sh2_ref,
                wssd_ref, wconf_ref, bconf_ref, bgate_ref, wo_ref, nffn_ref, wr_ref, br_ref,
                x1_ref, h2i_ref, er_ref, topp_ref, cnt_ref, cnt_scr, *, ctx_tiles):
    @pl.when(pl.program_id(0) == 0)
    def _():
        cnt_scr[...] = jnp.zeros(cnt_scr.shape, F32)

    is_ctx = pl.program_id(0) < ctx_tiles
    rows = x1_ref.shape[0]
    sub = rows // MIX_CHAINS
    lane = lax.broadcasted_iota(jnp.int32, (sub, LANES), 1)
    lane_f = lane.astype(F32)
    neg_inf = jnp.float32(-jnp.inf)

    def chain(c):
        rs = slice(c * sub, (c + 1) * sub)
        y_in = jnp.where(is_ctx, yc_ref[rs, :], yl_ref[rs, :])
        u_in = jnp.where(is_ctx, uc_ref[rs, :], ul_ref[rs, :])
        o_ssd = jnp.dot(y_in, wssd_ref[...], preferred_element_type=F32)
        o_conf = jnp.dot(u_in, wconf_ref[...], preferred_element_type=F32) + bconf_ref[...]
        gates = _sigmoid(gate_ref[rs, :].astype(F32) + bgate_ref[...])
        merged = gates[:, :D_MODEL] * o_ssd + gates[:, D_MODEL:] * o_conf
        out = jnp.dot(merged.astype(BF16), wo_ref[...], preferred_element_type=F32)
        x1 = jnp.where(is_ctx, xc_ref[rs, :], xl_ref[rs, :]) + g1_ref[...] * out
        x1_ref[rs, :] = x1
        ms = jnp.mean(x1 * x1, axis=-1, keepdims=True)
        h2 = (x1 * lax.rsqrt(ms + EPS)) * nffn_ref[...]
        h2 = h2 * (1.0 + sc2_ref[...]) + sh2_ref[...]
        packed = _pack_halves(h2)
        for qd in range(XW):
            h2i_ref[pl.ds(c * sub * XW + qd, sub, stride=XW), :] = packed[:, qd * LANES:(qd + 1) * LANES]
        h_hi = h2.astype(BF16)
        h_lo = (h2 - h_hi.astype(F32)).astype(BF16)
        logits = jnp.dot(jnp.concatenate([h_hi, h_hi, h_lo], axis=1), wr_ref[...],
                         preferred_element_type=F32) + br_ref[...]
        work = jnp.where(lane < N_EXPERTS, logits, neg_inf)
        vals, idxs = [], []
        for _ in range(TOP_K):
            m = jnp.max(work, axis=-1, keepdims=True)
            idx = jnp.min(jnp.where(work == m, lane_f, jnp.float32(LANES)), axis=-1, keepdims=True)
            vals.append(m)
            idxs.append(idx)
            work = jnp.where(lane_f == idx, neg_inf, work)
        es = [jnp.exp(v - vals[0]) for v in vals]
        denom = es[0] + es[1] + es[2] + es[3]
        member = jnp.zeros((sub, LANES), F32)
        topp = jnp.zeros((sub, LANES), F32)
        for k in range(TOP_K):
            member = member + jnp.where(lane_f == idxs[k], 1.0, 0.0)
            topp = jnp.where(lane == k, es[k] / denom, topp)
        topp_ref[rs, :] = topp
        return idxs, member

    results = [chain(c) for c in range(MIX_CHAINS)]
    member = jnp.concatenate([m for _, m in results], axis=0)
    r_i = lax.broadcasted_iota(jnp.int32, (rows, rows), 0)
    c_i = lax.broadcasted_iota(jnp.int32, (rows, rows), 1)
    earlier = jnp.where(c_i < r_i, 1.0, 0.0).astype(BF16)
    rank = jnp.dot(earlier, member.astype(BF16), preferred_element_type=F32) + cnt_scr[...]
    cnt = cnt_scr[...] + jnp.sum(member, axis=0, keepdims=True)
    cnt_scr[...] = cnt
    cnt_ref[...] = jnp.broadcast_to(cnt, cnt_ref.shape).astype(jnp.int32)
    for c, (idxs, _) in enumerate(results):
        rs = slice(c * sub, (c + 1) * sub)
        er = jnp.zeros((sub, LANES), F32)
        for k in range(TOP_K):
            rank_k = jnp.sum(jnp.where(lane_f == idxs[k], rank[rs, :], 0.0), axis=-1, keepdims=True)
            er = jnp.where(lane == k, idxs[k] + N_EXPERTS * rank_k, er)
        er_ref[rs, :] = er.astype(jnp.int32)


def _mix_out(x_ctx, x_lat, y_ctx, y_lat, u_ctx, u_lat, proj, modr, w_ssd, w_conf, b_conf, b_gate, w_o, norm_ffn,
             w_router3, b_router):
    t_ctx = x_ctx.shape[0]
    t = t_ctx + x_lat.shape[0]
    tm = 512
    upt = tm // UNIT
    ctx_tiles = t_ctx // tm
    full = lambda shape: pl.BlockSpec(shape, lambda m: (0,) * len(shape), pipeline_mode=pl.Buffered(1))
    mod = lambda which: pl.BlockSpec((None, None, 1, D_MODEL), lambda m: (m * upt, which, 0, 0))
    return pl.pallas_call(
        functools.partial(_mix_kernel, ctx_tiles=ctx_tiles),
        grid=(t // tm,),
        in_specs=_two_source_specs(tm, D_MODEL, ctx_tiles, 1) + _two_source_specs(tm, D_INNER, ctx_tiles, 1)
                 + _two_source_specs(tm, D_CONF, ctx_tiles, 1) + [
                  pl.BlockSpec((tm, 2 * D_MODEL), lambda m: (m, COL_GATE // (2 * D_MODEL))),
                  mod(2), mod(4), mod(3),
                  full((D_INNER, D_MODEL)), full((D_CONF, D_MODEL)), full((1, D_MODEL)),
                  full((1, 2 * D_MODEL)), full((D_MODEL, D_MODEL)), full((1, D_MODEL)),
                  full((3 * D_MODEL, LANES)), full((1, LANES))],
        out_specs=[pl.BlockSpec((tm, D_MODEL), lambda m: (m, 0)),
                   pl.BlockSpec((tm * XW, LANES), lambda m: (m, 0)),
                   pl.BlockSpec((tm, LANES), lambda m: (m, 0)),
                   pl.BlockSpec((tm, LANES), lambda m: (m, 0)),
                   pl.BlockSpec((8, LANES), lambda m: (0, 0))],
        out_shape=[jax.ShapeDtypeStruct((t, D_MODEL), F32),
                   jax.ShapeDtypeStruct((t * XW, LANES), U32),
                   jax.ShapeDtypeStruct((t, LANES), jnp.int32),
                   jax.ShapeDtypeStruct((t, LANES), F32),
                   jax.ShapeDtypeStruct((8, LANES), jnp.int32)],
        scratch_shapes=[pltpu.VMEM((1, LANES), F32)],
        compiler_params=_cparams(1, 56 * 1024 * 1024),
        name="mix_out_router",
    )(x_ctx, x_lat, y_ctx, y_lat, u_ctx, u_lat, proj, modr, modr, modr, w_ssd, w_conf, b_conf.reshape(1, D_MODEL),
      b_gate.reshape(1, 2 * D_MODEL), w_o, norm_ffn.reshape(1, D_MODEL), w_router3, b_router)


def _moe_kernel(off_ref, pair_ref, pb_ref, h2i_ref, wgu_hbm, wd_hbm, bgu_ref, bd_ref, out_ref,
                acc, x_scr, y_scr, stage_gu, stage_d, wgu_b, wd_b, wsem, osem):
    e = pl.program_id(0)
    tr = MOE_TILE

    def weight_copies(ex):
        return (pltpu.make_async_copy(wgu_hbm.at[ex], stage_gu, wsem.at[0]),
                pltpu.make_async_copy(wd_hbm.at[ex], stage_d, wsem.at[1]))

    @pl.when(e == 0)
    def _():
        acc[...] = jnp.zeros(acc.shape, F32)
        y_scr[...] = jnp.zeros(y_scr.shape, F32)
        for cp in weight_copies(0):
            cp.start()

    for cp in weight_copies(e):
        cp.wait()

    def to_bf16(i, c):
        rows = pl.ds(pl.multiple_of(i * MOE_CAST_ROWS, MOE_CAST_ROWS), MOE_CAST_ROWS)
        wgu_b[rows, :] = stage_gu[rows, :].astype(BF16)
        wd_b[rows, :] = stage_d[rows, :].astype(BF16)
        return c

    lax.fori_loop(0, D_MODEL // MOE_CAST_ROWS, to_bf16, 0, unroll=2)

    @pl.when(e + 1 < N_EXPERTS)
    def _():
        for cp in weight_copies(e + 1):
            cp.start()

    start = off_ref[e]
    end = off_ref[e + 1]
    ntiles = (end - start + tr - 1) // tr
    bgu = bgu_ref[...]
    bd = bd_ref[...]

    def gather_rows(base, slot, r0, n):
        for r in range(r0, r0 + n):
            src = jnp.bitwise_and(pair_ref[base + r], -TOP_K)
            x_scr[slot, pl.ds(r * XW, XW), :] = h2i_ref[pl.ds(pl.multiple_of(src, XW), XW), :]

    def scatter_rows(base, slot, r0, live):
        dsts, ps = [], []
        for rr in range(8):
            r = base + r0 + rr
            pair = pair_ref[jnp.maximum(r, 0)]
            dsts.append(pl.multiple_of(jnp.bitwise_and(pair, -TOP_K) * (AW // XW), AW))
            pw = lax.bitcast_convert_type(pb_ref[pair], F32)
            ps.append(jnp.where(jnp.logical_and(live, r < end), pw, 0.0))
        olds = [acc[pl.ds(dsts[rr], AW), :] for rr in range(8)]
        news = [olds[rr] + ps[rr] * y_scr[slot, pl.ds((r0 + rr) * AW, AW), :] for rr in range(8)]
        for rr in range(8):
            acc[pl.ds(dsts[rr], AW), :] = news[rr]

    @pl.when(ntiles > 0)
    def _():
        def first_gather(r8, c2):
            for rr in range(8):
                r = r8 * 8 + rr
                src = jnp.bitwise_and(pair_ref[start + r], -TOP_K)
                x_scr[0, pl.ds(pl.multiple_of(r * XW, XW), XW), :] = h2i_ref[pl.ds(pl.multiple_of(src, XW), XW), :]
            return c2

        lax.fori_loop(0, tr // 8, first_gather, 0)

    def tile_body(ti, carry):
        slot = jnp.bitwise_and(ti, 1)
        other = 1 - slot
        base = start + ti * tr
        parts = []
        for qd in range(XW):
            w = x_scr[slot, pl.ds(qd, tr, stride=XW), :]
            parts.append(lax.bitcast_convert_type(jnp.bitwise_and(w, jnp.uint32(0xFFFF0000)), F32))
            parts.append(lax.bitcast_convert_type(jnp.left_shift(w, jnp.uint32(16)), F32))
        x = jnp.concatenate(parts, axis=1).astype(BF16)
        for r0 in range(0, tr, 8):
            scatter_rows(base - tr, other, r0, ti > 0)
        gather_rows(base + tr, other, 0, tr)
        gu = jnp.dot(x, wgu_b[...], preferred_element_type=F32) + bgu
        g = jnp.minimum(gu[:, :D_EXPERT], SWIGLU_LIMIT)
        u = jnp.clip(gu[:, D_EXPERT:], -SWIGLU_LIMIT, SWIGLU_LIMIT)
        act = (u + 1.0) * g * _sigmoid(SWIGLU_ALPHA * g)
        y = jnp.dot(act.astype(BF16), wd_b[...], preferred_element_type=F32) + bd
        for qd in range(AW):
            y_scr[slot, pl.ds(qd, tr, stride=AW), :] = y[:, qd * LANES:(qd + 1) * LANES]
        return carry

    lax.fori_loop(0, ntiles, tile_body, 0)

    @pl.when(ntiles > 0)
    def _():
        last = ntiles - 1
        lslot = jnp.bitwise_and(last, 1)
        lbase = start + last * tr

        def last_scatter(r8, c2):
            dsts, ps = [], []
            for rr in range(8):
                r = lbase + r8 * 8 + rr
                pair = pair_ref[r]
                dsts.append(pl.multiple_of(jnp.bitwise_and(pair, -TOP_K) * (AW // XW), AW))
                pw = lax.bitcast_convert_type(pb_ref[pair], F32)
                ps.append(jnp.where(r < end, pw, 0.0))
            olds = [acc[pl.ds(dsts[rr], AW), :] for rr in range(8)]
            news = [olds[rr] + ps[rr] * y_scr[lslot, pl.ds(pl.multiple_of((r8 * 8 + rr) * AW, AW), AW), :]
                    for rr in range(8)]
            for rr in range(8):
                acc[pl.ds(dsts[rr], AW), :] = news[rr]
            return c2

        lax.fori_loop(0, tr // 8, last_scatter, 0)

    @pl.when(e == N_EXPERTS - 1)
    def _():
        cp = pltpu.make_async_copy(acc, out_ref, osem)
        cp.start()
        cp.wait()


def _moe_grouped(h2i, off, row_pair, pbits, w_gu, b_gu, w_down, b_down):
    t = h2i.shape[0] // XW
    assert D_EXPERT == D_MODEL
    grid_spec = pltpu.PrefetchScalarGridSpec(
        num_scalar_prefetch=3,
        grid=(N_EXPERTS,),
        in_specs=[pl.BlockSpec((t * XW, LANES), lambda e, *_: (0, 0), pipeline_mode=pl.Buffered(1)),
                  pl.BlockSpec(memory_space=pl.ANY),
                  pl.BlockSpec(memory_space=pl.ANY),
                  pl.BlockSpec((None, 1, 2 * D_EXPERT), lambda e, *_: (e, 0, 0)),
                  pl.BlockSpec((None, 1, D_MODEL), lambda e, *_: (e, 0, 0))],
        out_specs=pl.BlockSpec(memory_space=pl.ANY),
        scratch_shapes=[pltpu.VMEM((t * AW, LANES), F32),
                        pltpu.VMEM((2, MOE_TILE * XW, LANES), U32),
                        pltpu.VMEM((2, MOE_TILE * AW, LANES), F32),
                        pltpu.VMEM((D_MODEL, 2 * D_EXPERT), F32),
                        pltpu.VMEM((D_EXPERT, D_MODEL), F32),
                        pltpu.VMEM((D_MODEL, 2 * D_EXPERT), BF16),
                        pltpu.VMEM((D_EXPERT, D_MODEL), BF16),
                        pltpu.SemaphoreType.DMA((2,)),
                        pltpu.SemaphoreType.DMA(())])
    return pl.pallas_call(
        _moe_kernel,
        grid_spec=grid_spec,
        out_shape=jax.ShapeDtypeStruct((t * AW, LANES), F32),
        compiler_params=_cparams(1, VMEM_LIMIT),
        name="moe_grouped",
    )(off, row_pair, pbits, h2i, w_gu, w_down, b_gu.reshape(N_EXPERTS, 1, 2 * D_EXPERT),
      b_down.reshape(N_EXPERTS, 1, D_MODEL))


def _residual_kernel(x1_ref, moe_ref, g2_ref, nw_ref, oc_ref, ol_ref, *, normalize, ctx_tiles):
    tm = x1_ref.shape[0]
    moe = jnp.concatenate([moe_ref[pl.ds(qd, tm, stride=AW), :] for qd in range(AW)], axis=1)
    x2 = x1_ref[...] + g2_ref[...] * moe
    if normalize:
        ms = jnp.mean(x2 * x2, axis=-1, keepdims=True)
        x2 = (x2 * lax.rsqrt(ms + EPS)) * nw_ref[...]

    @pl.when(pl.program_id(0) < ctx_tiles)
    def _():
        oc_ref[...] = x2

    @pl.when(pl.program_id(0) >= ctx_tiles)
    def _():
        ol_ref[...] = x2


def _residual(x1, moe_i, modr, norm_w, *, normalize, t_ctx):
    t = x1.shape[0]
    tm = 512
    upt = tm // UNIT
    ctx_tiles = t_ctx // tm
    return pl.pallas_call(
        functools.partial(_residual_kernel, normalize=normalize, ctx_tiles=ctx_tiles),
        grid=(t // tm,),
        in_specs=[pl.BlockSpec((tm, D_MODEL), lambda m: (m, 0)),
                  pl.BlockSpec((tm * AW, LANES), lambda m: (m, 0)),
                  pl.BlockSpec((None, None, 1, D_MODEL), lambda m: (m * upt, 5, 0, 0)),
                  pl.BlockSpec((1, D_MODEL), lambda m: (0, 0))],
        out_specs=_two_source_specs(tm, D_MODEL, ctx_tiles, 1),
        out_shape=[jax.ShapeDtypeStruct((t_ctx, D_MODEL), F32),
                   jax.ShapeDtypeStruct((t - t_ctx, D_MODEL), F32)],
        compiler_params=_cparams(1),
        name="ffn_residual_norm",
    )(x1, moe_i, modr, norm_w.reshape(1, D_MODEL))


def _route_kernel(pos_ref, pair_ref, *, n_pairs):
    def pad(i, c):
        pair_ref[n_pairs + i] = 0
        return c

    lax.fori_loop(0, ROUTE_PAD, pad, 0)

    def place(i8, c):
        for ii in range(8):
            i = i8 * 8 + ii
            pair_ref[pos_ref[i]] = i
        return c

    lax.fori_loop(0, n_pairs // 8, place, 0)


def _route(er_flat, cnt):
    n = er_flat.shape[0]
    off_end = jnp.cumsum(cnt)
    off = off_end - cnt
    expert = jnp.bitwise_and(er_flat, N_EXPERTS - 1)
    onehot = expert[:, None] == jnp.arange(N_EXPERTS, dtype=jnp.int32)[None, :]
    pos = lax.shift_right_logical(er_flat, RANK_SHIFT) + jnp.sum(jnp.where(onehot, off[None, :], 0), axis=1)
    smem = pl.BlockSpec(memory_space=pltpu.SMEM)
    row_pair = pl.pallas_call(
        functools.partial(_route_kernel, n_pairs=n),
        in_specs=[smem],
        out_specs=smem,
        out_shape=jax.ShapeDtypeStruct((n + ROUTE_PAD,), jnp.int32),
        name="moe_route",
    )(pos)
    return jnp.concatenate([off, jnp.broadcast_to(off_end[-1:], (8,))]), row_pair


def kernel(x_prompt, x_sample, state_ssm, c, c_ctx, w_ada, b_ada, norm_mix, norm_ffn, w_in, ssm_conv_w, ssm_conv_b, dt_bias, a_log, d_skip, ssm_norm_w, w_ssd_out, conf_dw_w, conf_dw_b, conf_ln_w, conf_ln_b, w_conf_out, b_conf_out, b_gate, w_o, w_router, b_router, w_gu, b_gu, w_down, b_down, norm_final):
    n_ctx, seq_ctx, _ = x_prompt.shape
    n_lat, seq_lat, _ = x_sample.shape
    depth = w_in.shape[0]
    t_ctx, t_lat = n_ctx * seq_ctx, n_lat * seq_lat
    t_all = t_ctx + t_lat
    assert seq_ctx == UNIT and seq_lat % 1024 == 0 and t_ctx % 1024 == 0
    assert n_lat + 1 <= 8 and seq_lat % GRID_W == 0 and UNIT % GRID_W == 0

    x_ctx, x_lat = x_prompt.reshape(t_ctx, D_MODEL), x_sample.reshape(t_lat, D_MODEL)
    cvec =jnp.concatenate([c_ctx[None, :], c, jnp.zeros((8 - 1 - n_lat, D_MODEL), F32)], axis=0)
    unit_row = jnp.concatenate([jnp.zeros((t_ctx // UNIT,), jnp.int32),
                                1 + jnp.arange(t_lat // UNIT, dtype=jnp.int32) // (seq_lat // UNIT)])
    assert XW == TOP_K and N_EXPERTS == 1 << RANK_SHIFT

    new_states = []
    for l in range(depth):
        mod = _ada_mod(cvec, w_ada[l], b_ada[l])
        modr = mod[unit_row].reshape(t_all // UNIT, 6, 1, D_MODEL)

        w_a, w_b, w_dt = _in_proj_weights(w_in[l])
        proj, dt_raw = _in_proj(x_ctx, x_lat, norm_mix[l], modr, w_a, w_b, w_dt)

        pad_h = lambda v: jnp.pad(v.reshape(2, 1, N_HEADS), ((0, 0), (0, 0), (0, LANES - N_HEADS)))
        dtb, alog = pad_h(dt_bias[l]), pad_h(a_log[l])
        dskip = jnp.repeat(d_skip[l], HEAD_DIM).reshape(1, D_INNER)
        nw = ssm_norm_w[l].reshape(1, D_INNER)
        h0t = state_ssm[:, l].astype(F32).reshape(n_lat, 2, D_INNER, D_STATE)

        xbc_ctx = _ssm_conv(proj, ssm_conv_w[l], ssm_conv_b[l], seq=seq_ctx, nseq=n_ctx, row0=0)
        xbc_lat = _ssm_conv(proj, ssm_conv_w[l], ssm_conv_b[l], seq=seq_lat, nseq=n_lat, row0=t_ctx)
        y_ctx, st_ctx = _ssd(xbc_ctx, proj, dt_raw, dtb, alog, dskip, nw, None,
                             seq=seq_ctx, nseq=n_ctx, row0=0, write_state=True)
        (y_lat,) = _ssd(xbc_lat, proj, dt_raw, dtb, alog, dskip, nw, h0t,
                        seq=seq_lat, nseq=n_lat, row0=t_ctx, write_state=False)
        u_ctx = _conformer(proj, conf_dw_w[l], conf_dw_b[l], conf_ln_w[l], conf_ln_b[l],
                           seg=seq_ctx, ntok=t_ctx, row0=0)
        u_lat = _conformer(proj, conf_dw_w[l], conf_dw_b[l], conf_ln_w[l], conf_ln_b[l],
                           seg=GRID_W, ntok=t_lat, row0=t_ctx)
        wr = jnp.pad(w_router[l], ((0, 0), (0, LANES - N_EXPERTS)))
        wr_hi = wr.astype(BF16)
        wr_lo = (wr - wr_hi.astype(F32)).astype(BF16)
        br = jnp.pad(b_router[l], (0, LANES - N_EXPERTS)).reshape(1, LANES)
        x1, h2i, er, topp, cnt = _mix_out(
            x_ctx, x_lat, y_ctx, y_lat, u_ctx, u_lat, proj, modr, w_ssd_out[l].astype(BF16),
            w_conf_out[l].astype(BF16), b_conf_out[l], b_gate[l], w_o[l].astype(BF16), norm_ffn[l],
            jnp.concatenate([wr_hi, wr_lo, wr_hi], axis=0), br)

        off, row_pair = _route(er[:, :TOP_K].reshape(-1), cnt[0, :N_EXPERTS])
        pbits = lax.bitcast_convert_type(topp[:, :TOP_K], jnp.int32).reshape(-1)
        moe_i = _moe_grouped(h2i, off, row_pair, pbits, w_gu[l], b_gu[l], w_down[l], b_down[l])
        last = l + 1 == depth
        x_ctx, x_lat = _residual(x1, moe_i, modr, norm_final if last else norm_ffn[l], normalize=last, t_ctx=t_ctx)
        new_states.append(st_ctx.reshape(n_ctx, 1, 2, N_HEADS, HEAD_DIM, D_STATE))

    y_prompt = x_ctx.reshape(n_ctx, seq_ctx, D_MODEL)
    y_sample = x_lat.reshape(n_lat, seq_lat, D_MODEL)
    new_state_ssm = new_states[0] if depth == 1 else jnp.concatenate(new_states, axis=1)
    return (y_prompt, y_sample, new_state_ssm)
```

```python
import functools

import jax
import jax.numpy as jnp
from jax import lax
from jax.experimental import pallas as pl
from jax.experimental.pallas import tpu as pltpu

F32 = jnp.float32
BF16 = jnp.bfloat16
U32 = jnp.uint32

D_MODEL = 1024
GRID_W = 64
D_INNER = 2 * D_MODEL
HEAD_DIM = 64
N_HEADS = D_INNER // HEAD_DIM
N_GROUPS = 8
HEADS_PER_GROUP = N_HEADS // N_GROUPS
D_STATE = 128
SSM_CONV = 5
CHUNK = 128
D_XBC = D_INNER + 2 * N_GROUPS * D_STATE
D_CONF = D_MODEL
CONF_KERNEL = 31
N_EXPERTS = 32
TOP_K = 4
D_EXPERT = D_MODEL
SWIGLU_LIMIT = 7.0
SWIGLU_ALPHA = 1.702
EPS = 1e-6

LANES = 128
UNIT = 256
GROUP_W = D_INNER // N_GROUPS
SSD_CHUNKS = 2
SSD_GROUP_UNROLL = 2
MIX_CHAINS = 2
MOE_TILE = 128
ROUTE_PAD = 2 * MOE_TILE
MOE_CAST_ROWS = 16
XW = D_MODEL // (2 * LANES)
AW = D_MODEL // LANES
RANK_SHIFT = 5
VMEM_LIMIT = 62 * 1024 * 1024

COL_Z = 0
COL_XBC = D_INNER
COL_GLU = D_XBC + D_INNER
COL_GATE = D_XBC + D_INNER + 2 * D_CONF
N_MAIN = D_XBC + D_INNER + 2 * D_CONF + 2 * D_MODEL


def _sigmoid(x):
    return 0.5 * jnp.tanh(0.5 * x) + 0.5


def _silu(x):
    return x * _sigmoid(x)


def _cparams(n_axes, vmem=None):
    return pltpu.CompilerParams(
        dimension_semantics=("arbitrary",) * n_axes,
        vmem_limit_bytes=vmem)


def _ada_kernel(c_ref, w_ref, b_ref, o_ref):
    s = _silu(c_ref[...])
    o_ref[...] = jnp.dot(s.astype(BF16), w_ref[...].astype(BF16),
                         preferred_element_type=F32) + b_ref[...]


def _ada_mod(cvec, w_ada, b_ada):
    n = w_ada.shape[1]
    tn = 1536
    return pl.pallas_call(
        _ada_kernel,
        grid=(n // tn,),
        in_specs=[pl.BlockSpec((8, D_MODEL), lambda j: (0, 0)),
                  pl.BlockSpec((D_MODEL, tn), lambda j: (0, j)),
                  pl.BlockSpec((1, tn), lambda j: (0, j))],
        out_specs=pl.BlockSpec((8, tn), lambda j: (0, j)),
        out_shape=jax.ShapeDtypeStruct((8, n), F32),
        compiler_params=_cparams(1, 40 * 1024 * 1024),
        name="ada_mod",
    )(cvec, w_ada, b_ada.reshape(1, n))


def _wprep_kernel(w_ref, wn_ref, oa_ref, ob_ref, odt_ref, *, a_tiles):
    n = pl.program_id(0)
    tn = w_ref.shape[0]
    shift = 2 * N_HEADS

    def put(o_ref, c, rows_t):
        for j in range(D_MODEL // LANES):
            o_ref[j * LANES:(j + 1) * LANES, c * LANES:(c + 1) * LANES] = \
                rows_t[:, j * LANES:(j + 1) * LANES].T.astype(BF16)

    @pl.when(n < a_tiles)
    def _():
        for c in range(tn // LANES):
            put(oa_ref, c, w_ref[c * LANES:(c + 1) * LANES, :])

    @pl.when(n >= a_tiles)
    def _():
        for c in range(tn // LANES):
            lo = w_ref[c * LANES + shift:(c + 1) * LANES, :]
            hi = w_ref[(c + 1) * LANES:(c + 1) * LANES + shift, :] if c + 1 < tn // LANES else wn_ref[0:shift, :]
            put(ob_ref, c, jnp.concatenate([lo, hi], axis=0))

    @pl.when(n == a_tiles)
    def _():
        zeros = jnp.zeros((LANES - N_HEADS, D_MODEL), F32)
        for d in range(2):
            put(odt_ref, d, jnp.concatenate([w_ref[d * N_HEADS:(d + 1) * N_HEADS, :], zeros], axis=0))


def _in_proj_weights(w_in):
    tn = 2048
    n_a = D_INNER + D_XBC
    n_b = 2 * D_CONF + 2 * D_MODEL
    a_tiles = n_a // tn
    assert n_a % tn == 0 and n_b % tn == 0 and 2 * N_HEADS < LANES
    n_tiles = a_tiles + n_b // tn
    assert w_in.shape[1] == n_tiles * tn + 2 * N_HEADS
    w_t = w_in.T
    return pl.pallas_call(
        functools.partial(_wprep_kernel, a_tiles=a_tiles),
        grid=(n_tiles,),
        in_specs=[pl.BlockSpec((tn, D_MODEL), lambda n: (n, 0)),
                  pl.BlockSpec((LANES, D_MODEL), lambda n: ((n + 1) * (tn // LANES), 0))],
        out_specs=[pl.BlockSpec((D_MODEL, tn), lambda n: (0, jnp.minimum(n, a_tiles - 1))),
                   pl.BlockSpec((D_MODEL, tn), lambda n: (0, jnp.maximum(n - a_tiles, 0))),
                   pl.BlockSpec((D_MODEL, 2 * LANES), lambda n: (0, 0))],
        out_shape=[jax.ShapeDtypeStruct((D_MODEL, n_a), BF16),
                   jax.ShapeDtypeStruct((D_MODEL, n_b), BF16),
                   jax.ShapeDtypeStruct((D_MODEL, 2 * LANES), BF16)],
        compiler_params=_cparams(1, 48 * 1024 * 1024),
        name="in_proj_weights",
    )(w_t, w_t)


def _inproj_kernel(xc_ref, xl_ref, nw_ref, sc_ref, sh_ref, wa_ref, wb_ref, wdt_ref, o_ref, dt_ref, h_scr, *,
                   ctx_tiles, a_tiles):
    @pl.when(pl.program_id(1) == 0)
    def _():
        xf = jnp.where(pl.program_id(0) < ctx_tiles, xc_ref[...], xl_ref[...])
        ms = jnp.mean(xf * xf, axis=-1, keepdims=True)
        hn = (xf * lax.rsqrt(ms + EPS)) * nw_ref[...]
        hn = hn * (1.0 + sc_ref[...]) + sh_ref[...]
        hb = hn.astype(BF16)
        h_scr[...] = hb
        dt_ref[...] = jnp.dot(hb, wdt_ref[...], preferred_element_type=F32)

    @pl.when(pl.program_id(1) < a_tiles)
    def _():
        o_ref[...] = jnp.dot(h_scr[...], wa_ref[...], preferred_element_type=F32).astype(BF16)

    @pl.when(pl.program_id(1) >= a_tiles)
    def _():
        o_ref[...] = jnp.dot(h_scr[...], wb_ref[...], preferred_element_type=F32).astype(BF16)


def _two_source_specs(tm, width, ctx_tiles, nargs):
    if nargs == 1:
        return [pl.BlockSpec((tm, width), lambda m: (jnp.minimum(m, ctx_tiles - 1), 0)),
                pl.BlockSpec((tm, width), lambda m: (jnp.maximum(m - ctx_tiles, 0), 0))]
    return [pl.BlockSpec((tm, width), lambda m, n: (jnp.minimum(m, ctx_tiles - 1), 0)),
            pl.BlockSpec((tm, width), lambda m, n: (jnp.maximum(m - ctx_tiles, 0), 0))]


def _in_proj(x_ctx, x_lat, norm_w, modr, w_a, w_b, w_dt):
    t_ctx = x_ctx.shape[0]
    t = t_ctx + x_lat.shape[0]
    tm, tn = 1024, 2048
    upt = tm // UNIT
    a_tiles = w_a.shape[1] // tn
    return pl.pallas_call(
        functools.partial(_inproj_kernel, ctx_tiles=t_ctx // tm, a_tiles=a_tiles),
        grid=(t // tm, N_MAIN // tn),
        in_specs=_two_source_specs(tm, D_MODEL, t_ctx // tm, 2) + [
                  pl.BlockSpec((1, D_MODEL), lambda m, n: (0, 0)),
                  pl.BlockSpec((None, None, 1, D_MODEL), lambda m, n: (m * upt, 1, 0, 0)),
                  pl.BlockSpec((None, None, 1, D_MODEL), lambda m, n: (m * upt, 0, 0, 0)),
                  pl.BlockSpec((D_MODEL, tn), lambda m, n: (0, jnp.minimum(n, a_tiles - 1))),
                  pl.BlockSpec((D_MODEL, tn), lambda m, n: (0, jnp.maximum(n - a_tiles, 0))),
                  pl.BlockSpec((D_MODEL, 2 * LANES), lambda m, n: (0, 0))],
        out_specs=[pl.BlockSpec((tm, tn), lambda m, n: (m, n)),
                   pl.BlockSpec((tm, 2 * LANES), lambda m, n: (m, 0))],
        out_shape=[jax.ShapeDtypeStruct((t, N_MAIN), BF16),
                   jax.ShapeDtypeStruct((t, 2 * LANES), F32)],
        scratch_shapes=[pltpu.VMEM((tm, D_MODEL), BF16)],
        compiler_params=_cparams(2, 48 * 1024 * 1024),
        name="in_proj",
    )(x_ctx, x_lat, norm_w.reshape(1, D_MODEL), modr, modr, w_a, w_b, w_dt)


def _ssm_conv_kernel(x_ref, w_ref, b_ref, o_ref, pad_scr, *, seq):
    cb = x_ref.shape[1]
    pad_scr[0:8, :] = jnp.zeros((8, cb), F32)
    pad_scr[8 + seq:16 + seq, :] = jnp.zeros((8, cb), F32)
    pad_scr[8:8 + seq, :] = x_ref[...].astype(F32)
    half = SSM_CONV // 2
    rows, cw = 256, 512
    for c0 in range(0, cb, cw):
        cs = slice(c0, c0 + cw)
        for r0 in range(0, seq, rows):
            acc = jnp.broadcast_to(b_ref[:, cs], (rows, cw))
            for k in range(SSM_CONV):
                acc = acc + w_ref[k:k + 1, cs] * pad_scr[8 + r0 + k - half:8 + r0 + k - half + rows, cs]
            o_ref[r0:r0 + rows, cs] = _silu(acc).astype(BF16)


def _ssm_conv(proj, conv_w, conv_b, *, seq, nseq, row0):
    cb = 2048
    blk0 = row0 // seq
    return pl.pallas_call(
        functools.partial(_ssm_conv_kernel, seq=seq),
        grid=(nseq, D_XBC // cb),
        in_specs=[pl.BlockSpec((seq, cb), lambda s, j: (blk0 + s, COL_XBC // cb + j)),
                  pl.BlockSpec((SSM_CONV, cb), lambda s, j: (0, j)),
                  pl.BlockSpec((1, cb), lambda s, j: (0, j))],
        out_specs=pl.BlockSpec((seq, cb), lambda s, j: (s, j)),
        out_shape=jax.ShapeDtypeStruct((nseq * seq, D_XBC), BF16),
        scratch_shapes=[pltpu.VMEM((seq + 16, cb), F32)],
        compiler_params=_cparams(2, 40 * 1024 * 1024),
        name="ssm_conv",
    )(proj, conv_w, conv_b.reshape(1, D_XBC))


def _split_bf16(v):
    hi = v.astype(BF16)
    lo = (v - hi.astype(F32)).astype(BF16)
    return jnp.concatenate([hi, lo], axis=1)


def _head_select_matrices():
    j = jnp.arange(2 * LANES, dtype=jnp.int32)[:, None] % LANES
    full = (j == (jnp.arange(N_HEADS * LANES, dtype=jnp.int32)[None, :] // LANES)).astype(BF16)
    exp = (j == (jnp.arange(D_INNER, dtype=jnp.int32)[None, :] // HEAD_DIM)).astype(BF16)
    return full, exp


def _ssd_kernel(*refs, nc, has_h0, write_state):
    (xs_ref, b_ref, c_ref, z_ref, dtr_ref, dtb_ref, alog_ref, dskip_ref, nw_ref, self_ref, sele_ref), rest = \
        refs[:11], refs[11:]
    if has_h0:
        h0_ref, rest = rest[0], rest[1:]
    y_ref, rest = rest[0], rest[1:]
    if write_state:
        st_ref, rest = rest[0], rest[1:]
    h_scr, ybuf, ychunk, colb_scr, wexp_scr, eexp_scr, texp_scr, rowq_scr = rest

    q = CHUNK
    phase = pl.program_id(1)
    c = pl.program_id(2)
    is_fwd = phase == 1
    c_eff = jnp.where(is_fwd, c, nc - 1 - c)

    @pl.when(c == 0)
    def _():
        if has_h0:
            for j in range(D_INNER // LANES):
                h_scr[:, j * LANES:(j + 1) * LANES] = h0_ref[j * LANES:(j + 1) * LANES, :].T
        else:
            h_scr[...] = jnp.zeros(h_scr.shape, F32)

    nrow = SSD_CHUNKS * q
    xdt = dtr_ref[...] + dtb_ref[...]
    dt = jnp.maximum(xdt, 0.0) + jnp.log(1.0 + jnp.exp(-jnp.abs(xdt)))
    a = -jnp.exp(alog_ref[...])
    adt = dt * a
    sgn = jnp.where(is_fwd, 1, -1)
    row = lax.broadcasted_iota(jnp.int32, (nrow, nrow), 0)
    col = lax.broadcasted_iota(jnp.int32, (nrow, nrow), 1)
    tri_blk = jnp.logical_and(row // q == col // q, (col - row) * sgn <= 0)
    cum2 = jnp.dot(tri_blk.astype(BF16), _split_bf16(adt), preferred_element_type=F32)
    cum = cum2[:, :LANES] + cum2[:, LANES:]
    tots = [jnp.where(is_fwd, cum[k * q + q - 1:k * q + q, :], cum[k * q:k * q + 1, :]) for k in range(SSD_CHUNKS)]
    tot_rows = jnp.concatenate([jnp.broadcast_to(t, (q, LANES)) for t in tots], axis=0)
    rowq = (cum - jnp.log(dt)).T
    for k in range(SSD_CHUNKS):
        rowq_scr[k] = rowq[:, k * q:(k + 1) * q]
    wdec = dt * jnp.exp(tot_rows - cum)
    eo = jnp.exp(cum)
    etot = jnp.concatenate([jnp.broadcast_to(jnp.exp(t), (8, LANES)) for t in tots], axis=0)
    colb_scr[...] = jnp.dot(_split_bf16(cum), self_ref[...], preferred_element_type=F32)
    expanded = jnp.dot(jnp.concatenate([_split_bf16(wdec), _split_bf16(eo), _split_bf16(etot)], axis=0),
                       sele_ref[...], preferred_element_type=F32)
    wexp_scr[...] = expanded[0:nrow]
    eexp_scr[...] = expanded[nrow:2 * nrow]
    for k in range(SSD_CHUNKS):
        texp_scr[k] = expanded[2 * nrow + 8 * k:2 * nrow + 8 * (k + 1)]
    lrow = lax.broadcasted_iota(jnp.int32, (q, q), 0)
    lcol = lax.broadcasted_iota(jnp.int32, (q, q), 1)
    tri = (lcol - lrow) * sgn <= 0
    lane_g = lax.broadcasted_iota(jnp.int32, (1, GROUP_W), 1)
    head_mask = [(lane_g // HEAD_DIM == hh).astype(BF16) for hh in range(HEADS_PER_GROUP)]
    neg_inf = jnp.float32(-jnp.inf)

    def group_body(g, carry):
        gs = pl.ds(pl.multiple_of(g * GROUP_W, GROUP_W), GROUP_W)
        ns = pl.ds(pl.multiple_of(g * D_STATE, D_STATE), D_STATE)
        for k in range(SSD_CHUNKS):
            ci = jnp.where(is_fwd, k, SSD_CHUNKS - 1 - k)
            r0 = pl.multiple_of(ci * q, q)
            rs = pl.ds(r0, q)
            bg = b_ref[rs, ns]
            cg = c_ref[rs, ns]
            scores = lax.dot_general(cg, bg, (((1,), (1,)), ((), ())), preferred_element_type=F32)
            xs_g = xs_ref[rs, gs]
            ms, xb = [], []
            for hh in range(HEADS_PER_GROUP):
                h = HEADS_PER_GROUP * g + hh
                colb = colb_scr[rs, pl.ds(pl.multiple_of(h * LANES, LANES), LANES)]
                seg = jnp.where(tri, colb - rowq_scr[ci, pl.ds(h, 1), :], neg_inf)
                ms.append((scores * jnp.exp(seg)).astype(BF16))
                xb.append(xs_g * head_mask[hh])
            y_diag = jnp.dot(jnp.concatenate(ms, axis=1), jnp.concatenate(xb, axis=0),
                             preferred_element_type=F32)
            xd = (xs_g.astype(F32) * wexp_scr[rs, gs]).astype(BF16)
            bt = bg.astype(F32).T.astype(BF16)
            st = jnp.dot(bt, xd, preferred_element_type=F32)
            hg = h_scr[:, gs]
            y_off = jnp.dot(cg, hg.astype(BF16), preferred_element_type=F32) * eexp_scr[rs, gs]
            ychunk[rs, gs] = y_diag + y_off
            h_scr[:, gs] = hg * texp_scr[ci, 0:1, gs] + st
        return carry

    lax.fori_loop(0, N_GROUPS, group_body, 0, unroll=SSD_GROUP_UNROLL)

    rows = pl.ds(pl.multiple_of(c_eff * nrow, nrow), nrow)

    @pl.when(jnp.logical_not(is_fwd))
    def _():
        ybuf[rows, :] = ychunk[...]

    @pl.when(is_fwd)
    def _():
        zf = z_ref[...].astype(F32)
        yt = ychunk[...] + ybuf[rows, :] + xs_ref[...].astype(F32) * dskip_ref[...]
        yz = yt * _silu(zf)
        for g in range(N_GROUPS):
            gs = slice(g * GROUP_W, (g + 1) * GROUP_W)
            blk = yz[:, gs]
            ms = jnp.mean(blk * blk, axis=-1, keepdims=True)
            y_ref[:, gs] = (blk * lax.rsqrt(ms + EPS) * nw_ref[:, gs]).astype(BF16)

    if write_state:
        @pl.when(c == nc - 1)
        def _():
            for j in range(D_INNER // LANES):
                st_ref[j * LANES:(j + 1) * LANES, :] = h_scr[:, j * LANES:(j + 1) * LANES].T


def _ssd(xbc_c, proj, dt_raw, dt_bias, a_log, d_skip, norm_w, h0t, *, seq, nseq, row0, write_state):
    rb = SSD_CHUNKS * CHUNK
    nc = seq // rb
    blk0 = row0 // rb
    has_h0 = h0t is not None
    assert seq % rb == 0 and row0 % rb == 0

    def tok(s, p, c):
        return s * nc + p * c + (1 - p) * (nc - 1 - c)

    in_specs = [
        pl.BlockSpec((rb, D_INNER), lambda s, p, c: (tok(s, p, c), 0)),
        pl.BlockSpec((rb, N_GROUPS * D_STATE), lambda s, p, c: (tok(s, p, c), 2)),
        pl.BlockSpec((rb, N_GROUPS * D_STATE), lambda s, p, c: (tok(s, p, c), 3)),
        pl.BlockSpec((rb, D_INNER), lambda s, p, c: (blk0 + s * nc + p * c, COL_Z // D_INNER)),
        pl.BlockSpec((rb, LANES), lambda s, p, c: (blk0 + tok(s, p, c), 1 - p)),
        pl.BlockSpec((None, 1, LANES), lambda s, p, c: (1 - p, 0, 0)),
        pl.BlockSpec((None, 1, LANES), lambda s, p, c: (1 - p, 0, 0)),
        pl.BlockSpec((1, D_INNER), lambda s, p, c: (0, 0)),
        pl.BlockSpec((1, D_INNER), lambda s, p, c: (0, 0)),
        pl.BlockSpec((2 * LANES, N_HEADS * LANES), lambda s, p, c: (0, 0)),
        pl.BlockSpec((2 * LANES, D_INNER), lambda s, p, c: (0, 0)),
    ]
    sel_full, sel_exp = _head_select_matrices()
    args = [xbc_c, xbc_c, xbc_c, proj, dt_raw, dt_bias, a_log, d_skip, norm_w, sel_full, sel_exp]
    if has_h0:
        in_specs.append(pl.BlockSpec((None, None, D_INNER, D_STATE), lambda s, p, c: (s, 1 - p, 0, 0)))
        args.append(h0t)
    out_specs = [pl.BlockSpec((rb, D_INNER), lambda s, p, c: (s * nc + p * c, 0))]
    out_shape = [jax.ShapeDtypeStruct((nseq * seq, D_INNER), BF16)]
    if write_state:
        out_specs.append(pl.BlockSpec((None, None, D_INNER, D_STATE), lambda s, p, c: (s, 1 - p, 0, 0)))
        out_shape.append(jax.ShapeDtypeStruct((nseq, 2, D_INNER, D_STATE), F32))
    return pl.pallas_call(
        functools.partial(_ssd_kernel, nc=nc, has_h0=has_h0, write_state=write_state),
        grid=(nseq, 2, nc),
        in_specs=in_specs,
        out_specs=out_specs,
        out_shape=out_shape,
        scratch_shapes=[pltpu.VMEM((D_STATE, D_INNER), F32),
                        pltpu.VMEM((seq, D_INNER), F32),
                        pltpu.VMEM((rb, D_INNER), F32),
                        pltpu.VMEM((rb, N_HEADS * LANES), F32),
                        pltpu.VMEM((rb, D_INNER), F32),
                        pltpu.VMEM((rb, D_INNER), F32),
                        pltpu.VMEM((SSD_CHUNKS, 8, D_INNER), F32),
                        pltpu.VMEM((SSD_CHUNKS, LANES, CHUNK), F32)],
        compiler_params=_cparams(3, 48 * 1024 * 1024),
        name="ssd_scan",
    )(*args)


def _conf_kernel(glu_ref, w_ref, b_ref, lnw_ref, lnb_ref, o_ref, pad_scr, sh_scr, conv_scr, *, seg):
    rows = glu_ref.shape[0]
    nseg = rows // seg
    half = CONF_KERNEL // 2
    front = 16
    span = seg + 24
    a = glu_ref[:, :D_CONF].astype(F32)
    b = glu_ref[:, D_CONF:].astype(F32)
    u = a * _sigmoid(b)
    for i in range(nseg):
        pad_scr[i, 0:front, :] = jnp.zeros((front, D_CONF), F32)
        pad_scr[i, front + seg:front + seg + 16, :] = jnp.zeros((16, D_CONF), F32)
        pad_scr[i, front:front + seg, :] = u[i * seg:(i + 1) * seg, :]
    for s in range(8):
        for i in range(nseg):
            for cbi in range(D_CONF // LANES):
                cs = slice(cbi * LANES, (cbi + 1) * LANES)
                for r0 in range(0, span, 56):
                    n = min(56, span - r0)
                    sh_scr[s, i, r0:r0 + n, cs] = pad_scr[i, s + r0:s + r0 + n, cs]
    rb = 64
    for i in range(nseg):
        for cbi in range(D_CONF // LANES):
            cs = slice(cbi * LANES, (cbi + 1) * LANES)
            for r0 in range(0, seg, rb):
                acc = jnp.broadcast_to(b_ref[:, cs], (rb, LANES))
                for k in range(CONF_KERNEL):
                    start = front + r0 + k - half
                    al = start - start % 8
                    acc = acc + w_ref[k:k + 1, cs] * sh_scr[start % 8, i, al:al + rb, cs]
                conv_scr[i * seg + r0:i * seg + r0 + rb, cs] = acc
    v = conv_scr[...]
    mu = jnp.mean(v, axis=-1, keepdims=True)
    vc = v - mu
    var = jnp.mean(vc * vc, axis=-1, keepdims=True)
    ln = (vc * lax.rsqrt(var + EPS)) * lnw_ref[...] + lnb_ref[...]
    o_ref[...] = _silu(ln).astype(BF16)


def _conformer(proj, dw_w, dw_b, ln_w, ln_b, *, seg, ntok, row0):
    rows = UNIT
    blk0 = row0 // rows
    return pl.pallas_call(
        functools.partial(_conf_kernel, seg=seg),
        grid=(ntok // rows,),
        in_specs=[pl.BlockSpec((rows, 2 * D_CONF), lambda i: (blk0 + i, COL_GLU // (2 * D_CONF))),
                  pl.BlockSpec((CONF_KERNEL, D_CONF), lambda i: (0, 0)),
                  pl.BlockSpec((1, D_CONF), lambda i: (0, 0)),
                  pl.BlockSpec((1, D_CONF), lambda i: (0, 0)),
                  pl.BlockSpec((1, D_CONF), lambda i: (0, 0))],
        out_specs=pl.BlockSpec((rows, D_CONF), lambda i: (i, 0)),
        out_shape=jax.ShapeDtypeStruct((ntok, D_CONF), BF16),
        scratch_shapes=[pltpu.VMEM((rows // seg, seg + 32, D_CONF), F32),
                        pltpu.VMEM((8, rows // seg, seg + 24, D_CONF), F32),
                        pltpu.VMEM((rows, D_CONF), F32)],
        compiler_params=_cparams(1, 40 * 1024 * 1024),
        name="conformer_conv",
    )(proj, dw_w, dw_b.reshape(1, D_CONF), ln_w.reshape(1, D_CONF), ln_b.reshape(1, D_CONF))


def _pack_halves(x):
    outs = []
    for cb in range(x.shape[1] // (2 * LANES)):
        hi = x[:, cb * 2 * LANES:cb * 2 * LANES + LANES].astype(BF16).astype(F32)
        lo = x[:, cb * 2 * LANES + LANES:(cb + 1) * 2 * LANES].astype(BF16).astype(F32)
        hw = lax.bitcast_convert_type(hi, U32)
        lw = jnp.right_shift(lax.bitcast_convert_type(lo, U32), jnp.uint32(16))
        outs.append(jnp.bitwise_or(hw, lw))
    return jnp.concatenate(outs, axis=1)


def _mix_kernel(xc_ref, xl_ref, yc_ref, yl_ref, uc_ref, ul_ref, gate_ref, g1_ref, sc2_ref, sh2_ref,
                wssd_ref, wconf_ref, bconf_ref, bgate_ref, wo_ref, nffn_ref, wr_ref, br_ref,
                x1_ref, h2i_ref, er_ref, topp_ref, cnt_ref, cnt_scr, *, ctx_tiles):
    @pl.when(pl.program_id(0) == 0)
    def _():
        cnt_scr[...] = jnp.zeros(cnt_scr.shape, F32)

    is_ctx = pl.program_id(0) < ctx_tiles
    rows = x1_ref.shape[0]
    sub = rows // MIX_CHAINS
    lane = lax.broadcasted_iota(jnp.int32, (sub, LANES), 1)
    lane_f = lane.astype(F32)
    neg_inf = jnp.float32(-jnp.inf)

    def chain(c):
        rs = slice(c * sub, (c + 1) * sub)
        y_in = jnp.where(is_ctx, yc_ref[rs, :], yl_ref[rs, :])
        u_in = jnp.where(is_ctx, uc_ref[rs, :], ul_ref[rs, :])
        o_ssd = jnp.dot(y_in, wssd_ref[...], preferred_element_type=F32)
        o_conf = jnp.dot(u_in, wconf_ref[...], preferred_element_type=F32) + bconf_ref[...]
        gates = _sigmoid(gate_ref[rs, :].astype(F32) + bgate_ref[...])
        merged = gates[:, :D_MODEL] * o_ssd + gates[:, D_MODEL:] * o_conf
        out = jnp.dot(merged.astype(BF16), wo_ref[...], preferred_element_type=F32)
        x1 = jnp.where(is_ctx, xc_ref[rs, :], xl_ref[rs, :]) + g1_ref[...] * out
        x1_ref[rs, :] = x1
        ms = jnp.mean(x1 * x1, axis=-1, keepdims=True)
        h2 = (x1 * lax.rsqrt(ms + EPS)) * nffn_ref[...]
        h2 = h2 * (1.0 + sc2_ref[...]) + sh2_ref[...]
        packed = _pack_halves(h2)
        for qd in range(XW):
            h2i_ref[pl.ds(c * sub * XW + qd, sub, stride=XW), :] = packed[:, qd * LANES:(qd + 1) * LANES]
        h_hi = h2.astype(BF16)
        h_lo = (h2 - h_hi.astype(F32)).astype(BF16)
        logits = jnp.dot(jnp.concatenate([h_hi, h_hi, h_lo], axis=1), wr_ref[...],
                         preferred_element_type=F32) + br_ref[...]
        work = jnp.where(lane < N_EXPERTS, logits, neg_inf)
        vals, idxs = [], []
        for _ in range(TOP_K):
            m = jnp.max(work, axis=-1, keepdims=True)
            idx = jnp.min(jnp.where(work == m, lane_f, jnp.float32(LANES)), axis=-1, keepdims=True)
            vals.append(m)
            idxs.append(idx)
            work = jnp.where(lane_f == idx, neg_inf, work)
        es = [jnp.exp(v - vals[0]) for v in vals]
        denom = es[0] + es[1] + es[2] + es[3]
        member = jnp.zeros((sub, LANES), F32)
        topp = jnp.zeros((sub, LANES), F32)
        for k in range(TOP_K):
            member = member + jnp.where(lane_f == idxs[k], 1.0, 0.0)
            topp = jnp.where(lane == k, es[k] / denom, topp)
        topp_ref[rs, :] = topp
        return idxs, member

    results = [chain(c) for c in range(MIX_CHAINS)]
    member = jnp.concatenate([m for _, m in results], axis=0)
    r_i = lax.broadcasted_iota(jnp.int32, (rows, rows), 0)
    c_i = lax.broadcasted_iota(jnp.int32, (rows, rows), 1)
    earlier = jnp.where(c_i < r_i, 1.0, 0.0).astype(BF16)
    rank = jnp.dot(earlier, member.astype(BF16), preferred_element_type=F32) + cnt_scr[...]
    cnt = cnt_scr[...] + jnp.sum(member, axis=0, keepdims=True)
    cnt_scr[...] = cnt
    cnt_ref[...] = jnp.broadcast_to(cnt, cnt_ref.shape).astype(jnp.int32)
    for c, (idxs, _) in enumerate(results):
        rs = slice(c * sub, (c + 1) * sub)
        er = jnp.zeros((sub, LANES), F32)
        for k in range(TOP_K):
            rank_k = jnp.sum(jnp.where(lane_f == idxs[k], rank[rs, :], 0.0), axis=-1, keepdims=True)
            er = jnp.where(lane == k, idxs[k] + N_EXPERTS * rank_k, er)
        er_ref[rs, :] = er.astype(jnp.int32)


def _mix_out(x_ctx, x_lat, y_ctx, y_lat, u_ctx, u_lat, proj, modr, w_ssd, w_conf, b_conf, b_gate, w_o, norm_ffn,
             w_router3, b_router):
    t_ctx = x_ctx.shape[0]
    t = t_ctx + x_lat.shape[0]
    tm = 512
    upt = tm // UNIT
    ctx_tiles = t_ctx // tm
    full = lambda shape: pl.BlockSpec(shape, lambda m: (0,) * len(shape), pipeline_mode=pl.Buffered(1))
    mod = lambda which: pl.BlockSpec((None, None, 1, D_MODEL), lambda m: (m * upt, which, 0, 0))
    return pl.pallas_call(
        functools.partial(_mix_kernel, ctx_tiles=ctx_tiles),
        grid=(t // tm,),
        in_specs=_two_source_specs(tm, D_MODEL, ctx_tiles, 1) + _two_source_specs(tm, D_INNER, ctx_tiles, 1)
                 + _two_source_specs(tm, D_CONF, ctx_tiles, 1) + [
                  pl.BlockSpec((tm, 2 * D_MODEL), lambda m: (m, COL_GATE // (2 * D_MODEL))),
                  mod(2), mod(4), mod(3),
                  full((D_INNER, D_MODEL)), full((D_CONF, D_MODEL)), full((1, D_MODEL)),
                  full((1, 2 * D_MODEL)), full((D_MODEL, D_MODEL)), full((1, D_MODEL)),
                  full((3 * D_MODEL, LANES)), full((1, LANES))],
        out_specs=[pl.BlockSpec((tm, D_MODEL), lambda m: (m, 0)),
                   pl.BlockSpec((tm * XW, LANES), lambda m: (m, 0)),
                   pl.BlockSpec((tm, LANES), lambda m: (m, 0)),
                   pl.BlockSpec((tm, LANES), lambda m: (m, 0)),
                   pl.BlockSpec((8, LANES), lambda m: (0, 0))],
        out_shape=[jax.ShapeDtypeStruct((t, D_MODEL), F32),
                   jax.ShapeDtypeStruct((t * XW, LANES), U32),
                   jax.ShapeDtypeStruct((t, LANES), jnp.int32),
                   jax.ShapeDtypeStruct((t, LANES), F32),
                   jax.ShapeDtypeStruct((8, LANES), jnp.int32)],
        scratch_shapes=[pltpu.VMEM((1, LANES), F32)],
        compiler_params=_cparams(1, 56 * 1024 * 1024),
        name="mix_out_router",
    )(x_ctx, x_lat, y_ctx, y_lat, u_ctx, u_lat, proj, modr, modr, modr, w_ssd, w_conf, b_conf.reshape(1, D_MODEL),
      b_gate.reshape(1, 2 * D_MODEL), w_o, norm_ffn.reshape(1, D_MODEL), w_router3, b_router)


def _moe_kernel(off_ref, pair_ref, pb_ref, h2i_ref, wgu_hbm, wd_hbm, bgu_ref, bd_ref, out_ref,
                acc, x_scr, y_scr, stage_gu, stage_d, wgu_b, wd_b, wsem, osem):
    e = pl.program_id(0)
    tr = MOE_TILE

    def weight_copies(ex):
        return (pltpu.make_async_copy(wgu_hbm.at[ex], stage_gu, wsem.at[0]),
                pltpu.make_async_copy(wd_hbm.at[ex], stage_d, wsem.at[1]))

    @pl.when(e == 0)
    def _():
        acc[...] = jnp.zeros(acc.shape, F32)
        y_scr[...] = jnp.zeros(y_scr.shape, F32)
        for cp in weight_copies(0):
            cp.start()

    for cp in weight_copies(e):
        cp.wait()

    start = off_ref[e]
    end = off_ref[e + 1]
    ntiles = (end - start + tr - 1) // tr
    bgu = bgu_ref[...]
    bd = bd_ref[...]
    cast_steps = D_MODEL // MOE_CAST_ROWS
    rows_per_step = tr // cast_steps

    def to_bf16(i, c):
        rows = pl.ds(pl.multiple_of(i * MOE_CAST_ROWS, MOE_CAST_ROWS), MOE_CAST_ROWS)
        wgu_b[rows, :] = stage_gu[rows, :].astype(BF16)
        wd_b[rows, :] = stage_d[rows, :].astype(BF16)
        for rr in range(rows_per_step):
            r = i * rows_per_step + rr
            src = jnp.bitwise_and(pair_ref[start + r], -TOP_K)
            x_scr[0, pl.ds(pl.multiple_of(r * XW, XW), XW), :] = h2i_ref[pl.ds(pl.multiple_of(src, XW), XW), :]
        return c

    lax.fori_loop(0, cast_steps, to_bf16, 0, unroll=2)

    @pl.when(e + 1 < N_EXPERTS)
    def _():
        for cp in weight_copies(e + 1):
            cp.start()

    def gather_rows(base, slot, r0, n):
        for r in range(r0, r0 + n):
            src = jnp.bitwise_and(pair_ref[base + r], -TOP_K)
            x_scr[slot, pl.ds(r * XW, XW), :] = h2i_ref[pl.ds(pl.multiple_of(src, XW), XW), :]

    def scatter_rows(base, slot, r0, live):
        dsts, ps = [], []
        for rr in range(8):
            r = base + r0 + rr
            pair = pair_ref[jnp.maximum(r, 0)]
            dsts.append(pl.multiple_of(jnp.bitwise_and(pair, -TOP_K) * (AW // XW), AW))
            pw = lax.bitcast_convert_type(pb_ref[pair], F32)
            ps.append(jnp.where(jnp.logical_and(live, r < end), pw, 0.0))
        olds = [acc[pl.ds(dsts[rr], AW), :] for rr in range(8)]
        news = [olds[rr] + ps[rr] * y_scr[slot, pl.ds((r0 + rr) * AW, AW), :] for rr in range(8)]
        for rr in range(8):
            acc[pl.ds(dsts[rr], AW), :] = news[rr]

    def tile_body(ti, carry):
        slot = jnp.bitwise_and(ti, 1)
        other = 1 - slot
        base = start + ti * tr
        parts = []
        for qd in range(XW):
            w = x_scr[slot, pl.ds(qd, tr, stride=XW), :]
            parts.append(lax.bitcast_convert_type(jnp.bitwise_and(w, jnp.uint32(0xFFFF0000)), F32))
            parts.append(lax.bitcast_convert_type(jnp.left_shift(w, jnp.uint32(16)), F32))
        x = jnp.concatenate(parts, axis=1).astype(BF16)
        for r0 in range(0, tr, 8):
            scatter_rows(base - tr, other, r0, ti > 0)
        gather_rows(base + tr, other, 0, tr)
        gu = jnp.dot(x, wgu_b[...], preferred_element_type=F32) + bgu
        g = jnp.minimum(gu[:, :D_EXPERT], SWIGLU_LIMIT)
        u = jnp.clip(gu[:, D_EXPERT:], -SWIGLU_LIMIT, SWIGLU_LIMIT)
        act = (u + 1.0) * g * _sigmoid(SWIGLU_ALPHA * g)
        y = jnp.dot(act.astype(BF16), wd_b[...], preferred_element_type=F32) + bd
        for qd in range(AW):
            y_scr[slot, pl.ds(qd, tr, stride=AW), :] = y[:, qd * LANES:(qd + 1) * LANES]
        return carry

    lax.fori_loop(0, ntiles, tile_body, 0)

    @pl.when(ntiles > 0)
    def _():
        last = ntiles - 1
        lslot = jnp.bitwise_and(last, 1)
        lbase = start + last * tr

        def last_scatter(r8, c2):
            dsts, ps = [], []
            for rr in range(8):
                r = lbase + r8 * 8 + rr
                pair = pair_ref[r]
                dsts.append(pl.multiple_of(jnp.bitwise_and(pair, -TOP_K) * (AW // XW), AW))
                pw = lax.bitcast_convert_type(pb_ref[pair], F32)
                ps.append(jnp.where(r < end, pw, 0.0))
            olds = [acc[pl.ds(dsts[rr], AW), :] for rr in range(8)]
            news = [olds[rr] + ps[rr] * y_scr[lslot, pl.ds(pl.multiple_of((r8 * 8 + rr) * AW, AW), AW), :]
                    for rr in range(8)]
            for rr in range(8):
                acc[pl.ds(dsts[rr], AW), :] = news[rr]
            return c2

        lax.fori_loop(0, tr // 8, last_scatter, 0)

    @pl.when(e == N_EXPERTS - 1)
    def _():
        cp = pltpu.make_async_copy(acc, out_ref, osem)
        cp.start()
        cp.wait()


def _moe_grouped(h2i, off, row_pair, pbits, w_gu, b_gu, w_down, b_down):
    t = h2i.shape[0] // XW
    assert D_EXPERT == D_MODEL
    grid_spec = pltpu.PrefetchScalarGridSpec(
        num_scalar_prefetch=3,
        grid=(N_EXPERTS,),
        in_specs=[pl.BlockSpec((t * XW, LANES), lambda e, *_: (0, 0), pipeline_mode=pl.Buffered(1)),
                  pl.BlockSpec(memory_space=pl.ANY),
                  pl.BlockSpec(memory_space=pl.ANY),
                  pl.BlockSpec((None, 1, 2 * D_EXPERT), lambda e, *_: (e, 0, 0)),
                  pl.BlockSpec((None, 1, D_MODEL), lambda e, *_: (e, 0, 0))],
        out_specs=pl.BlockSpec(memory_space=pl.ANY),
        scratch_shapes=[pltpu.VMEM((t * AW, LANES), F32),
                        pltpu.VMEM((2, MOE_TILE * XW, LANES), U32),
                        pltpu.VMEM((2, MOE_TILE * AW, LANES), F32),
                        pltpu.VMEM((D_MODEL, 2 * D_EXPERT), F32),
                        pltpu.VMEM((D_EXPERT, D_MODEL), F32),
                        pltpu.VMEM((D_MODEL, 2 * D_EXPERT), BF16),
                        pltpu.VMEM((D_EXPERT, D_MODEL), BF16),
                        pltpu.SemaphoreType.DMA((2,)),
                        pltpu.SemaphoreType.DMA(())])
    return pl.pallas_call(
        _moe_kernel,
        grid_spec=grid_spec,
        out_shape=jax.ShapeDtypeStruct((t * AW, LANES), F32),
        compiler_params=_cparams(1, VMEM_LIMIT),
        name="moe_grouped",
    )(off, row_pair, pbits, h2i, w_gu, w_down, b_gu.reshape(N_EXPERTS, 1, 2 * D_EXPERT),
      b_down.reshape(N_EXPERTS, 1, D_MODEL))


def _residual_kernel(x1_ref, moe_ref, g2_ref, nw_ref, oc_ref, ol_ref, *, normalize, ctx_tiles):
    tm = x1_ref.shape[0]
    moe = jnp.concatenate([moe_ref[pl.ds(qd, tm, stride=AW), :] for qd in range(AW)], axis=1)
    x2 = x1_ref[...] + g2_ref[...] * moe
    if normalize:
        ms = jnp.mean(x2 * x2, axis=-1, keepdims=True)
        x2 = (x2 * lax.rsqrt(ms + EPS)) * nw_ref[...]

    @pl.when(pl.program_id(0) < ctx_tiles)
    def _():
        oc_ref[...] = x2

    @pl.when(pl.program_id(0) >= ctx_tiles)
    def _():
        ol_ref[...] = x2


def _residual(x1, moe_i, modr, norm_w, *, normalize, t_ctx):
    t = x1.shape[0]
    tm = 512
    upt = tm // UNIT
    ctx_tiles = t_ctx // tm
    return pl.pallas_call(
        functools.partial(_residual_kernel, normalize=normalize, ctx_tiles=ctx_tiles),
        grid=(t // tm,),
        in_specs=[pl.BlockSpec((tm, D_MODEL), lambda m: (m, 0)),
                  pl.BlockSpec((tm * AW, LANES), lambda m: (m, 0)),
                  pl.BlockSpec((None, None, 1, D_MODEL), lambda m: (m * upt, 5, 0, 0)),
                  pl.BlockSpec((1, D_MODEL), lambda m: (0, 0))],
        out_specs=_two_source_specs(tm, D_MODEL, ctx_tiles, 1),
        out_shape=[jax.ShapeDtypeStruct((t_ctx, D_MODEL), F32),
                   jax.ShapeDtypeStruct((t - t_ctx, D_MODEL), F32)],
        compiler_params=_cparams(1),
        name="ffn_residual_norm",
    )(x1, moe_i, modr, norm_w.reshape(1, D_MODEL))


def _route_kernel(pos_ref, pair_ref, *, n_pairs):
    def pad(i, c):
        pair_ref[n_pairs + i] = 0
        return c

    lax.fori_loop(0, ROUTE_PAD, pad, 0)

    def place(i8, c):
        for ii in range(8):
            i = i8 * 8 + ii
            pair_ref[pos_ref[i]] = i
        return c

    lax.fori_loop(0, n_pairs // 8, place, 0)


def _route(er_flat, cnt):
    n = er_flat.shape[0]
    off_end = jnp.cumsum(cnt)
    off = off_end - cnt
    expert = jnp.bitwise_and(er_flat, N_EXPERTS - 1)
    onehot = expert[:, None] == jnp.arange(N_EXPERTS, dtype=jnp.int32)[None, :]
    pos = lax.shift_right_logical(er_flat, RANK_SHIFT) + jnp.sum(jnp.where(onehot, off[None, :], 0), axis=1)
    smem = pl.BlockSpec(memory_space=pltpu.SMEM)
    row_pair = pl.pallas_call(
        functools.partial(_route_kernel, n_pairs=n),
        in_specs=[smem],
        out_specs=smem,
        out_shape=jax.ShapeDtypeStruct((n + ROUTE_PAD,), jnp.int32),
        name="moe_route",
    )(pos)
    return jnp.concatenate([off, jnp.broadcast_to(off_end[-1:], (8,))]), row_pair


def kernel(x_prompt, x_sample, state_ssm, c, c_ctx, w_ada, b_ada, norm_mix, norm_ffn, w_in, ssm_conv_w, ssm_conv_b, dt_bias, a_log, d_skip, ssm_norm_w, w_ssd_out, conf_dw_w, conf_dw_b, conf_ln_w, conf_ln_b, w_conf_out, b_conf_out, b_gate, w_o, w_router, b_router, w_gu, b_gu, w_down, b_down, norm_final):
    n_ctx, seq_ctx, _ = x_prompt.shape
    n_lat, seq_lat, _ = x_sample.shape
    depth = w_in.shape[0]
    t_ctx, t_lat = n_ctx * seq_ctx, n_lat * seq_lat
    t_all = t_ctx + t_lat
    assert seq_ctx == UNIT and seq_lat % 1024 == 0 and t_ctx % 1024 == 0
    assert n_lat + 1 <= 8 and seq_lat % GRID_W == 0 and UNIT % GRID_W == 0

    x_ctx, x_lat = x_prompt.reshape(t_ctx, D_MODEL), x_sample.reshape(t_lat, D_MODEL)
    cvec =jnp.concatenate([c_ctx[None, :], c, jnp.zeros((8 - 1 - n_lat, D_MODEL), F32)], axis=0)
    unit_row = jnp.concatenate([jnp.zeros((t_ctx // UNIT,), jnp.int32),
                                1 + jnp.arange(t_lat // UNIT, dtype=jnp.int32) // (seq_lat // UNIT)])
    assert XW == TOP_K and N_EXPERTS == 1 << RANK_SHIFT

    new_states = []
    for l in range(depth):
        mod = _ada_mod(cvec, w_ada[l], b_ada[l])
        modr = mod[unit_row].reshape(t_all // UNIT, 6, 1, D_MODEL)

        w_a, w_b, w_dt = _in_proj_weights(w_in[l])
        proj, dt_raw = _in_proj(x_ctx, x_lat, norm_mix[l], modr, w_a, w_b, w_dt)

        pad_h = lambda v: jnp.pad(v.reshape(2, 1, N_HEADS), ((0, 0), (0, 0), (0, LANES - N_HEADS)))
        dtb, alog = pad_h(dt_bias[l]), pad_h(a_log[l])
        dskip = jnp.repeat(d_skip[l], HEAD_DIM).reshape(1, D_INNER)
        nw = ssm_norm_w[l].reshape(1, D_INNER)
        h0t = state_ssm[:, l].astype(F32).reshape(n_lat, 2, D_INNER, D_STATE)

        xbc_ctx = _ssm_conv(proj, ssm_conv_w[l], ssm_conv_b[l], seq=seq_ctx, nseq=n_ctx, row0=0)
        xbc_lat = _ssm_conv(proj, ssm_conv_w[l], ssm_conv_b[l], seq=seq_lat, nseq=n_lat, row0=t_ctx)
        y_ctx, st_ctx = _ssd(xbc_ctx, proj, dt_raw, dtb, alog, dskip, nw, None,
                             seq=seq_ctx, nseq=n_ctx, row0=0, write_state=True)
        (y_lat,) = _ssd(xbc_lat, proj, dt_raw, dtb, alog, dskip, nw, h0t,
                        seq=seq_lat, nseq=n_lat, row0=t_ctx, write_state=False)
        u_ctx = _conformer(proj, conf_dw_w[l], conf_dw_b[l], conf_ln_w[l], conf_ln_b[l],
                           seg=seq_ctx, ntok=t_ctx, row0=0)
        u_lat = _conformer(proj, conf_dw_w[l], conf_dw_b[l], conf_ln_w[l], conf_ln_b[l],
                           seg=GRID_W, ntok=t_lat, row0=t_ctx)
        wr = jnp.pad(w_router[l], ((0, 0), (0, LANES - N_EXPERTS)))
        wr_hi = wr.astype(BF16)
        wr_lo = (wr - wr_hi.astype(F32)).astype(BF16)
        br = jnp.pad(b_router[l], (0, LANES - N_EXPERTS)).reshape(1, LANES)
        x1, h2i, er, topp, cnt = _mix_out(
            x_ctx, x_lat, y_ctx, y_lat, u_ctx, u_lat, proj, modr, w_ssd_out[l].astype(BF16),
            w_conf_out[l].astype(BF16), b_conf_out[l], b_gate[l], w_o[l].astype(BF16), norm_ffn[l],
            jnp.concatenate([wr_hi, wr_lo, wr_hi], axis=0), br)

        off, row_pair = _route(er[:, :TOP_K].reshape(-1), cnt[0, :N_EXPERTS])
        pbits = lax.bitcast_convert_type(topp[:, :TOP_K], jnp.int32).reshape(-1)
        moe_i = _moe_grouped(h2i, off, row_pair, pbits, w_gu[l], b_gu[l], w_down[l], b_down[l])
        last = l + 1 == depth
        x_ctx, x_lat = _residual(x1, moe_i, modr, norm_final if last else norm_ffn[l], normalize=last, t_ctx=t_ctx)
        new_states.append(st_ctx.reshape(n_ctx, 1, 2, N_HEADS, HEAD_DIM, D_STATE))

    y_prompt = x_ctx.reshape(n_ctx, seq_ctx, D_MODEL)
    y_sample = x_lat.reshape(n_lat, seq_lat, D_MODEL)
    new_state_ssm = new_states[0] if depth == 1 else jnp.concatenate(new_states, axis=1)
    return (y_prompt, y_sample, new_state_ssm)
```

```python
import functools

import jax
import jax.numpy as jnp
from jax import lax
from jax.experimental import pallas as pl
from jax.experimental.pallas import tpu as pltpu

F32 = jnp.float32
BF16 = jnp.bfloat16
U32 = jnp.uint32

D_MODEL = 1024
GRID_W = 64
D_INNER = 2 * D_MODEL
HEAD_DIM = 64
N_HEADS = D_INNER // HEAD_DIM
N_GROUPS = 8
HEADS_PER_GROUP = N_HEADS // N_GROUPS
D_STATE = 128
SSM_CONV = 5
CHUNK = 128
D_XBC = D_INNER + 2 * N_GROUPS * D_STATE
D_CONF = D_MODEL
CONF_KERNEL = 31
N_EXPERTS = 32
TOP_K = 4
D_EXPERT = D_MODEL
SWIGLU_LIMIT = 7.0
SWIGLU_ALPHA = 1.702
EPS = 1e-6

LANES = 128
UNIT = 256
GROUP_W = D_INNER // N_GROUPS
SSD_CHUNKS = 2
SSD_GROUP_UNROLL = 4
MIX_CHAINS = 2
MOE_TILE = 128
ROUTE_PAD = 2 * MOE_TILE
MOE_CAST_ROWS = 16
XW = D_MODEL // (2 * LANES)
AW = D_MODEL // LANES
RANK_SHIFT = 5
VMEM_LIMIT = 62 * 1024 * 1024

COL_Z = 0
COL_XBC = D_INNER
COL_GLU = D_XBC + D_INNER
COL_GATE = D_XBC + D_INNER + 2 * D_CONF
N_MAIN = D_XBC + D_INNER + 2 * D_CONF + 2 * D_MODEL


def _sigmoid(x):
    return 0.5 * jnp.tanh(0.5 * x) + 0.5


def _silu(x):
    return x * _sigmoid(x)


def _cparams(n_axes, vmem=None):
    return pltpu.CompilerParams(
        dimension_semantics=("arbitrary",) * n_axes,
        vmem_limit_bytes=vmem)


def _ada_kernel(c_ref, w_ref, b_ref, o_ref):
    s = _silu(c_ref[...])
    o_ref[...] = jnp.dot(s.astype(BF16), w_ref[...].astype(BF16),
                         preferred_element_type=F32) + b_ref[...]


def _ada_mod(cvec, w_ada, b_ada):
    n = w_ada.shape[1]
    tn = 1536
    return pl.pallas_call(
        _ada_kernel,
        grid=(n // tn,),
        in_specs=[pl.BlockSpec((8, D_MODEL), lambda j: (0, 0)),
                  pl.BlockSpec((D_MODEL, tn), lambda j: (0, j)),
                  pl.BlockSpec((1, tn), lambda j: (0, j))],
        out_specs=pl.BlockSpec((8, tn), lambda j: (0, j)),
        out_shape=jax.ShapeDtypeStruct((8, n), F32),
        compiler_params=_cparams(1, 40 * 1024 * 1024),
        name="ada_mod",
    )(cvec, w_ada, b_ada.reshape(1, n))


def _wprep_kernel(w_ref, wn_ref, oa_ref, ob_ref, odt_ref, *, a_tiles):
    n = pl.program_id(0)
    tn = w_ref.shape[0]
    shift = 2 * N_HEADS

    def put(o_ref, c, rows_t):
        for j in range(D_MODEL // LANES):
            o_ref[j * LANES:(j + 1) * LANES, c * LANES:(c + 1) * LANES] = \
                rows_t[:, j * LANES:(j + 1) * LANES].T.astype(BF16)

    @pl.when(n < a_tiles)
    def _():
        for c in range(tn // LANES):
            put(oa_ref, c, w_ref[c * LANES:(c + 1) * LANES, :])

    @pl.when(n >= a_tiles)
    def _():
        for c in range(tn // LANES):
            lo = w_ref[c * LANES + shift:(c + 1) * LANES, :]
            hi = w_ref[(c + 1) * LANES:(c + 1) * LANES + shift, :] if c + 1 < tn // LANES else wn_ref[0:shift, :]
            put(ob_ref, c, jnp.concatenate([lo, hi], axis=0))

    @pl.when(n == a_tiles)
    def _():
        zeros = jnp.zeros((LANES - N_HEADS, D_MODEL), F32)
        for d in range(2):
            put(odt_ref, d, jnp.concatenate([w_ref[d * N_HEADS:(d + 1) * N_HEADS, :], zeros], axis=0))


def _in_proj_weights(w_in):
    tn = 2048
    n_a = D_INNER + D_XBC
    n_b = 2 * D_CONF + 2 * D_MODEL
    a_tiles = n_a // tn
    assert n_a % tn == 0 and n_b % tn == 0 and 2 * N_HEADS < LANES
    n_tiles = a_tiles + n_b // tn
    assert w_in.shape[1] == n_tiles * tn + 2 * N_HEADS
    w_t = w_in.T
    return pl.pallas_call(
        functools.partial(_wprep_kernel, a_tiles=a_tiles),
        grid=(n_tiles,),
        in_specs=[pl.BlockSpec((tn, D_MODEL), lambda n: (n, 0)),
                  pl.BlockSpec((LANES, D_MODEL), lambda n: ((n + 1) * (tn // LANES), 0))],
        out_specs=[pl.BlockSpec((D_MODEL, tn), lambda n: (0, jnp.minimum(n, a_tiles - 1))),
                   pl.BlockSpec((D_MODEL, tn), lambda n: (0, jnp.maximum(n - a_tiles, 0))),
                   pl.BlockSpec((D_MODEL, 2 * LANES), lambda n: (0, 0))],
        out_shape=[jax.ShapeDtypeStruct((D_MODEL, n_a), BF16),
                   jax.ShapeDtypeStruct((D_MODEL, n_b), BF16),
                   jax.ShapeDtypeStruct((D_MODEL, 2 * LANES), BF16)],
        compiler_params=_cparams(1, 48 * 1024 * 1024),
        name="in_proj_weights",
    )(w_t, w_t)


def _inproj_kernel(xc_ref, xl_ref, nw_ref, sc_ref, sh_ref, wa_ref, wb_ref, wdt_ref, o_ref, dt_ref, h_scr, *,
                   ctx_tiles, a_tiles):
    @pl.when(pl.program_id(1) == 0)
    def _():
        xf = jnp.where(pl.program_id(0) < ctx_tiles, xc_ref[...], xl_ref[...])
        ms = jnp.mean(xf * xf, axis=-1, keepdims=True)
        hn = (xf * lax.rsqrt(ms + EPS)) * nw_ref[...]
        hn = hn * (1.0 + sc_ref[...]) + sh_ref[...]
        hb = hn.astype(BF16)
        h_scr[...] = hb
        dt_ref[...] = jnp.dot(hb, wdt_ref[...], preferred_element_type=F32)

    @pl.when(pl.program_id(1) < a_tiles)
    def _():
        o_ref[...] = jnp.dot(h_scr[...], wa_ref[...], preferred_element_type=F32).astype(BF16)

    @pl.when(pl.program_id(1) >= a_tiles)
    def _():
        o_ref[...] = jnp.dot(h_scr[...], wb_ref[...], preferred_element_type=F32).astype(BF16)


def _two_source_specs(tm, width, ctx_tiles, nargs):
    if nargs == 1:
        return [pl.BlockSpec((tm, width), lambda m: (jnp.minimum(m, ctx_tiles - 1), 0)),
                pl.BlockSpec((tm, width), lambda m: (jnp.maximum(m - ctx_tiles, 0), 0))]
    return [pl.BlockSpec((tm, width), lambda m, n: (jnp.minimum(m, ctx_tiles - 1), 0)),
            pl.BlockSpec((tm, width), lambda m, n: (jnp.maximum(m - ctx_tiles, 0), 0))]


def _in_proj(x_ctx, x_lat, norm_w, modr, w_a, w_b, w_dt):
    t_ctx = x_ctx.shape[0]
    t = t_ctx + x_lat.shape[0]
    tm, tn = 1024, 2048
    upt = tm // UNIT
    a_tiles = w_a.shape[1] // tn
    return pl.pallas_call(
        functools.partial(_inproj_kernel, ctx_tiles=t_ctx // tm, a_tiles=a_tiles),
        grid=(t // tm, N_MAIN // tn),
        in_specs=_two_source_specs(tm, D_MODEL, t_ctx // tm, 2) + [
                  pl.BlockSpec((1, D_MODEL), lambda m, n: (0, 0)),
                  pl.BlockSpec((None, None, 1, D_MODEL), lambda m, n: (m * upt, 1, 0, 0)),
                  pl.BlockSpec((None, None, 1, D_MODEL), lambda m, n: (m * upt, 0, 0, 0)),
                  pl.BlockSpec((D_MODEL, tn), lambda m, n: (0, jnp.minimum(n, a_tiles - 1))),
                  pl.BlockSpec((D_MODEL, tn), lambda m, n: (0, jnp.maximum(n - a_tiles, 0))),
                  pl.BlockSpec((D_MODEL, 2 * LANES), lambda m, n: (0, 0))],
        out_specs=[pl.BlockSpec((tm, tn), lambda m, n: (m, n)),
                   pl.BlockSpec((tm, 2 * LANES), lambda m, n: (m, 0))],
        out_shape=[jax.ShapeDtypeStruct((t, N_MAIN), BF16),
                   jax.ShapeDtypeStruct((t, 2 * LANES), F32)],
        scratch_shapes=[pltpu.VMEM((tm, D_MODEL), BF16)],
        compiler_params=_cparams(2, 48 * 1024 * 1024),
        name="in_proj",
    )(x_ctx, x_lat, norm_w.reshape(1, D_MODEL), modr, modr, w_a, w_b, w_dt)


def _ssm_conv_kernel(x_ref, w_ref, b_ref, o_ref, pad_scr, *, seq):
    cb = x_ref.shape[1]
    pad_scr[0:8, :] = jnp.zeros((8, cb), F32)
    pad_scr[8 + seq:16 + seq, :] = jnp.zeros((8, cb), F32)
    pad_scr[8:8 + seq, :] = x_ref[...].astype(F32)
    half = SSM_CONV // 2
    rows, cw = 256, 512
    for c0 in range(0, cb, cw):
        cs = slice(c0, c0 + cw)
        for r0 in range(0, seq, rows):
            acc = jnp.broadcast_to(b_ref[:, cs], (rows, cw))
            for k in range(SSM_CONV):
                acc = acc + w_ref[k:k + 1, cs] * pad_scr[8 + r0 + k - half:8 + r0 + k - half + rows, cs]
            o_ref[r0:r0 + rows, cs] = _silu(acc).astype(BF16)


def _ssm_conv(proj, conv_w, conv_b, *, seq, nseq, row0):
    cb = 2048
    blk0 = row0 // seq
    return pl.pallas_call(
        functools.partial(_ssm_conv_kernel, seq=seq),
        grid=(nseq, D_XBC // cb),
        in_specs=[pl.BlockSpec((seq, cb), lambda s, j: (blk0 + s, COL_XBC // cb + j)),
                  pl.BlockSpec((SSM_CONV, cb), lambda s, j: (0, j)),
                  pl.BlockSpec((1, cb), lambda s, j: (0, j))],
        out_specs=pl.BlockSpec((seq, cb), lambda s, j: (s, j)),
        out_shape=jax.ShapeDtypeStruct((nseq * seq, D_XBC), BF16),
        scratch_shapes=[pltpu.VMEM((seq + 16, cb), F32)],
        compiler_params=_cparams(2, 40 * 1024 * 1024),
        name="ssm_conv",
    )(proj, conv_w, conv_b.reshape(1, D_XBC))


def _split_bf16(v):
    hi = v.astype(BF16)
    lo = (v - hi.astype(F32)).astype(BF16)
    return jnp.concatenate([hi, lo], axis=1)


def _head_select_matrices():
    j = jnp.arange(2 * LANES, dtype=jnp.int32)[:, None] % LANES
    full = (j == (jnp.arange(N_HEADS * LANES, dtype=jnp.int32)[None, :] // LANES)).astype(BF16)
    exp = (j == (jnp.arange(D_INNER, dtype=jnp.int32)[None, :] // HEAD_DIM)).astype(BF16)
    return full, exp


def _ssd_kernel(*refs, nc, has_h0, write_state):
    (xs_ref, b_ref, c_ref, z_ref, dtr_ref, dtb_ref, alog_ref, dskip_ref, nw_ref, self_ref, sele_ref), rest = \
        refs[:11], refs[11:]
    if has_h0:
        h0_ref, rest = rest[0], rest[1:]
    y_ref, rest = rest[0], rest[1:]
    if write_state:
        st_ref, rest = rest[0], rest[1:]
    h_scr, ybuf, ychunk, colb_scr, wexp_scr, eexp_scr, texp_scr, rowq_scr = rest

    q = CHUNK
    phase = pl.program_id(1)
    c = pl.program_id(2)
    is_fwd = phase == 1
    c_eff = jnp.where(is_fwd, c, nc - 1 - c)

    @pl.when(c == 0)
    def _():
        if has_h0:
            for j in range(D_INNER // LANES):
                h_scr[:, j * LANES:(j + 1) * LANES] = h0_ref[j * LANES:(j + 1) * LANES, :].T
        else:
            h_scr[...] = jnp.zeros(h_scr.shape, F32)

    nrow = SSD_CHUNKS * q
    xdt = dtr_ref[...] + dtb_ref[...]
    dt = jnp.maximum(xdt, 0.0) + jnp.log(1.0 + jnp.exp(-jnp.abs(xdt)))
    a = -jnp.exp(alog_ref[...])
    adt = dt * a
    sgn = jnp.where(is_fwd, 1, -1)
    row = lax.broadcasted_iota(jnp.int32, (nrow, nrow), 0)
    col = lax.broadcasted_iota(jnp.int32, (nrow, nrow), 1)
    tri_blk = jnp.logical_and(row // q == col // q, (col - row) * sgn <= 0)
    cum2 = jnp.dot(tri_blk.astype(BF16), _split_bf16(adt), preferred_element_type=F32)
    cum = cum2[:, :LANES] + cum2[:, LANES:]
    tots = [jnp.where(is_fwd, cum[k * q + q - 1:k * q + q, :], cum[k * q:k * q + 1, :]) for k in range(SSD_CHUNKS)]
    tot_rows = jnp.concatenate([jnp.broadcast_to(t, (q, LANES)) for t in tots], axis=0)
    rowq = (cum - jnp.log(dt)).T
    for k in range(SSD_CHUNKS):
        rowq_scr[k] = rowq[:, k * q:(k + 1) * q]
    wdec = dt * jnp.exp(tot_rows - cum)
    eo = jnp.exp(cum)
    etot = jnp.concatenate([jnp.broadcast_to(jnp.exp(t), (8, LANES)) for t in tots], axis=0)
    colb_scr[...] = jnp.dot(_split_bf16(cum), self_ref[...], preferred_element_type=F32)
    expanded = jnp.dot(jnp.concatenate([_split_bf16(wdec), _split_bf16(eo), _split_bf16(etot)], axis=0),
                       sele_ref[...], preferred_element_type=F32)
    wexp_scr[...] = expanded[0:nrow]
    eexp_scr[...] = expanded[nrow:2 * nrow]
    for k in range(SSD_CHUNKS):
        texp_scr[k] = expanded[2 * nrow + 8 * k:2 * nrow + 8 * (k + 1)]
    lrow = lax.broadcasted_iota(jnp.int32, (q, q), 0)
    lcol = lax.broadcasted_iota(jnp.int32, (q, q), 1)
    tri = (lcol - lrow) * sgn <= 0
    lane_g = lax.broadcasted_iota(jnp.int32, (1, GROUP_W), 1)
    head_mask = [(lane_g // HEAD_DIM == hh).astype(BF16) for hh in range(HEADS_PER_GROUP)]
    neg_inf = jnp.float32(-jnp.inf)

    def group_body(g, carry):
        gs = pl.ds(pl.multiple_of(g * GROUP_W, GROUP_W), GROUP_W)
        ns = pl.ds(pl.multiple_of(g * D_STATE, D_STATE), D_STATE)
        for k in range(SSD_CHUNKS):
            ci = jnp.where(is_fwd, k, SSD_CHUNKS - 1 - k)
            r0 = pl.multiple_of(ci * q, q)
            rs = pl.ds(r0, q)
            bg = b_ref[rs, ns]
            cg = c_ref[rs, ns]
            scores = lax.dot_general(cg, bg, (((1,), (1,)), ((), ())), preferred_element_type=F32)
            xs_g = xs_ref[rs, gs]
            ms, xb = [], []
            for hh in range(HEADS_PER_GROUP):
                h = HEADS_PER_GROUP * g + hh
                colb = colb_scr[rs, pl.ds(pl.multiple_of(h * LANES, LANES), LANES)]
                seg = jnp.where(tri, colb - rowq_scr[ci, pl.ds(h, 1), :], neg_inf)
                ms.append((scores * jnp.exp(seg)).astype(BF16))
                xb.append(xs_g * head_mask[hh])
            y_diag = jnp.dot(jnp.concatenate(ms, axis=1), jnp.concatenate(xb, axis=0),
                             preferred_element_type=F32)
            xd = (xs_g.astype(F32) * wexp_scr[rs, gs]).astype(BF16)
            bt = bg.astype(F32).T.astype(BF16)
            st = jnp.dot(bt, xd, preferred_element_type=F32)
            hg = h_scr[:, gs]
            y_off = jnp.dot(cg, hg.astype(BF16), preferred_element_type=F32) * eexp_scr[rs, gs]
            ychunk[rs, gs] = y_diag + y_off
            h_scr[:, gs] = hg * texp_scr[ci, 0:1, gs] + st
        return carry

    lax.fori_loop(0, N_GROUPS, group_body, 0, unroll=SSD_GROUP_UNROLL)

    rows = pl.ds(pl.multiple_of(c_eff * nrow, nrow), nrow)

    @pl.when(jnp.logical_not(is_fwd))
    def _():
        ybuf[rows, :] = ychunk[...]

    @pl.when(is_fwd)
    def _():
        zf = z_ref[...].astype(F32)
        yt = ychunk[...] + ybuf[rows, :] + xs_ref[...].astype(F32) * dskip_ref[...]
        yz = yt * _silu(zf)
        for g in range(N_GROUPS):
            gs = slice(g * GROUP_W, (g + 1) * GROUP_W)
            blk = yz[:, gs]
            ms = jnp.mean(blk * blk, axis=-1, keepdims=True)
            y_ref[:, gs] = (blk * lax.rsqrt(ms + EPS) * nw_ref[:, gs]).astype(BF16)

    if write_state:
        @pl.when(c == nc - 1)
        def _():
            for j in range(D_INNER // LANES):
                st_ref[j * LANES:(j + 1) * LANES, :] = h_scr[:, j * LANES:(j + 1) * LANES].T


def _ssd(xbc_c, proj, dt_raw, dt_bias, a_log, d_skip, norm_w, h0t, *, seq, nseq, row0, write_state):
    rb = SSD_CHUNKS * CHUNK
    nc = seq // rb
    blk0 = row0 // rb
    has_h0 = h0t is not None
    assert seq % rb == 0 and row0 % rb == 0

    def tok(s, p, c):
        return s * nc + p * c + (1 - p) * (nc - 1 - c)

    in_specs = [
        pl.BlockSpec((rb, D_INNER), lambda s, p, c: (tok(s, p, c), 0)),
        pl.BlockSpec((rb, N_GROUPS * D_STATE), lambda s, p, c: (tok(s, p, c), 2)),
        pl.BlockSpec((rb, N_GROUPS * D_STATE), lambda s, p, c: (tok(s, p, c), 3)),
        pl.BlockSpec((rb, D_INNER), lambda s, p, c: (blk0 + s * nc + p * c, COL_Z // D_INNER)),
        pl.BlockSpec((rb, LANES), lambda s, p, c: (blk0 + tok(s, p, c), 1 - p)),
        pl.BlockSpec((None, 1, LANES), lambda s, p, c: (1 - p, 0, 0)),
        pl.BlockSpec((None, 1, LANES), lambda s, p, c: (1 - p, 0, 0)),
        pl.BlockSpec((1, D_INNER), lambda s, p, c: (0, 0)),
        pl.BlockSpec((1, D_INNER), lambda s, p, c: (0, 0)),
        pl.BlockSpec((2 * LANES, N_HEADS * LANES), lambda s, p, c: (0, 0)),
        pl.BlockSpec((2 * LANES, D_INNER), lambda s, p, c: (0, 0)),
    ]
    sel_full, sel_exp = _head_select_matrices()
    args = [xbc_c, xbc_c, xbc_c, proj, dt_raw, dt_bias, a_log, d_skip, norm_w, sel_full, sel_exp]
    if has_h0:
        in_specs.append(pl.BlockSpec((None, None, D_INNER, D_STATE), lambda s, p, c: (s, 1 - p, 0, 0)))
        args.append(h0t)
    out_specs = [pl.BlockSpec((rb, D_INNER), lambda s, p, c: (s * nc + p * c, 0))]
    out_shape = [jax.ShapeDtypeStruct((nseq * seq, D_INNER), BF16)]
    if write_state:
        out_specs.append(pl.BlockSpec((None, None, D_INNER, D_STATE), lambda s, p, c: (s, 1 - p, 0, 0)))
        out_shape.append(jax.ShapeDtypeStruct((nseq, 2, D_INNER, D_STATE), F32))
    return pl.pallas_call(
        functools.partial(_ssd_kernel, nc=nc, has_h0=has_h0, write_state=write_state),
        grid=(nseq, 2, nc),
        in_specs=in_specs,
        out_specs=out_specs,
        out_shape=out_shape,
        scratch_shapes=[pltpu.VMEM((D_STATE, D_INNER), F32),
                        pltpu.VMEM((seq, D_INNER), F32),
                        pltpu.VMEM((rb, D_INNER), F32),
                        pltpu.VMEM((rb, N_HEADS * LANES), F32),
                        pltpu.VMEM((rb, D_INNER), F32),
                        pltpu.VMEM((rb, D_INNER), F32),
                        pltpu.VMEM((SSD_CHUNKS, 8, D_INNER), F32),
                        pltpu.VMEM((SSD_CHUNKS, LANES, CHUNK), F32)],
        compiler_params=_cparams(3, 48 * 1024 * 1024),
        name="ssd_scan",
    )(*args)


def _conf_kernel(glu_ref, w_ref, b_ref, lnw_ref, lnb_ref, o_ref, pad_scr, sh_scr, conv_scr, *, seg):
    rows = glu_ref.shape[0]
    nseg = rows // seg
    half = CONF_KERNEL // 2
    front = 16
    span = seg + 24
    a = glu_ref[:, :D_CONF].astype(F32)
    b = glu_ref[:, D_CONF:].astype(F32)
    u = a * _sigmoid(b)
    for i in range(nseg):
        pad_scr[i, 0:front, :] = jnp.zeros((front, D_CONF), F32)
        pad_scr[i, front + seg:front + seg + 16, :] = jnp.zeros((16, D_CONF), F32)
        pad_scr[i, front:front + seg, :] = u[i * seg:(i + 1) * seg, :]
    for s in range(8):
        for i in range(nseg):
            for cbi in range(D_CONF // LANES):
                cs = slice(cbi * LANES, (cbi + 1) * LANES)
                for r0 in range(0, span, 56):
                    n = min(56, span - r0)
                    sh_scr[s, i, r0:r0 + n, cs] = pad_scr[i, s + r0:s + r0 + n, cs]
    rb = 64
    for i in range(nseg):
        for cbi in range(D_CONF // LANES):
            cs = slice(cbi * LANES, (cbi + 1) * LANES)
            for r0 in range(0, seg, rb):
                acc = jnp.broadcast_to(b_ref[:, cs], (rb, LANES))
                for k in range(CONF_KERNEL):
                    start = front + r0 + k - half
                    al = start - start % 8
                    acc = acc + w_ref[k:k + 1, cs] * sh_scr[start % 8, i, al:al + rb, cs]
                conv_scr[i * seg + r0:i * seg + r0 + rb, cs] = acc
    v = conv_scr[...]
    mu = jnp.mean(v, axis=-1, keepdims=True)
    vc = v - mu
    var = jnp.mean(vc * vc, axis=-1, keepdims=True)
    ln = (vc * lax.rsqrt(var + EPS)) * lnw_ref[...] + lnb_ref[...]
    o_ref[...] = _silu(ln).astype(BF16)


def _conformer(proj, dw_w, dw_b, ln_w, ln_b, *, seg, ntok, row0):
    rows = UNIT
    blk0 = row0 // rows
    return pl.pallas_call(
        functools.partial(_conf_kernel, seg=seg),
        grid=(ntok // rows,),
        in_specs=[pl.BlockSpec((rows, 2 * D_CONF), lambda i: (blk0 + i, COL_GLU // (2 * D_CONF))),
                  pl.BlockSpec((CONF_KERNEL, D_CONF), lambda i: (0, 0)),
                  pl.BlockSpec((1, D_CONF), lambda i: (0, 0)),
                  pl.BlockSpec((1, D_CONF), lambda i: (0, 0)),
                  pl.BlockSpec((1, D_CONF), lambda i: (0, 0))],
        out_specs=pl.BlockSpec((rows, D_CONF), lambda i: (i, 0)),
        out_shape=jax.ShapeDtypeStruct((ntok, D_CONF), BF16),
        scratch_shapes=[pltpu.VMEM((rows // seg, seg + 32, D_CONF), F32),
                        pltpu.VMEM((8, rows // seg, seg + 24, D_CONF), F32),
                        pltpu.VMEM((rows, D_CONF), F32)],
        compiler_params=_cparams(1, 40 * 1024 * 1024),
        name="conformer_conv",
    )(proj, dw_w, dw_b.reshape(1, D_CONF), ln_w.reshape(1, D_CONF), ln_b.reshape(1, D_CONF))


def _pack_halves(x):
    outs = []
    for cb in range(x.shape[1] // (2 * LANES)):
        hi = x[:, cb * 2 * LANES:cb * 2 * LANES + LANES].astype(BF16).astype(F32)
        lo = x[:, cb * 2 * LANES + LANES:(cb + 1) * 2 * LANES].astype(BF16).astype(F32)
        hw = lax.bitcast_convert_type(hi, U32)
        lw = jnp.right_shift(lax.bitcast_convert_type(lo, U32), jnp.uint32(16))
        outs.append(jnp.bitwise_or(hw, lw))
    return jnp.concatenate(outs, axis=1)


def _mix_kernel(xc_ref, xl_ref, yc_ref, yl_ref, uc_ref, ul_ref, gate_ref, g1_ref, sc2_ref, sh2_ref,
                wssd_ref, wconf_ref, bconf_ref, bgate_ref, wo_ref, nffn_ref, wr_ref, br_ref,
                x1_ref, h2i_ref, er_ref, topp_ref, cnt_ref, cnt_scr, *, ctx_tiles):
    @pl.when(pl.program_id(0) == 0)
    def _():
        cnt_scr[...] = jnp.zeros(cnt_scr.shape, F32)

    is_ctx = pl.program_id(0) < ctx_tiles
    rows = x1_ref.shape[0]
    sub = rows // MIX_CHAINS
    lane = lax.broadcasted_iota(jnp.int32, (sub, LANES), 1)
    lane_f = lane.astype(F32)
    neg_inf = jnp.float32(-jnp.inf)

    def chain(c):
        rs = slice(c * sub, (c + 1) * sub)
        y_in = jnp.where(is_ctx, yc_ref[rs, :], yl_ref[rs, :])
        u_in = jnp.where(is_ctx, uc_ref[rs, :], ul_ref[rs, :])
        o_ssd = jnp.dot(y_in, wssd_ref[...], preferred_element_type=F32)
        o_conf = jnp.dot(u_in, wconf_ref[...], preferred_element_type=F32) + bconf_ref[...]
        gates = _sigmoid(gate_ref[rs, :].astype(F32) + bgate_ref[...])
        merged = gates[:, :D_MODEL] * o_ssd + gates[:, D_MODEL:] * o_conf
        out = jnp.dot(merged.astype(BF16), wo_ref[...], preferred_element_type=F32)
        x1 = jnp.where(is_ctx, xc_ref[rs, :], xl_ref[rs, :]) + g1_ref[...] * out
        x1_ref[rs, :] = x1
        ms = jnp.mean(x1 * x1, axis=-1, keepdims=True)
        h2 = (x1 * lax.rsqrt(ms + EPS)) * nffn_ref[...]
        h2 = h2 * (1.0 + sc2_ref[...]) + sh2_ref[...]
        packed = _pack_halves(h2)
        for qd in range(XW):
            h2i_ref[pl.ds(c * sub * XW + qd, sub, stride=XW), :] = packed[:, qd * LANES:(qd + 1) * LANES]
        h_hi = h2.astype(BF16)
        h_lo = (h2 - h_hi.astype(F32)).astype(BF16)
        logits = jnp.dot(jnp.concatenate([h_hi, h_hi, h_lo], axis=1), wr_ref[...],
                         preferred_element_type=F32) + br_ref[...]
        work = jnp.where(lane < N_EXPERTS, logits, neg_inf)
        vals, idxs = [], []
        for _ in range(TOP_K):
            m = jnp.max(work, axis=-1, keepdims=True)
            idx = jnp.min(jnp.where(work == m, lane_f, jnp.float32(LANES)), axis=-1, keepdims=True)
            vals.append(m)
            idxs.append(idx)
            work = jnp.where(lane_f == idx, neg_inf, work)
        es = [jnp.exp(v - vals[0]) for v in vals]
        denom = es[0] + es[1] + es[2] + es[3]
        member = jnp.zeros((sub, LANES), F32)
        topp = jnp.zeros((sub, LANES), F32)
        for k in range(TOP_K):
            member = member + jnp.where(lane_f == idxs[k], 1.0, 0.0)
            topp = jnp.where(lane == k, es[k] / denom, topp)
        topp_ref[rs, :] = topp
        return idxs, member

    results = [chain(c) for c in range(MIX_CHAINS)]
    member = jnp.concatenate([m for _, m in results], axis=0)
    r_i = lax.broadcasted_iota(jnp.int32, (rows, rows), 0)
    c_i = lax.broadcasted_iota(jnp.int32, (rows, rows), 1)
    earlier = jnp.where(c_i < r_i, 1.0, 0.0).astype(BF16)
    rank = jnp.dot(earlier, member.astype(BF16), preferred_element_type=F32) + cnt_scr[...]
    cnt = cnt_scr[...] + jnp.sum(member, axis=0, keepdims=True)
    cnt_scr[...] = cnt
    cnt_ref[...] = jnp.broadcast_to(cnt, cnt_ref.shape).astype(jnp.int32)
    for c, (idxs, _) in enumerate(results):
        rs = slice(c * sub, (c + 1) * sub)
        er = jnp.zeros((sub, LANES), F32)
        for k in range(TOP_K):
            rank_k = jnp.sum(jnp.where(lane_f == idxs[k], rank[rs, :], 0.0), axis=-1, keepdims=True)
            er = jnp.where(lane == k, idxs[k] + N_EXPERTS * rank_k, er)
        er_ref[rs, :] = er.astype(jnp.int32)


def _mix_out(x_ctx, x_lat, y_ctx, y_lat, u_ctx, u_lat, proj, modr, w_ssd, w_conf, b_conf, b_gate, w_o, norm_ffn,
             w_router3, b_router):
    t_ctx = x_ctx.shape[0]
    t = t_ctx + x_lat.shape[0]
    tm = 512
    upt = tm // UNIT
    ctx_tiles = t_ctx // tm
    full = lambda shape: pl.BlockSpec(shape, lambda m: (0,) * len(shape), pipeline_mode=pl.Buffered(1))
    mod = lambda which: pl.BlockSpec((None, None, 1, D_MODEL), lambda m: (m * upt, which, 0, 0))
    return pl.pallas_call(
        functools.partial(_mix_kernel, ctx_tiles=ctx_tiles),
        grid=(t // tm,),
        in_specs=_two_source_specs(tm, D_MODEL, ctx_tiles, 1) + _two_source_specs(tm, D_INNER, ctx_tiles, 1)
                 + _two_source_specs(tm, D_CONF, ctx_tiles, 1) + [
                  pl.BlockSpec((tm, 2 * D_MODEL), lambda m: (m, COL_GATE // (2 * D_MODEL))),
                  mod(2), mod(4), mod(3),
                  full((D_INNER, D_MODEL)), full((D_CONF, D_MODEL)), full((1, D_MODEL)),
                  full((1, 2 * D_MODEL)), full((D_MODEL, D_MODEL)), full((1, D_MODEL)),
                  full((3 * D_MODEL, LANES)), full((1, LANES))],
        out_specs=[pl.BlockSpec((tm, D_MODEL), lambda m: (m, 0)),
                   pl.BlockSpec((tm * XW, LANES), lambda m: (m, 0)),
                   pl.BlockSpec((tm, LANES), lambda m: (m, 0)),
                   pl.BlockSpec((tm, LANES), lambda m: (m, 0)),
                   pl.BlockSpec((8, LANES), lambda m: (0, 0))],
        out_shape=[jax.ShapeDtypeStruct((t, D_MODEL), F32),
                   jax.ShapeDtypeStruct((t * XW, LANES), U32),
                   jax.ShapeDtypeStruct((t, LANES), jnp.int32),
                   jax.ShapeDtypeStruct((t, LANES), F32),
                   jax.ShapeDtypeStruct((8, LANES), jnp.int32)],
        scratch_shapes=[pltpu.VMEM((1, LANES), F32)],
        compiler_params=_cparams(1, 56 * 1024 * 1024),
        name="mix_out_router",
    )(x_ctx, x_lat, y_ctx, y_lat, u_ctx, u_lat, proj, modr, modr, modr, w_ssd, w_conf, b_conf.reshape(1, D_MODEL),
      b_gate.reshape(1, 2 * D_MODEL), w_o, norm_ffn.reshape(1, D_MODEL), w_router3, b_router)


def _moe_kernel(off_ref, pair_ref, pb_ref, h2i_ref, wgu_hbm, wd_hbm, bgu_ref, bd_ref, out_ref,
                acc, x_scr, y_scr, stage_gu, stage_d, wgu_b, wd_b, wsem, osem):
    e = pl.program_id(0)
    tr = MOE_TILE

    def weight_copies(ex):
        return (pltpu.make_async_copy(wgu_hbm.at[ex], stage_gu, wsem.at[0]),
                pltpu.make_async_copy(wd_hbm.at[ex], stage_d, wsem.at[1]))

    @pl.when(e == 0)
    def _():
        acc[...] = jnp.zeros(acc.shape, F32)
        for cp in weight_copies(0):
            cp.start()

    for cp in weight_copies(e):
        cp.wait()

    start = off_ref[e]
    end = off_ref[e + 1]
    ntiles = (end - start + tr - 1) // tr
    bgu = bgu_ref[...]
    bd = bd_ref[...]
    cast_steps = D_MODEL // MOE_CAST_ROWS
    rows_per_step = tr // cast_steps

    def to_bf16(i, c):
        rows = pl.ds(pl.multiple_of(i * MOE_CAST_ROWS, MOE_CAST_ROWS), MOE_CAST_ROWS)
        wgu_b[rows, :] = stage_gu[rows, :].astype(BF16)
        wd_b[rows, :] = stage_d[rows, :].astype(BF16)
        for rr in range(rows_per_step):
            r = i * rows_per_step + rr
            src = jnp.bitwise_and(pair_ref[start + r], -TOP_K)
            x_scr[0, pl.ds(pl.multiple_of(r * XW, XW), XW), :] = h2i_ref[pl.ds(pl.multiple_of(src, XW), XW), :]
        return c

    lax.fori_loop(0, cast_steps, to_bf16, 0, unroll=2)

    @pl.when(e + 1 < N_EXPERTS)
    def _():
        for cp in weight_copies(e + 1):
            cp.start()

    def gather_rows(base, slot, r0, n):
        for r in range(r0, r0 + n):
            src = jnp.bitwise_and(pair_ref[base + r], -TOP_K)
            x_scr[slot, pl.ds(r * XW, XW), :] = h2i_ref[pl.ds(pl.multiple_of(src, XW), XW), :]

    def scatter_rows(base, slot, r0):
        dsts, ps = [], []
        for rr in range(8):
            pair = pair_ref[base + r0 + rr]
            dsts.append(pl.multiple_of(jnp.bitwise_and(pair, -TOP_K) * (AW // XW), AW))
            ps.append(lax.bitcast_convert_type(pb_ref[pair], F32))
        olds = [acc[pl.ds(dsts[rr], AW), :] for rr in range(8)]
        news = [olds[rr] + ps[rr] * y_scr[slot, pl.ds((r0 + rr) * AW, AW), :] for rr in range(8)]
        for rr in range(8):
            acc[pl.ds(dsts[rr], AW), :] = news[rr]

    def tile_step(ti, first):
        slot = jnp.bitwise_and(ti, 1)
        other = 1 - slot
        base = start + ti * tr
        parts = []
        for qd in range(XW):
            w = x_scr[slot, pl.ds(qd, tr, stride=XW), :]
            parts.append(lax.bitcast_convert_type(jnp.bitwise_and(w, jnp.uint32(0xFFFF0000)), F32))
            parts.append(lax.bitcast_convert_type(jnp.left_shift(w, jnp.uint32(16)), F32))
        x = jnp.concatenate(parts, axis=1).astype(BF16)
        if not first:
            for r0 in range(0, tr, 8):
                scatter_rows(base - tr, other, r0)
        gather_rows(base + tr, other, 0, tr)
        gu = jnp.dot(x, wgu_b[...], preferred_element_type=F32) + bgu
        g = jnp.minimum(gu[:, :D_EXPERT], SWIGLU_LIMIT)
        u = jnp.clip(gu[:, D_EXPERT:], -SWIGLU_LIMIT, SWIGLU_LIMIT)
        act = (u + 1.0) * g * _sigmoid(SWIGLU_ALPHA * g)
        y = jnp.dot(act.astype(BF16), wd_b[...], preferred_element_type=F32) + bd
        for qd in range(AW):
            y_scr[slot, pl.ds(qd, tr, stride=AW), :] = y[:, qd * LANES:(qd + 1) * LANES]

    @pl.when(ntiles > 0)
    def _():
        tile_step(0, True)

    def tile_body(ti, carry):
        tile_step(ti, False)
        return carry

    lax.fori_loop(1, ntiles, tile_body, 0)

    @pl.when(ntiles > 0)
    def _():
        last = ntiles - 1
        lslot = jnp.bitwise_and(last, 1)
        lbase = start + last * tr

        def last_scatter(r8, c2):
            dsts, ps = [], []
            for rr in range(8):
                r = lbase + r8 * 8 + rr
                pair = pair_ref[r]
                dsts.append(pl.multiple_of(jnp.bitwise_and(pair, -TOP_K) * (AW // XW), AW))
                pw = lax.bitcast_convert_type(pb_ref[pair], F32)
                ps.append(jnp.where(r < end, pw, 0.0))
            olds = [acc[pl.ds(dsts[rr], AW), :] for rr in range(8)]
            news = [olds[rr] + ps[rr] * y_scr[lslot, pl.ds(pl.multiple_of((r8 * 8 + rr) * AW, AW), AW), :]
                    for rr in range(8)]
            for rr in range(8):
                acc[pl.ds(dsts[rr], AW), :] = news[rr]
            return c2

        lax.fori_loop(0, tr // 8, last_scatter, 0)

    @pl.when(e == N_EXPERTS - 1)
    def _():
        cp = pltpu.make_async_copy(acc, out_ref, osem)
        cp.start()
        cp.wait()


def _moe_grouped(h2i, off, row_pair, pbits, w_gu, b_gu, w_down, b_down):
    t = h2i.shape[0] // XW
    assert D_EXPERT == D_MODEL
    grid_spec = pltpu.PrefetchScalarGridSpec(
        num_scalar_prefetch=3,
        grid=(N_EXPERTS,),
        in_specs=[pl.BlockSpec((t * XW, LANES), lambda e, *_: (0, 0), pipeline_mode=pl.Buffered(1)),
                  pl.BlockSpec(memory_space=pl.ANY),
                  pl.BlockSpec(memory_space=pl.ANY),
                  pl.BlockSpec((None, 1, 2 * D_EXPERT), lambda e, *_: (e, 0, 0)),
                  pl.BlockSpec((None, 1, D_MODEL), lambda e, *_: (e, 0, 0))],
        out_specs=pl.BlockSpec(memory_space=pl.ANY),
        scratch_shapes=[pltpu.VMEM((t * AW, LANES), F32),
                        pltpu.VMEM((2, MOE_TILE * XW, LANES), U32),
                        pltpu.VMEM((2, MOE_TILE * AW, LANES), F32),
                        pltpu.VMEM((D_MODEL, 2 * D_EXPERT), F32),
                        pltpu.VMEM((D_EXPERT, D_MODEL), F32),
                        pltpu.VMEM((D_MODEL, 2 * D_EXPERT), BF16),
                        pltpu.VMEM((D_EXPERT, D_MODEL), BF16),
                        pltpu.SemaphoreType.DMA((2,)),
                        pltpu.SemaphoreType.DMA(())])
    return pl.pallas_call(
        _moe_kernel,
        grid_spec=grid_spec,
        out_shape=jax.ShapeDtypeStruct((t * AW, LANES), F32),
        compiler_params=_cparams(1, VMEM_LIMIT),
        name="moe_grouped",
    )(off, row_pair, pbits, h2i, w_gu, w_down, b_gu.reshape(N_EXPERTS, 1, 2 * D_EXPERT),
      b_down.reshape(N_EXPERTS, 1, D_MODEL))


def _residual_kernel(x1_ref, moe_ref, g2_ref, nw_ref, oc_ref, ol_ref, *, normalize, ctx_tiles):
    tm = x1_ref.shape[0]
    moe = jnp.concatenate([moe_ref[pl.ds(qd, tm, stride=AW), :] for qd in range(AW)], axis=1)
    x2 = x1_ref[...] + g2_ref[...] * moe
    if normalize:
        ms = jnp.mean(x2 * x2, axis=-1, keepdims=True)
        x2 = (x2 * lax.rsqrt(ms + EPS)) * nw_ref[...]

    @pl.when(pl.program_id(0) < ctx_tiles)
    def _():
        oc_ref[...] = x2

    @pl.when(pl.program_id(0) >= ctx_tiles)
    def _():
        ol_ref[...] = x2


def _residual(x1, moe_i, modr, norm_w, *, normalize, t_ctx):
    t = x1.shape[0]
    tm = 512
    upt = tm // UNIT
    ctx_tiles = t_ctx // tm
    return pl.pallas_call(
        functools.partial(_residual_kernel, normalize=normalize, ctx_tiles=ctx_tiles),
        grid=(t // tm,),
        in_specs=[pl.BlockSpec((tm, D_MODEL), lambda m: (m, 0)),
                  pl.BlockSpec((tm * AW, LANES), lambda m: (m, 0)),
                  pl.BlockSpec((None, None, 1, D_MODEL), lambda m: (m * upt, 5, 0, 0)),
                  pl.BlockSpec((1, D_MODEL), lambda m: (0, 0))],
        out_specs=_two_source_specs(tm, D_MODEL, ctx_tiles, 1),
        out_shape=[jax.ShapeDtypeStruct((t_ctx, D_MODEL), F32),
                   jax.ShapeDtypeStruct((t - t_ctx, D_MODEL), F32)],
        compiler_params=_cparams(1),
        name="ffn_residual_norm",
    )(x1, moe_i, modr, norm_w.reshape(1, D_MODEL))


def _route_kernel(pos_ref, pair_ref, *, n_pairs):
    def pad(i, c):
        pair_ref[n_pairs + i] = 0
        return c

    lax.fori_loop(0, ROUTE_PAD, pad, 0)

    def place(i8, c):
        for ii in range(8):
            i = i8 * 8 + ii
            pair_ref[pos_ref[i]] = i
        return c

    lax.fori_loop(0, n_pairs // 8, place, 0)


def _route(er_flat, cnt):
    n = er_flat.shape[0]
    off_end = jnp.cumsum(cnt)
    off = off_end - cnt
    expert = jnp.bitwise_and(er_flat, N_EXPERTS - 1)
    onehot = expert[:, None] == jnp.arange(N_EXPERTS, dtype=jnp.int32)[None, :]
    pos = lax.shift_right_logical(er_flat, RANK_SHIFT) + jnp.sum(jnp.where(onehot, off[None, :], 0), axis=1)
    smem = pl.BlockSpec(memory_space=pltpu.SMEM)
    row_pair = pl.pallas_call(
        functools.partial(_route_kernel, n_pairs=n),
        in_specs=[smem],
        out_specs=smem,
        out_shape=jax.ShapeDtypeStruct((n + ROUTE_PAD,), jnp.int32),
        name="moe_route",
    )(pos)
    return jnp.concatenate([off, jnp.broadcast_to(off_end[-1:], (8,))]), row_pair


def kernel(x_prompt, x_sample, state_ssm, c, c_ctx, w_ada, b_ada, norm_mix, norm_ffn, w_in, ssm_conv_w, ssm_conv_b, dt_bias, a_log, d_skip, ssm_norm_w, w_ssd_out, conf_dw_w, conf_dw_b, conf_ln_w, conf_ln_b, w_conf_out, b_conf_out, b_gate, w_o, w_router, b_router, w_gu, b_gu, w_down, b_down, norm_final):
    n_ctx, seq_ctx, _ = x_prompt.shape
    n_lat, seq_lat, _ = x_sample.shape
    depth = w_in.shape[0]
    t_ctx, t_lat = n_ctx * seq_ctx, n_lat * seq_lat
    t_all = t_ctx + t_lat
    assert seq_ctx == UNIT and seq_lat % 1024 == 0 and t_ctx % 1024 == 0
    assert n_lat + 1 <= 8 and seq_lat % GRID_W == 0 and UNIT % GRID_W == 0

    x_ctx, x_lat = x_prompt.reshape(t_ctx, D_MODEL), x_sample.reshape(t_lat, D_MODEL)
    cvec =jnp.concatenate([c_ctx[None, :], c, jnp.zeros((8 - 1 - n_lat, D_MODEL), F32)], axis=0)
    unit_row = jnp.concatenate([jnp.zeros((t_ctx // UNIT,), jnp.int32),
                                1 + jnp.arange(t_lat // UNIT, dtype=jnp.int32) // (seq_lat // UNIT)])
    assert XW == TOP_K and N_EXPERTS == 1 << RANK_SHIFT

    new_states = []
    for l in range(depth):
        mod = _ada_mod(cvec, w_ada[l], b_ada[l])
        modr = mod[unit_row].reshape(t_all // UNIT, 6, 1, D_MODEL)

        w_a, w_b, w_dt = _in_proj_weights(w_in[l])
        proj, dt_raw = _in_proj(x_ctx, x_lat, norm_mix[l], modr, w_a, w_b, w_dt)

        pad_h = lambda v: jnp.pad(v.reshape(2, 1, N_HEADS), ((0, 0), (0, 0), (0, LANES - N_HEADS)))
        dtb, alog = pad_h(dt_bias[l]), pad_h(a_log[l])
        dskip = jnp.repeat(d_skip[l], HEAD_DIM).reshape(1, D_INNER)
        nw = ssm_norm_w[l].reshape(1, D_INNER)
        h0t = state_ssm[:, l].astype(F32).reshape(n_lat, 2, D_INNER, D_STATE)

        xbc_ctx = _ssm_conv(proj, ssm_conv_w[l], ssm_conv_b[l], seq=seq_ctx, nseq=n_ctx, row0=0)
        xbc_lat = _ssm_conv(proj, ssm_conv_w[l], ssm_conv_b[l], seq=seq_lat, nseq=n_lat, row0=t_ctx)
        y_ctx, st_ctx = _ssd(xbc_ctx, proj, dt_raw, dtb, alog, dskip, nw, None,
                             seq=seq_ctx, nseq=n_ctx, row0=0, write_state=True)
        (y_lat,) = _ssd(xbc_lat, proj, dt_raw, dtb, alog, dskip, nw, h0t,
                        seq=seq_lat, nseq=n_lat, row0=t_ctx, write_state=False)
        u_ctx = _conformer(proj, conf_dw_w[l], conf_dw_b[l], conf_ln_w[l], conf_ln_b[l],
                           seg=seq_ctx, ntok=t_ctx, row0=0)
        u_lat = _conformer(proj, conf_dw_w[l], conf_dw_b[l], conf_ln_w[l], conf_ln_b[l],
                           seg=GRID_W, ntok=t_lat, row0=t_ctx)
        wr = jnp.pad(w_router[l], ((0, 0), (0, LANES - N_EXPERTS)))
        wr_hi = wr.astype(BF16)
        wr_lo = (wr - wr_hi.astype(F32)).astype(BF16)
        br = jnp.pad(b_router[l], (0, LANES - N_EXPERTS)).reshape(1, LANES)
        x1, h2i, er, topp, cnt = _mix_out(
            x_ctx, x_lat, y_ctx, y_lat, u_ctx, u_lat, proj, modr, w_ssd_out[l].astype(BF16),
            w_conf_out[l].astype(BF16), b_conf_out[l], b_gate[l], w_o[l].astype(BF16), norm_ffn[l],
            jnp.concatenate([wr_hi, wr_lo, wr_hi], axis=0), br)

        off, row_pair = _route(er[:, :TOP_K].reshape(-1), cnt[0, :N_EXPERTS])
        pbits = lax.bitcast_convert_type(topp[:, :TOP_K], jnp.int32).reshape(-1)
        moe_i = _moe_grouped(h2i, off, row_pair, pbits, w_gu[l], b_gu[l], w_down[l], b_down[l])
        last = l + 1 == depth
        x_ctx, x_lat = _residual(x1, moe_i, modr, norm_final if last else norm_ffn[l], normalize=last, t_ctx=t_ctx)
        new_states.append(st_ctx.reshape(n_ctx, 1, 2, N_HEADS, HEAD_DIM, D_STATE))

    y_prompt = x_ctx.reshape(n_ctx, seq_ctx, D_MODEL)
    y_sample = x_lat.reshape(n_lat, seq_lat, D_MODEL)
    new_state_ssm = new_states[0] if depth == 1 else jnp.concatenate(new_states, axis=1)
    return (y_prompt, y_sample, new_state_ssm)
```

```python
import functools

import jax
import jax.numpy as jnp
from jax import lax
from jax.experimental import pallas as pl
from jax.experimental.pallas import tpu as pltpu

F32 = jnp.float32
BF16 = jnp.bfloat16
U32 = jnp.uint32

D_MODEL = 1024
GRID_W = 64
D_INNER = 2 * D_MODEL
HEAD_DIM = 64
N_HEADS = D_INNER // HEAD_DIM
N_GROUPS = 8
HEADS_PER_GROUP = N_HEADS // N_GROUPS
D_STATE = 128
SSM_CONV = 5
CHUNK = 128
D_XBC = D_INNER + 2 * N_GROUPS * D_STATE
D_CONF = D_MODEL
CONF_KERNEL = 31
N_EXPERTS = 32
TOP_K = 4
D_EXPERT = D_MODEL
SWIGLU_LIMIT = 7.0
SWIGLU_ALPHA = 1.702
EPS = 1e-6

LANES = 128
UNIT = 256
GROUP_W = D_INNER // N_GROUPS
SSD_CHUNKS = 2
SSD_GROUP_UNROLL = 4
MIX_CHAINS = 2
MOE_TILE = 128
ROUTE_PAD = 2 * MOE_TILE
MOE_CAST_ROWS = 16
XW = D_MODEL // (2 * LANES)
AW = D_MODEL // LANES
RANK_SHIFT = 5
VMEM_LIMIT = 62 * 1024 * 1024

COL_Z = 0
COL_XBC = D_INNER
COL_GLU = D_XBC + D_INNER
COL_GATE = D_XBC + D_INNER + 2 * D_CONF
N_MAIN = D_XBC + D_INNER + 2 * D_CONF + 2 * D_MODEL


def _sigmoid(x):
    return 0.5 * jnp.tanh(0.5 * x) + 0.5


def _silu(x):
    return x * _sigmoid(x)


def _cparams(n_axes, vmem=None):
    return pltpu.CompilerParams(
        dimension_semantics=("arbitrary",) * n_axes,
        vmem_limit_bytes=vmem)


def _ada_kernel(c_ref, w_ref, b_ref, o_ref):
    s = _silu(c_ref[...])
    o_ref[...] = jnp.dot(s.astype(BF16), w_ref[...].astype(BF16),
                         preferred_element_type=F32) + b_ref[...]


def _ada_mod(cvec, w_ada, b_ada):
    n = w_ada.shape[1]
    tn = 1536
    return pl.pallas_call(
        _ada_kernel,
        grid=(n // tn,),
        in_specs=[pl.BlockSpec((8, D_MODEL), lambda j: (0, 0)),
                  pl.BlockSpec((D_MODEL, tn), lambda j: (0, j)),
                  pl.BlockSpec((1, tn), lambda j: (0, j))],
        out_specs=pl.BlockSpec((8, tn), lambda j: (0, j)),
        out_shape=jax.ShapeDtypeStruct((8, n), F32),
        compiler_params=_cparams(1, 40 * 1024 * 1024),
        name="ada_mod",
    )(cvec, w_ada, b_ada.reshape(1, n))


def _wprep_kernel(w_ref, wn_ref, oa_ref, ob_ref, odt_ref, *, a_tiles):
    n = pl.program_id(0)
    tn = w_ref.shape[0]
    shift = 2 * N_HEADS

    def put(o_ref, c, rows_t):
        for j in range(D_MODEL // LANES):
            o_ref[j * LANES:(j + 1) * LANES, c * LANES:(c + 1) * LANES] = \
                rows_t[:, j * LANES:(j + 1) * LANES].T.astype(BF16)

    @pl.when(n < a_tiles)
    def _():
        for c in range(tn // LANES):
            put(oa_ref, c, w_ref[c * LANES:(c + 1) * LANES, :])

    @pl.when(n >= a_tiles)
    def _():
        for c in range(tn // LANES):
            lo = w_ref[c * LANES + shift:(c + 1) * LANES, :]
            hi = w_ref[(c + 1) * LANES:(c + 1) * LANES + shift, :] if c + 1 < tn // LANES else wn_ref[0:shift, :]
            put(ob_ref, c, jnp.concatenate([lo, hi], axis=0))

    @pl.when(n == a_tiles)
    def _():
        zeros = jnp.zeros((LANES - N_HEADS, D_MODEL), F32)
        for d in range(2):
            put(odt_ref, d, jnp.concatenate([w_ref[d * N_HEADS:(d + 1) * N_HEADS, :], zeros], axis=0))


def _in_proj_weights(w_in):
    tn = 2048
    n_a = D_INNER + D_XBC
    n_b = 2 * D_CONF + 2 * D_MODEL
    a_tiles = n_a // tn
    assert n_a % tn == 0 and n_b % tn == 0 and 2 * N_HEADS < LANES
    n_tiles = a_tiles + n_b // tn
    assert w_in.shape[1] == n_tiles * tn + 2 * N_HEADS
    w_t = w_in.T
    return pl.pallas_call(
        functools.partial(_wprep_kernel, a_tiles=a_tiles),
        grid=(n_tiles,),
        in_specs=[pl.BlockSpec((tn, D_MODEL), lambda n: (n, 0)),
                  pl.BlockSpec((LANES, D_MODEL), lambda n: ((n + 1) * (tn // LANES), 0))],
        out_specs=[pl.BlockSpec((D_MODEL, tn), lambda n: (0, jnp.minimum(n, a_tiles - 1))),
                   pl.BlockSpec((D_MODEL, tn), lambda n: (0, jnp.maximum(n - a_tiles, 0))),
                   pl.BlockSpec((D_MODEL, 2 * LANES), lambda n: (0, 0))],
        out_shape=[jax.ShapeDtypeStruct((D_MODEL, n_a), BF16),
                   jax.ShapeDtypeStruct((D_MODEL, n_b), BF16),
                   jax.ShapeDtypeStruct((D_MODEL, 2 * LANES), BF16)],
        compiler_params=_cparams(1, 48 * 1024 * 1024),
        name="in_proj_weights",
    )(w_t, w_t)


def _inproj_kernel(xc_ref, xl_ref, nw_ref, sc_ref, sh_ref, wa_ref, wb_ref, wdt_ref, o_ref, dt_ref, h_scr, *,
                   ctx_tiles, a_tiles):
    @pl.when(pl.program_id(1) == 0)
    def _():
        xf = jnp.where(pl.program_id(0) < ctx_tiles, xc_ref[...], xl_ref[...])
        ms = jnp.mean(xf * xf, axis=-1, keepdims=True)
        hn = (xf * lax.rsqrt(ms + EPS)) * nw_ref[...]
        hn = hn * (1.0 + sc_ref[...]) + sh_ref[...]
        hb = hn.astype(BF16)
        h_scr[...] = hb
        dt_ref[...] = jnp.dot(hb, wdt_ref[...], preferred_element_type=F32)

    @pl.when(pl.program_id(1) < a_tiles)
    def _():
        o_ref[...] = jnp.dot(h_scr[...], wa_ref[...], preferred_element_type=F32).astype(BF16)

    @pl.when(pl.program_id(1) >= a_tiles)
    def _():
        o_ref[...] = jnp.dot(h_scr[...], wb_ref[...], preferred_element_type=F32).astype(BF16)


def _two_source_specs(tm, width, ctx_tiles, nargs):
    if nargs == 1:
        return [pl.BlockSpec((tm, width), lambda m: (jnp.minimum(m, ctx_tiles - 1), 0)),
                pl.BlockSpec((tm, width), lambda m: (jnp.maximum(m - ctx_tiles, 0), 0))]
    return [pl.BlockSpec((tm, width), lambda m, n: (jnp.minimum(m, ctx_tiles - 1), 0)),
            pl.BlockSpec((tm, width), lambda m, n: (jnp.maximum(m - ctx_tiles, 0), 0))]


def _in_proj(x_ctx, x_lat, norm_w, modr, w_a, w_b, w_dt):
    t_ctx = x_ctx.shape[0]
    t = t_ctx + x_lat.shape[0]
    tm, tn = 1024, 2048
    upt = tm // UNIT
    a_tiles = w_a.shape[1] // tn
    return pl.pallas_call(
        functools.partial(_inproj_kernel, ctx_tiles=t_ctx // tm, a_tiles=a_tiles),
        grid=(t // tm, N_MAIN // tn),
        in_specs=_two_source_specs(tm, D_MODEL, t_ctx // tm, 2) + [
                  pl.BlockSpec((1, D_MODEL), lambda m, n: (0, 0)),
                  pl.BlockSpec((None, None, 1, D_MODEL), lambda m, n: (m * upt, 1, 0, 0)),
                  pl.BlockSpec((None, None, 1, D_MODEL), lambda m, n: (m * upt, 0, 0, 0)),
                  pl.BlockSpec((D_MODEL, tn), lambda m, n: (0, jnp.minimum(n, a_tiles - 1))),
                  pl.BlockSpec((D_MODEL, tn), lambda m, n: (0, jnp.maximum(n - a_tiles, 0))),
                  pl.BlockSpec((D_MODEL, 2 * LANES), lambda m, n: (0, 0))],
        out_specs=[pl.BlockSpec((tm, tn), lambda m, n: (m, n)),
                   pl.BlockSpec((tm, 2 * LANES), lambda m, n: (m, 0))],
        out_shape=[jax.ShapeDtypeStruct((t, N_MAIN), BF16),
                   jax.ShapeDtypeStruct((t, 2 * LANES), F32)],
        scratch_shapes=[pltpu.VMEM((tm, D_MODEL), BF16)],
        compiler_params=_cparams(2, 48 * 1024 * 1024),
        name="in_proj",
    )(x_ctx, x_lat, norm_w.reshape(1, D_MODEL), modr, modr, w_a, w_b, w_dt)


def _ssm_conv_kernel(x_ref, w_ref, b_ref, o_ref, pad_scr, *, seq):
    cb = x_ref.shape[1]
    pad_scr[0:8, :] = jnp.zeros((8, cb), F32)
    pad_scr[8 + seq:16 + seq, :] = jnp.zeros((8, cb), F32)
    pad_scr[8:8 + seq, :] = x_ref[...].astype(F32)
    half = SSM_CONV // 2
    rows, cw = 256, 512
    for c0 in range(0, cb, cw):
        cs = slice(c0, c0 + cw)
        for r0 in range(0, seq, rows):
            win = pad_scr[r0:r0 + rows + 16, cs]
            acc = jnp.broadcast_to(b_ref[:, cs], (rows, cw))
            for k in range(SSM_CONV):
                tap = win if k == half else pltpu.roll(win, (half - k) % (rows + 16), 0)
                acc = acc + w_ref[k:k + 1, cs] * tap[8:8 + rows]
            o_ref[r0:r0 + rows, cs] = _silu(acc).astype(BF16)


def _ssm_conv(proj, conv_w, conv_b, *, seq, nseq, row0):
    cb = 2048
    blk0 = row0 // seq
    return pl.pallas_call(
        functools.partial(_ssm_conv_kernel, seq=seq),
        grid=(nseq, D_XBC // cb),
        in_specs=[pl.BlockSpec((seq, cb), lambda s, j: (blk0 + s, COL_XBC // cb + j)),
                  pl.BlockSpec((SSM_CONV, cb), lambda s, j: (0, j)),
                  pl.BlockSpec((1, cb), lambda s, j: (0, j))],
        out_specs=pl.BlockSpec((seq, cb), lambda s, j: (s, j)),
        out_shape=jax.ShapeDtypeStruct((nseq * seq, D_XBC), BF16),
        scratch_shapes=[pltpu.VMEM((seq + 16, cb), F32)],
        compiler_params=_cparams(2, 40 * 1024 * 1024),
        name="ssm_conv",
    )(proj, conv_w, conv_b.reshape(1, D_XBC))


def _split_bf16(v):
    hi = v.astype(BF16)
    lo = (v - hi.astype(F32)).astype(BF16)
    return jnp.concatenate([hi, lo], axis=1)


def _head_select_matrices():
    j = jnp.arange(2 * LANES, dtype=jnp.int32)[:, None] % LANES
    full = (j == (jnp.arange(N_HEADS * LANES, dtype=jnp.int32)[None, :] // LANES)).astype(BF16)
    exp = (j == (jnp.arange(D_INNER, dtype=jnp.int32)[None, :] // HEAD_DIM)).astype(BF16)
    return full, exp


def _ssd_kernel(*refs, nc, has_h0, write_state):
    (xs_ref, b_ref, c_ref, z_ref, dtr_ref, dtb_ref, alog_ref, dskip_ref, nw_ref, self_ref, sele_ref), rest = \
        refs[:11], refs[11:]
    if has_h0:
        h0_ref, rest = rest[0], rest[1:]
    y_ref, rest = rest[0], rest[1:]
    if write_state:
        st_ref, rest = rest[0], rest[1:]
    h_scr, ybuf, ychunk, colb_scr, wexp_scr, eexp_scr, texp_scr, rowq_scr = rest

    q = CHUNK
    phase = pl.program_id(1)
    c = pl.program_id(2)
    is_fwd = phase == 1
    c_eff = jnp.where(is_fwd, c, nc - 1 - c)

    @pl.when(c == 0)
    def _():
        if has_h0:
            for j in range(D_INNER // LANES):
                h_scr[:, j * LANES:(j + 1) * LANES] = h0_ref[j * LANES:(j + 1) * LANES, :].T
        else:
            h_scr[...] = jnp.zeros(h_scr.shape, F32)

    nrow = SSD_CHUNKS * q
    xdt = dtr_ref[...] + dtb_ref[...]
    dt = jnp.maximum(xdt, 0.0) + jnp.log(1.0 + jnp.exp(-jnp.abs(xdt)))
    a = -jnp.exp(alog_ref[...])
    adt = dt * a
    sgn = jnp.where(is_fwd, 1, -1)
    row = lax.broadcasted_iota(jnp.int32, (nrow, nrow), 0)
    col = lax.broadcasted_iota(jnp.int32, (nrow, nrow), 1)
    tri_blk = jnp.logical_and(row // q == col // q, (col - row) * sgn <= 0)
    cum2 = jnp.dot(tri_blk.astype(BF16), _split_bf16(adt), preferred_element_type=F32)
    cum = cum2[:, :LANES] + cum2[:, LANES:]
    tots = [jnp.where(is_fwd, cum[k * q + q - 1:k * q + q, :], cum[k * q:k * q + 1, :]) for k in range(SSD_CHUNKS)]
    tot_rows = jnp.concatenate([jnp.broadcast_to(t, (q, LANES)) for t in tots], axis=0)
    rowq = (cum - jnp.log(dt)).T
    for k in range(SSD_CHUNKS):
        rowq_scr[k] = rowq[:, k * q:(k + 1) * q]
    wdec = dt * jnp.exp(tot_rows - cum)
    eo = jnp.exp(cum)
    etot = jnp.concatenate([jnp.broadcast_to(jnp.exp(t), (8, LANES)) for t in tots], axis=0)
    colb_scr[...] = jnp.dot(_split_bf16(cum), self_ref[...], preferred_element_type=F32)
    expanded = jnp.dot(jnp.concatenate([_split_bf16(wdec), _split_bf16(eo), _split_bf16(etot)], axis=0),
                       sele_ref[...], preferred_element_type=F32)
    wexp_scr[...] = expanded[0:nrow]
    eexp_scr[...] = expanded[nrow:2 * nrow]
    for k in range(SSD_CHUNKS):
        texp_scr[k] = expanded[2 * nrow + 8 * k:2 * nrow + 8 * (k + 1)]
    lrow = lax.broadcasted_iota(jnp.int32, (q, q), 0)
    lcol = lax.broadcasted_iota(jnp.int32, (q, q), 1)
    tri = (lcol - lrow) * sgn <= 0
    lane_g = lax.broadcasted_iota(jnp.int32, (1, GROUP_W), 1)
    head_mask = [(lane_g // HEAD_DIM == hh).astype(BF16) for hh in range(HEADS_PER_GROUP)]
    neg_inf = jnp.float32(-jnp.inf)

    def group_body(g, carry):
        gs = pl.ds(pl.multiple_of(g * GROUP_W, GROUP_W), GROUP_W)
        ns = pl.ds(pl.multiple_of(g * D_STATE, D_STATE), D_STATE)
        for k in range(SSD_CHUNKS):
            ci = jnp.where(is_fwd, k, SSD_CHUNKS - 1 - k)
            r0 = pl.multiple_of(ci * q, q)
            rs = pl.ds(r0, q)
            bg = b_ref[rs, ns]
            cg = c_ref[rs, ns]
            scores = lax.dot_general(cg, bg, (((1,), (1,)), ((), ())), preferred_element_type=F32)
            xs_g = xs_ref[rs, gs]
            ms, xb = [], []
            for hh in range(HEADS_PER_GROUP):
                h = HEADS_PER_GROUP * g + hh
                colb = colb_scr[rs, pl.ds(pl.multiple_of(h * LANES, LANES), LANES)]
                seg = jnp.where(tri, colb - rowq_scr[ci, pl.ds(h, 1), :], neg_inf)
                ms.append((scores * jnp.exp(seg)).astype(BF16))
                xb.append(xs_g * head_mask[hh])
            y_diag = jnp.dot(jnp.concatenate(ms, axis=1), jnp.concatenate(xb, axis=0),
                             preferred_element_type=F32)
            xd = (xs_g.astype(F32) * wexp_scr[rs, gs]).astype(BF16)
            bt = bg.astype(F32).T.astype(BF16)
            st = jnp.dot(bt, xd, preferred_element_type=F32)
            hg = h_scr[:, gs]
            y_off = jnp.dot(cg, hg.astype(BF16), preferred_element_type=F32) * eexp_scr[rs, gs]
            ychunk[rs, gs] = y_diag + y_off
            h_scr[:, gs] = hg * texp_scr[ci, 0:1, gs] + st
        return carry

    lax.fori_loop(0, N_GROUPS, group_body, 0, unroll=SSD_GROUP_UNROLL)

    rows = pl.ds(pl.multiple_of(c_eff * nrow, nrow), nrow)

    @pl.when(jnp.logical_not(is_fwd))
    def _():
        ybuf[rows, :] = ychunk[...]

    @pl.when(is_fwd)
    def _():
        zf = z_ref[...].astype(F32)
        yt = ychunk[...] + ybuf[rows, :] + xs_ref[...].astype(F32) * dskip_ref[...]
        yz = yt * _silu(zf)
        for g in range(N_GROUPS):
            gs = slice(g * GROUP_W, (g + 1) * GROUP_W)
            blk = yz[:, gs]
            ms = jnp.mean(blk * blk, axis=-1, keepdims=True)
            y_ref[:, gs] = (blk * lax.rsqrt(ms + EPS) * nw_ref[:, gs]).astype(BF16)

    if write_state:
        @pl.when(c == nc - 1)
        def _():
            for j in range(D_INNER // LANES):
                st_ref[j * LANES:(j + 1) * LANES, :] = h_scr[:, j * LANES:(j + 1) * LANES].T


def _ssd(xbc_c, proj, dt_raw, dt_bias, a_log, d_skip, norm_w, h0t, *, seq, nseq, row0, write_state):
    rb = SSD_CHUNKS * CHUNK
    nc = seq // rb
    blk0 = row0 // rb
    has_h0 = h0t is not None
    assert seq % rb == 0 and row0 % rb == 0

    def tok(s, p, c):
        return s * nc + p * c + (1 - p) * (nc - 1 - c)

    in_specs = [
        pl.BlockSpec((rb, D_INNER), lambda s, p, c: (tok(s, p, c), 0)),
        pl.BlockSpec((rb, N_GROUPS * D_STATE), lambda s, p, c: (tok(s, p, c), 2)),
        pl.BlockSpec((rb, N_GROUPS * D_STATE), lambda s, p, c: (tok(s, p, c), 3)),
        pl.BlockSpec((rb, D_INNER), lambda s, p, c: (blk0 + s * nc + p * c, COL_Z // D_INNER)),
        pl.BlockSpec((rb, LANES), lambda s, p, c: (blk0 + tok(s, p, c), 1 - p)),
        pl.BlockSpec((None, 1, LANES), lambda s, p, c: (1 - p, 0, 0)),
        pl.BlockSpec((None, 1, LANES), lambda s, p, c: (1 - p, 0, 0)),
        pl.BlockSpec((1, D_INNER), lambda s, p, c: (0, 0)),
        pl.BlockSpec((1, D_INNER), lambda s, p, c: (0, 0)),
        pl.BlockSpec((2 * LANES, N_HEADS * LANES), lambda s, p, c: (0, 0)),
        pl.BlockSpec((2 * LANES, D_INNER), lambda s, p, c: (0, 0)),
    ]
    sel_full, sel_exp = _head_select_matrices()
    args = [xbc_c, xbc_c, xbc_c, proj, dt_raw, dt_bias, a_log, d_skip, norm_w, sel_full, sel_exp]
    if has_h0:
        in_specs.append(pl.BlockSpec((None, None, D_INNER, D_STATE), lambda s, p, c: (s, 1 - p, 0, 0)))
        args.append(h0t)
    out_specs = [pl.BlockSpec((rb, D_INNER), lambda s, p, c: (s * nc + p * c, 0))]
    out_shape = [jax.ShapeDtypeStruct((nseq * seq, D_INNER), BF16)]
    if write_state:
        out_specs.append(pl.BlockSpec((None, None, D_INNER, D_STATE), lambda s, p, c: (s, 1 - p, 0, 0)))
        out_shape.append(jax.ShapeDtypeStruct((nseq, 2, D_INNER, D_STATE), F32))
    return pl.pallas_call(
        functools.partial(_ssd_kernel, nc=nc, has_h0=has_h0, write_state=write_state),
        grid=(nseq, 2, nc),
        in_specs=in_specs,
        out_specs=out_specs,
        out_shape=out_shape,
        scratch_shapes=[pltpu.VMEM((D_STATE, D_INNER), F32),
                        pltpu.VMEM((seq, D_INNER), F32),
                        pltpu.VMEM((rb, D_INNER), F32),
                        pltpu.VMEM((rb, N_HEADS * LANES), F32),
                        pltpu.VMEM((rb, D_INNER), F32),
                        pltpu.VMEM((rb, D_INNER), F32),
                        pltpu.VMEM((SSD_CHUNKS, 8, D_INNER), F32),
                        pltpu.VMEM((SSD_CHUNKS, LANES, CHUNK), F32)],
        compiler_params=_cparams(3, 48 * 1024 * 1024),
        name="ssd_scan",
    )(*args)


def _conf_kernel(glu_ref, w_ref, b_ref, lnw_ref, lnb_ref, o_ref, pad_scr, sh_scr, conv_scr, *, seg):
    rows = glu_ref.shape[0]
    nseg = rows // seg
    half = CONF_KERNEL // 2
    front = 16
    span = seg + 24
    a = glu_ref[:, :D_CONF].astype(F32)
    b = glu_ref[:, D_CONF:].astype(F32)
    u = a * _sigmoid(b)
    for i in range(nseg):
        pad_scr[i, 0:front, :] = jnp.zeros((front, D_CONF), F32)
        pad_scr[i, front + seg:front + seg + 16, :] = jnp.zeros((16, D_CONF), F32)
        pad_scr[i, front:front + seg, :] = u[i * seg:(i + 1) * seg, :]
    padded = seg + 32
    for i in range(nseg):
        for cbi in range(D_CONF // LANES):
            cs = slice(cbi * LANES, (cbi + 1) * LANES)
            seg_pad = pad_scr[i, :, cs]
            for s in range(8):
                sh_scr[s, i, :, cs] = (seg_pad if s == 0 else pltpu.roll(seg_pad, padded - s, 0))[0:span]
    rb = 64
    for i in range(nseg):
        for cbi in range(D_CONF // LANES):
            cs = slice(cbi * LANES, (cbi + 1) * LANES)
            for r0 in range(0, seg, rb):
                acc = jnp.broadcast_to(b_ref[:, cs], (rb, LANES))
                for k in range(CONF_KERNEL):
                    start = front + r0 + k - half
                    al = start - start % 8
                    acc = acc + w_ref[k:k + 1, cs] * sh_scr[start % 8, i, al:al + rb, cs]
                conv_scr[i * seg + r0:i * seg + r0 + rb, cs] = acc
    v = conv_scr[...]
    mu = jnp.mean(v, axis=-1, keepdims=True)
    vc = v - mu
    var = jnp.mean(vc * vc, axis=-1, keepdims=True)
    ln = (vc * lax.rsqrt(var + EPS)) * lnw_ref[...] + lnb_ref[...]
    o_ref[...] = _silu(ln).astype(BF16)


def _conformer(proj, dw_w, dw_b, ln_w, ln_b, *, seg, ntok, row0):
    rows = UNIT
    blk0 = row0 // rows
    return pl.pallas_call(
        functools.partial(_conf_kernel, seg=seg),
        grid=(ntok // rows,),
        in_specs=[pl.BlockSpec((rows, 2 * D_CONF), lambda i: (blk0 + i, COL_GLU // (2 * D_CONF))),
                  pl.BlockSpec((CONF_KERNEL, D_CONF), lambda i: (0, 0)),
                  pl.BlockSpec((1, D_CONF), lambda i: (0, 0)),
                  pl.BlockSpec((1, D_CONF), lambda i: (0, 0)),
                  pl.BlockSpec((1, D_CONF), lambda i: (0, 0))],
        out_specs=pl.BlockSpec((rows, D_CONF), lambda i: (i, 0)),
        out_shape=jax.ShapeDtypeStruct((ntok, D_CONF), BF16),
        scratch_shapes=[pltpu.VMEM((rows // seg, seg + 32, D_CONF), F32),
                        pltpu.VMEM((8, rows // seg, seg + 24, D_CONF), F32),
                        pltpu.VMEM((rows, D_CONF), F32)],
        compiler_params=_cparams(1, 40 * 1024 * 1024),
        name="conformer_conv",
    )(proj, dw_w, dw_b.reshape(1, D_CONF), ln_w.reshape(1, D_CONF), ln_b.reshape(1, D_CONF))


def _pack_halves(x):
    outs = []
    for cb in range(x.shape[1] // (2 * LANES)):
        hi = x[:, cb * 2 * LANES:cb * 2 * LANES + LANES].astype(BF16).astype(F32)
        lo = x[:, cb * 2 * LANES + LANES:(cb + 1) * 2 * LANES].astype(BF16).astype(F32)
        hw = lax.bitcast_convert_type(hi, U32)
        lw = jnp.right_shift(lax.bitcast_convert_type(lo, U32), jnp.uint32(16))
        outs.append(jnp.bitwise_or(hw, lw))
    return jnp.concatenate(outs, axis=1)


def _mix_kernel(xc_ref, xl_ref, yc_ref, yl_ref, uc_ref, ul_ref, gate_ref, g1_ref, sc2_ref, sh2_ref,
                wssd_ref, wconf_ref, bconf_ref, bgate_ref, wo_ref, nffn_ref, wr_ref, br_ref,
                x1_ref, h2i_ref, er_ref, topp_ref, cnt_ref, cnt_scr, *, ctx_tiles):
    @pl.when(pl.program_id(0) == 0)
    def _():
        cnt_scr[...] = jnp.zeros(cnt_scr.shape, F32)

    is_ctx = pl.program_id(0) < ctx_tiles
    rows = x1_ref.shape[0]
    sub = rows // MIX_CHAINS
    lane = lax.broadcasted_iota(jnp.int32, (sub, LANES), 1)
    lane_f = lane.astype(F32)
    neg_inf = jnp.float32(-jnp.inf)

    def chain(c):
        rs = slice(c * sub, (c + 1) * sub)
        y_in = jnp.where(is_ctx, yc_ref[rs, :], yl_ref[rs, :])
        u_in = jnp.where(is_ctx, uc_ref[rs, :], ul_ref[rs, :])
        o_ssd = jnp.dot(y_in, wssd_ref[...], preferred_element_type=F32)
        o_conf = jnp.dot(u_in, wconf_ref[...], preferred_element_type=F32) + bconf_ref[...]
        gates = _sigmoid(gate_ref[rs, :].astype(F32) + bgate_ref[...])
        merged = gates[:, :D_MODEL] * o_ssd + gates[:, D_MODEL:] * o_conf
        out = jnp.dot(merged.astype(BF16), wo_ref[...], preferred_element_type=F32)
        x1 = jnp.where(is_ctx, xc_ref[rs, :], xl_ref[rs, :]) + g1_ref[...] * out
        x1_ref[rs, :] = x1
        ms = jnp.mean(x1 * x1, axis=-1, keepdims=True)
        h2 = (x1 * lax.rsqrt(ms + EPS)) * nffn_ref[...]
        h2 = h2 * (1.0 + sc2_ref[...]) + sh2_ref[...]
        packed = _pack_halves(h2)
        for qd in range(XW):
            h2i_ref[pl.ds(c * sub * XW + qd, sub, stride=XW), :] = packed[:, qd * LANES:(qd + 1) * LANES]
        h_hi = h2.astype(BF16)
        h_lo = (h2 - h_hi.astype(F32)).astype(BF16)
        logits = jnp.dot(jnp.concatenate([h_hi, h_hi, h_lo], axis=1), wr_ref[...],
                         preferred_element_type=F32) + br_ref[...]
        work = jnp.where(lane < N_EXPERTS, logits, neg_inf)
        vals, idxs = [], []
        for _ in range(TOP_K):
            m = jnp.max(work, axis=-1, keepdims=True)
            idx = jnp.min(jnp.where(work == m, lane_f, jnp.float32(LANES)), axis=-1, keepdims=True)
            vals.append(m)
            idxs.append(idx)
            work = jnp.where(lane_f == idx, neg_inf, work)
        es = [jnp.exp(v - vals[0]) for v in vals]
        denom = es[0] + es[1] + es[2] + es[3]
        member = jnp.zeros((sub, LANES), F32)
        topp = jnp.zeros((sub, LANES), F32)
        for k in range(TOP_K):
            member = member + jnp.where(lane_f == idxs[k], 1.0, 0.0)
            topp = jnp.where(lane == k, es[k] / denom, topp)
        topp_ref[rs, :] = topp
        return idxs, member

    results = [chain(c) for c in range(MIX_CHAINS)]
    member = jnp.concatenate([m for _, m in results], axis=0)
    r_i = lax.broadcasted_iota(jnp.int32, (rows, rows), 0)
    c_i = lax.broadcasted_iota(jnp.int32, (rows, rows), 1)
    earlier = jnp.where(c_i < r_i, 1.0, 0.0).astype(BF16)
    rank = jnp.dot(earlier, member.astype(BF16), preferred_element_type=F32) + cnt_scr[...]
    cnt = cnt_scr[...] + jnp.sum(member, axis=0, keepdims=True)
    cnt_scr[...] = cnt
    cnt_ref[...] = jnp.broadcast_to(cnt, cnt_ref.shape).astype(jnp.int32)
    for c, (idxs, _) in enumerate(results):
        rs = slice(c * sub, (c + 1) * sub)
        er = jnp.zeros((sub, LANES), F32)
        for k in range(TOP_K):
            rank_k = jnp.sum(jnp.where(lane_f == idxs[k], rank[rs, :], 0.0), axis=-1, keepdims=True)
            er = jnp.where(lane == k, idxs[k] + N_EXPERTS * rank_k, er)
        er_ref[rs, :] = er.astype(jnp.int32)


def _mix_out(x_ctx, x_lat, y_ctx, y_lat, u_ctx, u_lat, proj, modr, w_ssd, w_conf, b_conf, b_gate, w_o, norm_ffn,
             w_router3, b_router):
    t_ctx = x_ctx.shape[0]
    t = t_ctx + x_lat.shape[0]
    tm = 512
    upt = tm // UNIT
    ctx_tiles = t_ctx // tm
    full = lambda shape: pl.BlockSpec(shape, lambda m: (0,) * len(shape), pipeline_mode=pl.Buffered(1))
    mod = lambda which: pl.BlockSpec((None, None, 1, D_MODEL), lambda m: (m * upt, which, 0, 0))
    return pl.pallas_call(
        functools.partial(_mix_kernel, ctx_tiles=ctx_tiles),
        grid=(t // tm,),
        in_specs=_two_source_specs(tm, D_MODEL, ctx_tiles, 1) + _two_source_specs(tm, D_INNER, ctx_tiles, 1)
                 + _two_source_specs(tm, D_CONF, ctx_tiles, 1) + [
                  pl.BlockSpec((tm, 2 * D_MODEL), lambda m: (m, COL_GATE // (2 * D_MODEL))),
                  mod(2), mod(4), mod(3),
                  full((D_INNER, D_MODEL)), full((D_CONF, D_MODEL)), full((1, D_MODEL)),
                  full((1, 2 * D_MODEL)), full((D_MODEL, D_MODEL)), full((1, D_MODEL)),
                  full((3 * D_MODEL, LANES)), full((1, LANES))],
        out_specs=[pl.BlockSpec((tm, D_MODEL), lambda m: (m, 0)),
                   pl.BlockSpec((tm * XW, LANES), lambda m: (m, 0)),
                   pl.BlockSpec((tm, LANES), lambda m: (m, 0)),
                   pl.BlockSpec((tm, LANES), lambda m: (m, 0)),
                   pl.BlockSpec((8, LANES), lambda m: (0, 0))],
        out_shape=[jax.ShapeDtypeStruct((t, D_MODEL), F32),
                   jax.ShapeDtypeStruct((t * XW, LANES), U32),
                   jax.ShapeDtypeStruct((t, LANES), jnp.int32),
                   jax.ShapeDtypeStruct((t, LANES), F32),
                   jax.ShapeDtypeStruct((8, LANES), jnp.int32)],
        scratch_shapes=[pltpu.VMEM((1, LANES), F32)],
        compiler_params=_cparams(1, 56 * 1024 * 1024),
        name="mix_out_router",
    )(x_ctx, x_lat, y_ctx, y_lat, u_ctx, u_lat, proj, modr, modr, modr, w_ssd, w_conf, b_conf.reshape(1, D_MODEL),
      b_gate.reshape(1, 2 * D_MODEL), w_o, norm_ffn.reshape(1, D_MODEL), w_router3, b_router)


def _moe_kernel(off_ref, pair_ref, pb_ref, h2i_ref, wgu_hbm, wd_hbm, bgu_ref, bd_ref, out_ref,
                acc, x_scr, y_scr, stage_gu, stage_d, wgu_b, wd_b, wsem, osem):
    e = pl.program_id(0)
    tr = MOE_TILE

    def weight_copies(ex):
        return (pltpu.make_async_copy(wgu_hbm.at[ex], stage_gu, wsem.at[0]),
                pltpu.make_async_copy(wd_hbm.at[ex], stage_d, wsem.at[1]))

    @pl.when(e == 0)
    def _():
        acc[...] = jnp.zeros(acc.shape, F32)
        for cp in weight_copies(0):
            cp.start()

    for cp in weight_copies(e):
        cp.wait()

    start = off_ref[e]
    end = off_ref[e + 1]
    ntiles = (end - start + tr - 1) // tr
    bgu = bgu_ref[...]
    bd = bd_ref[...]
    cast_steps = D_MODEL // MOE_CAST_ROWS
    rows_per_step = tr // cast_steps

    def to_bf16(i, c):
        rows = pl.ds(pl.multiple_of(i * MOE_CAST_ROWS, MOE_CAST_ROWS), MOE_CAST_ROWS)
        wgu_b[rows, :] = stage_gu[rows, :].astype(BF16)
        wd_b[rows, :] = stage_d[rows, :].astype(BF16)
        for rr in range(rows_per_step):
            r = i * rows_per_step + rr
            src = jnp.bitwise_and(pair_ref[start + r], -TOP_K)
            x_scr[0, pl.ds(pl.multiple_of(r * XW, XW), XW), :] = h2i_ref[pl.ds(pl.multiple_of(src, XW), XW), :]
        return c

    lax.fori_loop(0, cast_steps, to_bf16, 0, unroll=2)

    @pl.when(e + 1 < N_EXPERTS)
    def _():
        for cp in weight_copies(e + 1):
            cp.start()

    def gather_rows(base, slot, r0, n):
        for r in range(r0, r0 + n):
            src = jnp.bitwise_and(pair_ref[base + r], -TOP_K)
            x_scr[slot, pl.ds(r * XW, XW), :] = h2i_ref[pl.ds(pl.multiple_of(src, XW), XW), :]

    def scatter_rows(base, slot, r0):
        dsts, ps = [], []
        for rr in range(8):
            pair = pair_ref[base + r0 + rr]
            dsts.append(pl.multiple_of(jnp.bitwise_and(pair, -TOP_K) * (AW // XW), AW))
            ps.append(lax.bitcast_convert_type(pb_ref[pair], F32))
        olds = [acc[pl.ds(dsts[rr], AW), :] for rr in range(8)]
        news = [olds[rr] + ps[rr] * y_scr[slot, pl.ds((r0 + rr) * AW, AW), :] for rr in range(8)]
        for rr in range(8):
            acc[pl.ds(dsts[rr], AW), :] = news[rr]

    def tile_step(ti, first):
        slot = jnp.bitwise_and(ti, 1)
        other = 1 - slot
        base = start + ti * tr
        parts = []
        for qd in range(XW):
            w = x_scr[slot, pl.ds(qd, tr, stride=XW), :]
            parts.append(lax.bitcast_convert_type(jnp.bitwise_and(w, jnp.uint32(0xFFFF0000)), F32))
            parts.append(lax.bitcast_convert_type(jnp.left_shift(w, jnp.uint32(16)), F32))
        x = jnp.concatenate(parts, axis=1).astype(BF16)
        if not first:
            for r0 in range(0, tr, 8):
                scatter_rows(base - tr, other, r0)
        gather_rows(base + tr, other, 0, tr)
        gu = jnp.dot(x, wgu_b[...], preferred_element_type=F32) + bgu
        g = jnp.minimum(gu[:, :D_EXPERT], SWIGLU_LIMIT)
        u = jnp.clip(gu[:, D_EXPERT:], -SWIGLU_LIMIT, SWIGLU_LIMIT)
        act = (u + 1.0) * g * _sigmoid(SWIGLU_ALPHA * g)
        y = jnp.dot(act.astype(BF16), wd_b[...], preferred_element_type=F32) + bd
        for qd in range(AW):
            y_scr[slot, pl.ds(qd, tr, stride=AW), :] = y[:, qd * LANES:(qd + 1) * LANES]

    @pl.when(ntiles > 0)
    def _():
        tile_step(0, True)

    def tile_body(ti, carry):
        tile_step(ti, False)
        return carry

    lax.fori_loop(1, ntiles, tile_body, 0)

    @pl.when(ntiles > 0)
    def _():
        last = ntiles - 1
        lslot = jnp.bitwise_and(last, 1)
        lbase = start + last * tr

        def last_scatter(r8, c2):
            dsts, ps = [], []
            for rr in range(8):
                r = lbase + r8 * 8 + rr
                pair = pair_ref[r]
                dsts.append(pl.multiple_of(jnp.bitwise_and(pair, -TOP_K) * (AW // XW), AW))
                pw = lax.bitcast_convert_type(pb_ref[pair], F32)
                ps.append(jnp.where(r < end, pw, 0.0))
            olds = [acc[pl.ds(dsts[rr], AW), :] for rr in range(8)]
            news = [olds[rr] + ps[rr] * y_scr[lslot, pl.ds(pl.multiple_of((r8 * 8 + rr) * AW, AW), AW), :]
                    for rr in range(8)]
            for rr in range(8):
                acc[pl.ds(dsts[rr], AW), :] = news[rr]
            return c2

        lax.fori_loop(0, tr // 8, last_scatter, 0)

    @pl.when(e == N_EXPERTS - 1)
    def _():
        cp = pltpu.make_async_copy(acc, out_ref, osem)
        cp.start()
        cp.wait()


def _moe_grouped(h2i, off, row_pair, pbits, w_gu, b_gu, w_down, b_down):
    t = h2i.shape[0] // XW
    assert D_EXPERT == D_MODEL
    grid_spec = pltpu.PrefetchScalarGridSpec(
        num_scalar_prefetch=3,
        grid=(N_EXPERTS,),
        in_specs=[pl.BlockSpec((t * XW, LANES), lambda e, *_: (0, 0), pipeline_mode=pl.Buffered(1)),
                  pl.BlockSpec(memory_space=pl.ANY),
                  pl.BlockSpec(memory_space=pl.ANY),
                  pl.BlockSpec((None, 1, 2 * D_EXPERT), lambda e, *_: (e, 0, 0)),
                  pl.BlockSpec((None, 1, D_MODEL), lambda e, *_: (e, 0, 0))],
        out_specs=pl.BlockSpec(memory_space=pl.ANY),
        scratch_shapes=[pltpu.VMEM((t * AW, LANES), F32),
                        pltpu.VMEM((2, MOE_TILE * XW, LANES), U32),
                        pltpu.VMEM((2, MOE_TILE * AW, LANES), F32),
                        pltpu.VMEM((D_MODEL, 2 * D_EXPERT), F32),
                        pltpu.VMEM((D_EXPERT, D_MODEL), F32),
                        pltpu.VMEM((D_MODEL, 2 * D_EXPERT), BF16),
                        pltpu.VMEM((D_EXPERT, D_MODEL), BF16),
                        pltpu.SemaphoreType.DMA((2,)),
                        pltpu.SemaphoreType.DMA(())])
    return pl.pallas_call(
        _moe_kernel,
        grid_spec=grid_spec,
        out_shape=jax.ShapeDtypeStruct((t * AW, LANES), F32),
        compiler_params=_cparams(1, VMEM_LIMIT),
        name="moe_grouped",
    )(off, row_pair, pbits, h2i, w_gu, w_down, b_gu.reshape(N_EXPERTS, 1, 2 * D_EXPERT),
      b_down.reshape(N_EXPERTS, 1, D_MODEL))


def _residual_kernel(x1_ref, moe_ref, g2_ref, nw_ref, oc_ref, ol_ref, *, normalize, ctx_tiles):
    tm = x1_ref.shape[0]
    moe = jnp.concatenate([moe_ref[pl.ds(qd, tm, stride=AW), :] for qd in range(AW)], axis=1)
    x2 = x1_ref[...] + g2_ref[...] * moe
    if normalize:
        ms = jnp.mean(x2 * x2, axis=-1, keepdims=True)
        x2 = (x2 * lax.rsqrt(ms + EPS)) * nw_ref[...]

    @pl.when(pl.program_id(0) < ctx_tiles)
    def _():
        oc_ref[...] = x2

    @pl.when(pl.program_id(0) >= ctx_tiles)
    def _():
        ol_ref[...] = x2


def _residual(x1, moe_i, modr, norm_w, *, normalize, t_ctx):
    t = x1.shape[0]
    tm = 512
    upt = tm // UNIT
    ctx_tiles = t_ctx // tm
    return pl.pallas_call(
        functools.partial(_residual_kernel, normalize=normalize, ctx_tiles=ctx_tiles),
        grid=(t // tm,),
        in_specs=[pl.BlockSpec((tm, D_MODEL), lambda m: (m, 0)),
                  pl.BlockSpec((tm * AW, LANES), lambda m: (m, 0)),
                  pl.BlockSpec((None, None, 1, D_MODEL), lambda m: (m * upt, 5, 0, 0)),
                  pl.BlockSpec((1, D_MODEL), lambda m: (0, 0))],
        out_specs=_two_source_specs(tm, D_MODEL, ctx_tiles, 1),
        out_shape=[jax.ShapeDtypeStruct((t_ctx, D_MODEL), F32),
                   jax.ShapeDtypeStruct((t - t_ctx, D_MODEL), F32)],
        compiler_params=_cparams(1),
        name="ffn_residual_norm",
    )(x1, moe_i, modr, norm_w.reshape(1, D_MODEL))


def _route_kernel(pos_ref, pair_ref, *, n_pairs):
    def pad(i, c):
        pair_ref[n_pairs + i] = 0
        return c

    lax.fori_loop(0, ROUTE_PAD, pad, 0)

    def place(i8, c):
        for ii in range(8):
            i = i8 * 8 + ii
            pair_ref[pos_ref[i]] = i
        return c

    lax.fori_loop(0, n_pairs // 8, place, 0)


def _route(er_flat, cnt):
    n = er_flat.shape[0]
    off_end = jnp.cumsum(cnt)
    off = off_end - cnt
    expert = jnp.bitwise_and(er_flat, N_EXPERTS - 1)
    onehot = expert[:, None] == jnp.arange(N_EXPERTS, dtype=jnp.int32)[None, :]
    pos = lax.shift_right_logical(er_flat, RANK_SHIFT) + jnp.sum(jnp.where(onehot, off[None, :], 0), axis=1)
    smem = pl.BlockSpec(memory_space=pltpu.SMEM)
    row_pair = pl.pallas_call(
        functools.partial(_route_kernel, n_pairs=n),
        in_specs=[smem],
        out_specs=smem,
        out_shape=jax.ShapeDtypeStruct((n + ROUTE_PAD,), jnp.int32),
        name="moe_route",
    )(pos)
    return jnp.concatenate([off, jnp.broadcast_to(off_end[-1:], (8,))]), row_pair


def kernel(x_prompt, x_sample, state_ssm, c, c_ctx, w_ada, b_ada, norm_mix, norm_ffn, w_in, ssm_conv_w, ssm_conv_b, dt_bias, a_log, d_skip, ssm_norm_w, w_ssd_out, conf_dw_w, conf_dw_b, conf_ln_w, conf_ln_b, w_conf_out, b_conf_out, b_gate, w_o, w_router, b_router, w_gu, b_gu, w_down, b_down, norm_final):
    n_ctx, seq_ctx, _ = x_prompt.shape
    n_lat, seq_lat, _ = x_sample.shape
    depth = w_in.shape[0]
    t_ctx, t_lat = n_ctx * seq_ctx, n_lat * seq_lat
    t_all = t_ctx + t_lat
    assert seq_ctx == UNIT and seq_lat % 1024 == 0 and t_ctx % 1024 == 0
    assert n_lat + 1 <= 8 and seq_lat % GRID_W == 0 and UNIT % GRID_W == 0

    x_ctx, x_lat = x_prompt.reshape(t_ctx, D_MODEL), x_sample.reshape(t_lat, D_MODEL)
    cvec =jnp.concatenate([c_ctx[None, :], c, jnp.zeros((8 - 1 - n_lat, D_MODEL), F32)], axis=0)
    unit_row = jnp.concatenate([jnp.zeros((t_ctx // UNIT,), jnp.int32),
                                1 + jnp.arange(t_lat // UNIT, dtype=jnp.int32) // (seq_lat // UNIT)])
    assert XW == TOP_K and N_EXPERTS == 1 << RANK_SHIFT

    new_states = []
    for l in range(depth):
        mod = _ada_mod(cvec, w_ada[l], b_ada[l])
        modr = mod[unit_row].reshape(t_all // UNIT, 6, 1, D_MODEL)

        w_a, w_b, w_dt = _in_proj_weights(w_in[l])
        proj, dt_raw = _in_proj(x_ctx, x_lat, norm_mix[l], modr, w_a, w_b, w_dt)

        pad_h = lambda v: jnp.pad(v.reshape(2, 1, N_HEADS), ((0, 0), (0, 0), (0, LANES - N_HEADS)))
        dtb, alog = pad_h(dt_bias[l]), pad_h(a_log[l])
        dskip = jnp.repeat(d_skip[l], HEAD_DIM).reshape(1, D_INNER)
        nw = ssm_norm_w[l].reshape(1, D_INNER)
        h0t = state_ssm[:, l].astype(F32).reshape(n_lat, 2, D_INNER, D_STATE)

        xbc_ctx = _ssm_conv(proj, ssm_conv_w[l], ssm_conv_b[l], seq=seq_ctx, nseq=n_ctx, row0=0)
        xbc_lat = _ssm_conv(proj, ssm_conv_w[l], ssm_conv_b[l], seq=seq_lat, nseq=n_lat, row0=t_ctx)
        y_ctx, st_ctx = _ssd(xbc_ctx, proj, dt_raw, dtb, alog, dskip, nw, None,
                             seq=seq_ctx, nseq=n_ctx, row0=0, write_state=True)
        (y_lat,) = _ssd(xbc_lat, proj, dt_raw, dtb, alog, dskip, nw, h0t,
                        seq=seq_lat, nseq=n_lat, row0=t_ctx, write_state=False)
        u_ctx = _conformer(proj, conf_dw_w[l], conf_dw_b[l], conf_ln_w[l], conf_ln_b[l],
                           seg=seq_ctx, ntok=t_ctx, row0=0)
        u_lat = _conformer(proj, conf_dw_w[l], conf_dw_b[l], conf_ln_w[l], conf_ln_b[l],
                           seg=GRID_W, ntok=t_lat, row0=t_ctx)
        wr = jnp.pad(w_router[l], ((0, 0), (0, LANES - N_EXPERTS)))
        wr_hi = wr.astype(BF16)
        wr_lo = (wr - wr_hi.astype(F32)).astype(BF16)
        br = jnp.pad(b_router[l], (0, LANES - N_EXPERTS)).reshape(1, LANES)
        x1, h2i, er, topp, cnt = _mix_out(
            x_ctx, x_lat, y_ctx, y_lat, u_ctx, u_lat, proj, modr, w_ssd_out[l].astype(BF16),
            w_conf_out[l].astype(BF16), b_conf_out[l], b_gate[l], w_o[l].astype(BF16), norm_ffn[l],
            jnp.concatenate([wr_hi, wr_lo, wr_hi], axis=0), br)

        off, row_pair = _route(er[:, :TOP_K].reshape(-1), cnt[0, :N_EXPERTS])
        pbits = lax.bitcast_convert_type(topp[:, :TOP_K], jnp.int32).reshape(-1)
        moe_i = _moe_grouped(h2i, off, row_pair, pbits, w_gu[l], b_gu[l], w_down[l], b_down[l])
        last = l + 1 == depth
        x_ctx, x_lat = _residual(x1, moe_i, modr, norm_final if last else norm_ffn[l], normalize=last, t_ctx=t_ctx)
        new_states.append(st_ctx.reshape(n_ctx, 1, 2, N_HEADS, HEAD_DIM, D_STATE))

    y_prompt = x_ctx.reshape(n_ctx, seq_ctx, D_MODEL)
    y_sample = x_lat.reshape(n_lat, seq_lat, D_MODEL)
    new_state_ssm = new_states[0] if depth == 1 else jnp.concatenate(new_states, axis=1)
    return (y_prompt, y_sample, new_state_ssm)
```

```python
import functools

import jax
import jax.numpy as jnp
from jax import lax
from jax.experimental import pallas as pl
from jax.experimental.pallas import tpu as pltpu

F32 = jnp.float32
BF16 = jnp.bfloat16
U32 = jnp.uint32

D_MODEL = 1024
GRID_W = 64
D_INNER = 2 * D_MODEL
HEAD_DIM = 64
N_HEADS = D_INNER // HEAD_DIM
N_GROUPS = 8
HEADS_PER_GROUP = N_HEADS // N_GROUPS
D_STATE = 128
SSM_CONV = 5
CHUNK = 128
D_XBC = D_INNER + 2 * N_GROUPS * D_STATE
D_CONF = D_MODEL
CONF_KERNEL = 31
N_EXPERTS = 32
TOP_K = 4
D_EXPERT = D_MODEL
SWIGLU_LIMIT = 7.0
SWIGLU_ALPHA = 1.702
EPS = 1e-6

LANES = 128
UNIT = 256
GROUP_W = D_INNER // N_GROUPS
SSD_CHUNKS = 2
SSD_GROUP_UNROLL = 4
MIX_CHAINS = 2
MOE_TILE = 128
ROUTE_PAD = 2 * MOE_TILE
MOE_CAST_ROWS = 16
XW = D_MODEL // (2 * LANES)
AW = D_MODEL // LANES
RANK_SHIFT = 5
MIB = 1024 * 1024
V7X_VMEM_BYTES = 64 * MIB
VMEM_LIMIT = V7X_VMEM_BYTES - 2 * MIB

COL_Z = 0
COL_XBC = D_INNER
COL_GLU = D_XBC + D_INNER
COL_GATE = D_XBC + D_INNER + 2 * D_CONF
N_MAIN = D_XBC + D_INNER + 2 * D_CONF + 2 * D_MODEL


def _sigmoid(x):
    return 0.5 * jnp.tanh(0.5 * x) + 0.5


def _silu(x):
    h = 0.5 * x
    return h * jnp.tanh(h) + h


def _cparams(n_axes, vmem=None):
    return pltpu.CompilerParams(
        dimension_semantics=("arbitrary",) * n_axes,
        vmem_limit_bytes=vmem)


def _ada_kernel(c_ref, w_ref, b_ref, o_ref):
    s = _silu(c_ref[...])
    o_ref[...] = jnp.dot(s.astype(BF16), w_ref[...].astype(BF16),
                         preferred_element_type=F32) + b_ref[...]


def _ada_mod(cvec, w_ada, b_ada):
    n = w_ada.shape[1]
    tn = 1536
    return pl.pallas_call(
        _ada_kernel,
        grid=(n // tn,),
        in_specs=[pl.BlockSpec((8, D_MODEL), lambda j: (0, 0)),
                  pl.BlockSpec((D_MODEL, tn), lambda j: (0, j)),
                  pl.BlockSpec((1, tn), lambda j: (0, j))],
        out_specs=pl.BlockSpec((8, tn), lambda j: (0, j)),
        out_shape=jax.ShapeDtypeStruct((8, n), F32),
        compiler_params=_cparams(1, 40 * MIB),
        name="ada_mod",
    )(cvec, w_ada, b_ada.reshape(1, n))


def _wprep_kernel(w_ref, wn_ref, oa_ref, ob_ref, odt_ref, *, a_tiles):
    n = pl.program_id(0)
    tn = w_ref.shape[0]
    shift = 2 * N_HEADS

    def put(o_ref, c, rows_t):
        for j in range(D_MODEL // LANES):
            o_ref[j * LANES:(j + 1) * LANES, c * LANES:(c + 1) * LANES] = \
                rows_t[:, j * LANES:(j + 1) * LANES].T.astype(BF16)

    @pl.when(n < a_tiles)
    def _():
        for c in range(tn // LANES):
            put(oa_ref, c, w_ref[c * LANES:(c + 1) * LANES, :])

    @pl.when(n >= a_tiles)
    def _():
        for c in range(tn // LANES):
            lo = w_ref[c * LANES + shift:(c + 1) * LANES, :]
            hi = w_ref[(c + 1) * LANES:(c + 1) * LANES + shift, :] if c + 1 < tn // LANES else wn_ref[0:shift, :]
            put(ob_ref, c, jnp.concatenate([lo, hi], axis=0))

    @pl.when(n == a_tiles)
    def _():
        zeros = jnp.zeros((LANES - N_HEADS, D_MODEL), F32)
        for d in range(2):
            put(odt_ref, d, jnp.concatenate([w_ref[d * N_HEADS:(d + 1) * N_HEADS, :], zeros], axis=0))


def _in_proj_weights(w_in):
    tn = 2048
    n_a = D_INNER + D_XBC
    n_b = 2 * D_CONF + 2 * D_MODEL
    a_tiles = n_a // tn
    assert n_a % tn == 0 and n_b % tn == 0 and 2 * N_HEADS < LANES
    n_tiles = a_tiles + n_b // tn
    assert w_in.shape[1] == n_tiles * tn + 2 * N_HEADS
    w_t = w_in.T
    return pl.pallas_call(
        functools.partial(_wprep_kernel, a_tiles=a_tiles),
        grid=(n_tiles,),
        in_specs=[pl.BlockSpec((tn, D_MODEL), lambda n: (n, 0)),
                  pl.BlockSpec((LANES, D_MODEL), lambda n: ((n + 1) * (tn // LANES), 0))],
        out_specs=[pl.BlockSpec((D_MODEL, tn), lambda n: (0, jnp.minimum(n, a_tiles - 1))),
                   pl.BlockSpec((D_MODEL, tn), lambda n: (0, jnp.maximum(n - a_tiles, 0))),
                   pl.BlockSpec((D_MODEL, 2 * LANES), lambda n: (0, 0))],
        out_shape=[jax.ShapeDtypeStruct((D_MODEL, n_a), BF16),
                   jax.ShapeDtypeStruct((D_MODEL, n_b), BF16),
                   jax.ShapeDtypeStruct((D_MODEL, 2 * LANES), BF16)],
        compiler_params=_cparams(1, 48 * MIB),
        name="in_proj_weights",
    )(w_t, w_t)


def _inproj_kernel(xc_ref, xl_ref, nw_ref, sc_ref, sh_ref, wa_ref, wb_ref, wdt_ref, o_ref, dt_ref, h_scr, *,
                   ctx_tiles, a_tiles):
    @pl.when(pl.program_id(1) == 0)
    def _():
        xf = jnp.where(pl.program_id(0) < ctx_tiles, xc_ref[...], xl_ref[...])
        ms = jnp.mean(xf * xf, axis=-1, keepdims=True)
        hn = (xf * lax.rsqrt(ms + EPS)) * nw_ref[...]
        hn = hn * (1.0 + sc_ref[...]) + sh_ref[...]
        hb = hn.astype(BF16)
        h_scr[...] = hb
        dt_ref[...] = jnp.dot(hb, wdt_ref[...], preferred_element_type=F32)

    @pl.when(pl.program_id(1) < a_tiles)
    def _():
        o_ref[...] = jnp.dot(h_scr[...], wa_ref[...], preferred_element_type=F32).astype(BF16)

    @pl.when(pl.program_id(1) >= a_tiles)
    def _():
        o_ref[...] = jnp.dot(h_scr[...], wb_ref[...], preferred_element_type=F32).astype(BF16)


def _two_source_specs(tm, width, ctx_tiles, nargs):
    if nargs == 1:
        return [pl.BlockSpec((tm, width), lambda m: (jnp.minimum(m, ctx_tiles - 1), 0)),
                pl.BlockSpec((tm, width), lambda m: (jnp.maximum(m - ctx_tiles, 0), 0))]
    return [pl.BlockSpec((tm, width), lambda m, n: (jnp.minimum(m, ctx_tiles - 1), 0)),
            pl.BlockSpec((tm, width), lambda m, n: (jnp.maximum(m - ctx_tiles, 0), 0))]


def _in_proj(x_ctx, x_lat, norm_w, modr, w_a, w_b, w_dt):
    t_ctx = x_ctx.shape[0]
    t = t_ctx + x_lat.shape[0]
    tm, tn = 1024, 2048
    upt = tm // UNIT
    a_tiles = w_a.shape[1] // tn
    return pl.pallas_call(
        functools.partial(_inproj_kernel, ctx_tiles=t_ctx // tm, a_tiles=a_tiles),
        grid=(t // tm, N_MAIN // tn),
        in_specs=_two_source_specs(tm, D_MODEL, t_ctx // tm, 2) + [
                  pl.BlockSpec((1, D_MODEL), lambda m, n: (0, 0)),
                  pl.BlockSpec((None, None, 1, D_MODEL), lambda m, n: (m * upt, 1, 0, 0)),
                  pl.BlockSpec((None, None, 1, D_MODEL), lambda m, n: (m * upt, 0, 0, 0)),
                  pl.BlockSpec((D_MODEL, tn), lambda m, n: (0, jnp.minimum(n, a_tiles - 1))),
                  pl.BlockSpec((D_MODEL, tn), lambda m, n: (0, jnp.maximum(n - a_tiles, 0))),
                  pl.BlockSpec((D_MODEL, 2 * LANES), lambda m, n: (0, 0))],
        out_specs=[pl.BlockSpec((tm, tn), lambda m, n: (m, n)),
                   pl.BlockSpec((tm, 2 * LANES), lambda m, n: (m, 0))],
        out_shape=[jax.ShapeDtypeStruct((t, N_MAIN), BF16),
                   jax.ShapeDtypeStruct((t, 2 * LANES), F32)],
        scratch_shapes=[pltpu.VMEM((tm, D_MODEL), BF16)],
        compiler_params=_cparams(2, 48 * MIB),
        name="in_proj",
    )(x_ctx, x_lat, norm_w.reshape(1, D_MODEL), modr, modr, w_a, w_b, w_dt)


def _ssm_conv_kernel(x_ref, w_ref, b_ref, o_ref, pad_scr, *, seq):
    cb = x_ref.shape[1]
    pad_scr[0:8, :] = jnp.zeros((8, cb), F32)
    pad_scr[8 + seq:16 + seq, :] = jnp.zeros((8, cb), F32)
    pad_scr[8:8 + seq, :] = x_ref[...].astype(F32)
    half = SSM_CONV // 2
    rows, cw = 256, 512
    for c0 in range(0, cb, cw):
        cs = slice(c0, c0 + cw)
        for r0 in range(0, seq, rows):
            win = pad_scr[r0:r0 + rows + 16, cs]
            acc = jnp.broadcast_to(b_ref[:, cs], (rows, cw))
            for k in range(SSM_CONV):
                tap = win if k == half else pltpu.roll(win, (half - k) % (rows + 16), 0)
                acc = acc + w_ref[k:k + 1, cs] * tap[8:8 + rows]
            o_ref[r0:r0 + rows, cs] = _silu(acc).astype(BF16)


def _ssm_conv(proj, conv_w, conv_b, *, seq, nseq, row0):
    cb = 2048
    blk0 = row0 // seq
    return pl.pallas_call(
        functools.partial(_ssm_conv_kernel, seq=seq),
        grid=(nseq, D_XBC // cb),
        in_specs=[pl.BlockSpec((seq, cb), lambda s, j: (blk0 + s, COL_XBC // cb + j)),
                  pl.BlockSpec((SSM_CONV, cb), lambda s, j: (0, j)),
                  pl.BlockSpec((1, cb), lambda s, j: (0, j))],
        out_specs=pl.BlockSpec((seq, cb), lambda s, j: (s, j)),
        out_shape=jax.ShapeDtypeStruct((nseq * seq, D_XBC), BF16),
        scratch_shapes=[pltpu.VMEM((seq + 16, cb), F32)],
        compiler_params=_cparams(2, 40 * MIB),
        name="ssm_conv",
    )(proj, conv_w, conv_b.reshape(1, D_XBC))


def _split_bf16(v):
    hi = v.astype(BF16)
    lo = (v - hi.astype(F32)).astype(BF16)
    return jnp.concatenate([hi, lo], axis=1)


def _head_select_matrices():
    j = jnp.arange(2 * LANES, dtype=jnp.int32)[:, None] % LANES
    full = (j == (jnp.arange(N_HEADS * LANES, dtype=jnp.int32)[None, :] // LANES)).astype(BF16)
    exp = (j == (jnp.arange(D_INNER, dtype=jnp.int32)[None, :] // HEAD_DIM)).astype(BF16)
    return full, exp


def _ssd_kernel(*refs, nc, has_h0, write_state):
    (xs_ref, b_ref, c_ref, z_ref, dtr_ref, dtb_ref, alog_ref, dskip_ref, nw_ref, self_ref, sele_ref), rest = \
        refs[:11], refs[11:]
    if has_h0:
        h0_ref, rest = rest[0], rest[1:]
    y_ref, rest = rest[0], rest[1:]
    if write_state:
        st_ref, rest = rest[0], rest[1:]
    h_scr, ybuf, ychunk, colb_scr, wexp_scr, eexp_scr, texp_scr, rowq_scr = rest

    q = CHUNK
    phase = pl.program_id(1)
    c = pl.program_id(2)
    is_fwd = phase == 1
    c_eff = jnp.where(is_fwd, c, nc - 1 - c)

    @pl.when(c == 0)
    def _():
        if has_h0:
            for j in range(D_INNER // LANES):
                h_scr[:, j * LANES:(j + 1) * LANES] = h0_ref[j * LANES:(j + 1) * LANES, :].T
        else:
            h_scr[...] = jnp.zeros(h_scr.shape, F32)

    nrow = SSD_CHUNKS * q
    xdt = dtr_ref[...] + dtb_ref[...]
    dt = jnp.maximum(xdt, 0.0) + jnp.log(1.0 + jnp.exp(-jnp.abs(xdt)))
    a = -jnp.exp(alog_ref[...])
    adt = dt * a
    sgn = jnp.where(is_fwd, 1, -1)
    row = lax.broadcasted_iota(jnp.int32, (nrow, nrow), 0)
    col = lax.broadcasted_iota(jnp.int32, (nrow, nrow), 1)
    tri_blk = jnp.logical_and(row // q == col // q, (col - row) * sgn <= 0)
    cum2 = jnp.dot(tri_blk.astype(BF16), _split_bf16(adt), preferred_element_type=F32)
    cum = cum2[:, :LANES] + cum2[:, LANES:]
    tots = [jnp.where(is_fwd, cum[k * q + q - 1:k * q + q, :], cum[k * q:k * q + 1, :]) for k in range(SSD_CHUNKS)]
    tot_rows = jnp.concatenate([jnp.broadcast_to(t, (q, LANES)) for t in tots], axis=0)
    rowq = (cum - jnp.log(dt)).T
    for k in range(SSD_CHUNKS):
        rowq_scr[k] = rowq[:, k * q:(k + 1) * q]
    wdec = dt * jnp.exp(tot_rows - cum)
    eo = jnp.exp(cum)
    etot = jnp.concatenate([jnp.broadcast_to(jnp.exp(t), (8, LANES)) for t in tots], axis=0)
    colb_scr[...] = jnp.dot(_split_bf16(cum), self_ref[...], preferred_element_type=F32)
    expanded = jnp.dot(jnp.concatenate([_split_bf16(wdec), _split_bf16(eo), _split_bf16(etot)], axis=0),
                       sele_ref[...], preferred_element_type=F32)
    wexp_scr[...] = expanded[0:nrow]
    eexp_scr[...] = expanded[nrow:2 * nrow]
    for k in range(SSD_CHUNKS):
        texp_scr[k] = expanded[2 * nrow + 8 * k:2 * nrow + 8 * (k + 1)]
    lrow = lax.broadcasted_iota(jnp.int32, (q, q), 0)
    lcol = lax.broadcasted_iota(jnp.int32, (q, q), 1)
    tri = (lcol - lrow) * sgn <= 0
    lane_g = lax.broadcasted_iota(jnp.int32, (1, GROUP_W), 1)
    head_mask = [(lane_g // HEAD_DIM == hh).astype(BF16) for hh in range(HEADS_PER_GROUP)]
    neg_inf = jnp.float32(-jnp.inf)

    def group_body(g, carry):
        gs = pl.ds(pl.multiple_of(g * GROUP_W, GROUP_W), GROUP_W)
        ns = pl.ds(pl.multiple_of(g * D_STATE, D_STATE), D_STATE)
        for k in range(SSD_CHUNKS):
            ci = jnp.where(is_fwd, k, SSD_CHUNKS - 1 - k)
            r0 = pl.multiple_of(ci * q, q)
            rs = pl.ds(r0, q)
            bg = b_ref[rs, ns]
            cg = c_ref[rs, ns]
            scores = lax.dot_general(cg, bg, (((1,), (1,)), ((), ())), preferred_element_type=F32)
            xs_g = xs_ref[rs, gs]
            ms, xb = [], []
            for hh in range(HEADS_PER_GROUP):
                h = HEADS_PER_GROUP * g + hh
                colb = colb_scr[rs, pl.ds(pl.multiple_of(h * LANES, LANES), LANES)]
                seg = jnp.where(tri, colb - rowq_scr[ci, pl.ds(h, 1), :], neg_inf)
                ms.append((scores * jnp.exp(seg)).astype(BF16))
                xb.append(xs_g * head_mask[hh])
            y_diag = jnp.dot(jnp.concatenate(ms, axis=1), jnp.concatenate(xb, axis=0),
                             preferred_element_type=F32)
            xd = (xs_g.astype(F32) * wexp_scr[rs, gs]).astype(BF16)
            bt = bg.astype(F32).T.astype(BF16)
            st = jnp.dot(bt, xd, preferred_element_type=F32)
            hg = h_scr[:, gs]
            y_off = jnp.dot(cg, hg.astype(BF16), preferred_element_type=F32) * eexp_scr[rs, gs]
            ychunk[rs, gs] = y_diag + y_off
            h_scr[:, gs] = hg * texp_scr[ci, 0:1, gs] + st
        return carry

    lax.fori_loop(0, N_GROUPS, group_body, 0, unroll=SSD_GROUP_UNROLL)

    rows = pl.ds(pl.multiple_of(c_eff * nrow, nrow), nrow)

    @pl.when(jnp.logical_not(is_fwd))
    def _():
        ybuf[rows, :] = ychunk[...]

    @pl.when(is_fwd)
    def _():
        zf = z_ref[...].astype(F32)
        yt = ychunk[...] + ybuf[rows, :] + xs_ref[...].astype(F32) * dskip_ref[...]
        yz = yt * _silu(zf)
        for g in range(N_GROUPS):
            gs = slice(g * GROUP_W, (g + 1) * GROUP_W)
            blk = yz[:, gs]
            ms = jnp.mean(blk * blk, axis=-1, keepdims=True)
            y_ref[:, gs] = (blk * lax.rsqrt(ms + EPS) * nw_ref[:, gs]).astype(BF16)

    if write_state:
        @pl.when(c == nc - 1)
        def _():
            for j in range(D_INNER // LANES):
                st_ref[j * LANES:(j + 1) * LANES, :] = h_scr[:, j * LANES:(j + 1) * LANES].T


def _ssd(xbc_c, proj, dt_raw, dt_bias, a_log, d_skip, norm_w, h0t, *, seq, nseq, row0, write_state):
    rb = SSD_CHUNKS * CHUNK
    nc = seq // rb
    blk0 = row0 // rb
    has_h0 = h0t is not None
    assert seq % rb == 0 and row0 % rb == 0

    def tok(s, p, c):
        return s * nc + p * c + (1 - p) * (nc - 1 - c)

    in_specs = [
        pl.BlockSpec((rb, D_INNER), lambda s, p, c: (tok(s, p, c), 0)),
        pl.BlockSpec((rb, N_GROUPS * D_STATE), lambda s, p, c: (tok(s, p, c), 2)),
        pl.BlockSpec((rb, N_GROUPS * D_STATE), lambda s, p, c: (tok(s, p, c), 3)),
        pl.BlockSpec((rb, D_INNER), lambda s, p, c: (blk0 + s * nc + p * c, COL_Z // D_INNER)),
        pl.BlockSpec((rb, LANES), lambda s, p, c: (blk0 + tok(s, p, c), 1 - p)),
        pl.BlockSpec((None, 1, LANES), lambda s, p, c: (1 - p, 0, 0)),
        pl.BlockSpec((None, 1, LANES), lambda s, p, c: (1 - p, 0, 0)),
        pl.BlockSpec((1, D_INNER), lambda s, p, c: (0, 0)),
        pl.BlockSpec((1, D_INNER), lambda s, p, c: (0, 0)),
        pl.BlockSpec((2 * LANES, N_HEADS * LANES), lambda s, p, c: (0, 0)),
        pl.BlockSpec((2 * LANES, D_INNER), lambda s, p, c: (0, 0)),
    ]
    sel_full, sel_exp = _head_select_matrices()
    args = [xbc_c, xbc_c, xbc_c, proj, dt_raw, dt_bias, a_log, d_skip, norm_w, sel_full, sel_exp]
    if has_h0:
        in_specs.append(pl.BlockSpec((None, None, D_INNER, D_STATE), lambda s, p, c: (s, 1 - p, 0, 0)))
        args.append(h0t)
    out_specs = [pl.BlockSpec((rb, D_INNER), lambda s, p, c: (s * nc + p * c, 0))]
    out_shape = [jax.ShapeDtypeStruct((nseq * seq, D_INNER), BF16)]
    if write_state:
        out_specs.append(pl.BlockSpec((None, None, D_INNER, D_STATE), lambda s, p, c: (s, 1 - p, 0, 0)))
        out_shape.append(jax.ShapeDtypeStruct((nseq, 2, D_INNER, D_STATE), F32))
    return pl.pallas_call(
        functools.partial(_ssd_kernel, nc=nc, has_h0=has_h0, write_state=write_state),
        grid=(nseq, 2, nc),
        in_specs=in_specs,
        out_specs=out_specs,
        out_shape=out_shape,
        scratch_shapes=[pltpu.VMEM((D_STATE, D_INNER), F32),
                        pltpu.VMEM((seq, D_INNER), F32),
                        pltpu.VMEM((rb, D_INNER), F32),
                        pltpu.VMEM((rb, N_HEADS * LANES), F32),
                        pltpu.VMEM((rb, D_INNER), F32),
                        pltpu.VMEM((rb, D_INNER), F32),
                        pltpu.VMEM((SSD_CHUNKS, 8, D_INNER), F32),
                        pltpu.VMEM((SSD_CHUNKS, LANES, CHUNK), F32)],
        compiler_params=_cparams(3, 48 * MIB),
        name="ssd_scan",
    )(*args)


def _conf_kernel(glu_ref, w_ref, b_ref, lnw_ref, lnb_ref, o_ref, pad_scr, sh_scr, conv_scr, *, seg):
    rows = glu_ref.shape[0]
    nseg = rows // seg
    half = CONF_KERNEL // 2
    front = 16
    span = seg + 24
    a = glu_ref[:, :D_CONF].astype(F32)
    b = glu_ref[:, D_CONF:].astype(F32)
    u = a * _sigmoid(b)
    for i in range(nseg):
        pad_scr[i, 0:front, :] = jnp.zeros((front, D_CONF), F32)
        pad_scr[i, front + seg:front + seg + 16, :] = jnp.zeros((16, D_CONF), F32)
        pad_scr[i, front:front + seg, :] = u[i * seg:(i + 1) * seg, :]
    padded = seg + 32
    for i in range(nseg):
        for cbi in range(D_CONF // LANES):
            cs = slice(cbi * LANES, (cbi + 1) * LANES)
            seg_pad = pad_scr[i, :, cs]
            for s in range(8):
                sh_scr[s, i, :, cs] = (seg_pad if s == 0 else pltpu.roll(seg_pad, padded - s, 0))[0:span]
    rb = 64
    for i in range(nseg):
        for cbi in range(D_CONF // LANES):
            cs = slice(cbi * LANES, (cbi + 1) * LANES)
            for r0 in range(0, seg, rb):
                acc = jnp.broadcast_to(b_ref[:, cs], (rb, LANES))
                for k in range(CONF_KERNEL):
                    start = front + r0 + k - half
                    al = start - start % 8
                    acc = acc + w_ref[k:k + 1, cs] * sh_scr[start % 8, i, al:al + rb, cs]
                conv_scr[i * seg + r0:i * seg + r0 + rb, cs] = acc
    v = conv_scr[...]
    mu = jnp.mean(v, axis=-1, keepdims=True)
    vc = v - mu
    var = jnp.mean(vc * vc, axis=-1, keepdims=True)
    ln = (vc * lax.rsqrt(var + EPS)) * lnw_ref[...] + lnb_ref[...]
    o_ref[...] = _silu(ln).astype(BF16)


def _conformer(proj, dw_w, dw_b, ln_w, ln_b, *, seg, ntok, row0):
    rows = UNIT
    blk0 = row0 // rows
    return pl.pallas_call(
        functools.partial(_conf_kernel, seg=seg),
        grid=(ntok // rows,),
        in_specs=[pl.BlockSpec((rows, 2 * D_CONF), lambda i: (blk0 + i, COL_GLU // (2 * D_CONF))),
                  pl.BlockSpec((CONF_KERNEL, D_CONF), lambda i: (0, 0)),
                  pl.BlockSpec((1, D_CONF), lambda i: (0, 0)),
                  pl.BlockSpec((1, D_CONF), lambda i: (0, 0)),
                  pl.BlockSpec((1, D_CONF), lambda i: (0, 0))],
        out_specs=pl.BlockSpec((rows, D_CONF), lambda i: (i, 0)),
        out_shape=jax.ShapeDtypeStruct((ntok, D_CONF), BF16),
        scratch_shapes=[pltpu.VMEM((rows // seg, seg + 32, D_CONF), F32),
                        pltpu.VMEM((8, rows // seg, seg + 24, D_CONF), F32),
                        pltpu.VMEM((rows, D_CONF), F32)],
        compiler_params=_cparams(1, 40 * MIB),
        name="conformer_conv",
    )(proj, dw_w, dw_b.reshape(1, D_CONF), ln_w.reshape(1, D_CONF), ln_b.reshape(1, D_CONF))


def _pack_halves(x):
    outs = []
    for cb in range(x.shape[1] // (2 * LANES)):
        hi = x[:, cb * 2 * LANES:cb * 2 * LANES + LANES].astype(BF16).astype(F32)
        lo = x[:, cb * 2 * LANES + LANES:(cb + 1) * 2 * LANES].astype(BF16).astype(F32)
        hw = lax.bitcast_convert_type(hi, U32)
        lw = jnp.right_shift(lax.bitcast_convert_type(lo, U32), jnp.uint32(16))
        outs.append(jnp.bitwise_or(hw, lw))
    return jnp.concatenate(outs, axis=1)


def _mix_kernel(xc_ref, xl_ref, yc_ref, yl_ref, uc_ref, ul_ref, gate_ref, g1_ref, sc2_ref, sh2_ref,
                wssd_ref, wconf_ref, bconf_ref, bgate_ref, wo_ref, nffn_ref, wr_ref, br_ref,
                x1_ref, h2i_ref, er_ref, topp_ref, cnt_ref, cnt_scr, *, ctx_tiles):
    @pl.when(pl.program_id(0) == 0)
    def _():
        cnt_scr[...] = jnp.zeros(cnt_scr.shape, F32)

    is_ctx = pl.program_id(0) < ctx_tiles
    rows = x1_ref.shape[0]
    sub = rows // MIX_CHAINS
    lane = lax.broadcasted_iota(jnp.int32, (sub, LANES), 1)
    lane_f = lane.astype(F32)
    neg_inf = jnp.float32(-jnp.inf)

    def chain(c):
        rs = slice(c * sub, (c + 1) * sub)
        y_in = jnp.where(is_ctx, yc_ref[rs, :], yl_ref[rs, :])
        u_in = jnp.where(is_ctx, uc_ref[rs, :], ul_ref[rs, :])
        o_ssd = jnp.dot(y_in, wssd_ref[...], preferred_element_type=F32)
        o_conf = jnp.dot(u_in, wconf_ref[...], preferred_element_type=F32) + bconf_ref[...]
        gates = _sigmoid(gate_ref[rs, :].astype(F32) + bgate_ref[...])
        merged = gates[:, :D_MODEL] * o_ssd + gates[:, D_MODEL:] * o_conf
        out = jnp.dot(merged.astype(BF16), wo_ref[...], preferred_element_type=F32)
        x1 = jnp.where(is_ctx, xc_ref[rs, :], xl_ref[rs, :]) + g1_ref[...] * out
        x1_ref[rs, :] = x1
        ms = jnp.mean(x1 * x1, axis=-1, keepdims=True)
        h2 = (x1 * lax.rsqrt(ms + EPS)) * nffn_ref[...]
        h2 = h2 * (1.0 + sc2_ref[...]) + sh2_ref[...]
        packed = _pack_halves(h2)
        for qd in range(XW):
            h2i_ref[pl.ds(c * sub * XW + qd, sub, stride=XW), :] = packed[:, qd * LANES:(qd + 1) * LANES]
        h_hi = h2.astype(BF16)
        h_lo = (h2 - h_hi.astype(F32)).astype(BF16)
        logits = jnp.dot(jnp.concatenate([h_hi, h_hi, h_lo], axis=1), wr_ref[...],
                         preferred_element_type=F32) + br_ref[...]
        work = jnp.where(lane < N_EXPERTS, logits, neg_inf)
        vals, idxs = [], []
        for _ in range(TOP_K):
            m = jnp.max(work, axis=-1, keepdims=True)
            idx = jnp.min(jnp.where(work == m, lane_f, jnp.float32(LANES)), axis=-1, keepdims=True)
            vals.append(m)
            idxs.append(idx)
            work = jnp.where(lane_f == idx, neg_inf, work)
        es = [jnp.exp(v - vals[0]) for v in vals]
        denom = es[0] + es[1] + es[2] + es[3]
        member = jnp.zeros((sub, LANES), F32)
        topp = jnp.zeros((sub, LANES), F32)
        for k in range(TOP_K):
            member = member + jnp.where(lane_f == idxs[k], 1.0, 0.0)
            topp = jnp.where(lane == k, es[k] / denom, topp)
        topp_ref[rs, :] = topp
        return idxs, member

    results = [chain(c) for c in range(MIX_CHAINS)]
    member = jnp.concatenate([m for _, m in results], axis=0)
    r_i = lax.broadcasted_iota(jnp.int32, (rows, rows), 0)
    c_i = lax.broadcasted_iota(jnp.int32, (rows, rows), 1)
    earlier = jnp.where(c_i < r_i, 1.0, 0.0).astype(BF16)
    rank = jnp.dot(earlier, member.astype(BF16), preferred_element_type=F32) + cnt_scr[...]
    cnt = cnt_scr[...] + jnp.sum(member, axis=0, keepdims=True)
    cnt_scr[...] = cnt
    cnt_ref[...] = jnp.broadcast_to(cnt, cnt_ref.shape).astype(jnp.int32)
    for c, (idxs, _) in enumerate(results):
        rs = slice(c * sub, (c + 1) * sub)
        er = jnp.zeros((sub, LANES), F32)
        for k in range(TOP_K):
            rank_k = jnp.sum(jnp.where(lane_f == idxs[k], rank[rs, :], 0.0), axis=-1, keepdims=True)
            er = jnp.where(lane == k, idxs[k] + N_EXPERTS * rank_k, er)
        er_ref[rs, :] = er.astype(jnp.int32)


def _mix_out(x_ctx, x_lat, y_ctx, y_lat, u_ctx, u_lat, proj, modr, w_ssd, w_conf, b_conf, b_gate, w_o, norm_ffn,
             w_router3, b_router):
    t_ctx = x_ctx.shape[0]
    t = t_ctx + x_lat.shape[0]
    tm = 512
    upt = tm // UNIT
    ctx_tiles = t_ctx // tm
    full = lambda shape: pl.BlockSpec(shape, lambda m: (0,) * len(shape), pipeline_mode=pl.Buffered(1))
    mod = lambda which: pl.BlockSpec((None, None, 1, D_MODEL), lambda m: (m * upt, which, 0, 0))
    return pl.pallas_call(
        functools.partial(_mix_kernel, ctx_tiles=ctx_tiles),
        grid=(t // tm,),
        in_specs=_two_source_specs(tm, D_MODEL, ctx_tiles, 1) + _two_source_specs(tm, D_INNER, ctx_tiles, 1)
                 + _two_source_specs(tm, D_CONF, ctx_tiles, 1) + [
                  pl.BlockSpec((tm, 2 * D_MODEL), lambda m: (m, COL_GATE // (2 * D_MODEL))),
                  mod(2), mod(4), mod(3),
                  full((D_INNER, D_MODEL)), full((D_CONF, D_MODEL)), full((1, D_MODEL)),
                  full((1, 2 * D_MODEL)), full((D_MODEL, D_MODEL)), full((1, D_MODEL)),
                  full((3 * D_MODEL, LANES)), full((1, LANES))],
        out_specs=[pl.BlockSpec((tm, D_MODEL), lambda m: (m, 0)),
                   pl.BlockSpec((tm * XW, LANES), lambda m: (m, 0)),
                   pl.BlockSpec((tm, LANES), lambda m: (m, 0)),
                   pl.BlockSpec((tm, LANES), lambda m: (m, 0)),
                   pl.BlockSpec((8, LANES), lambda m: (0, 0))],
        out_shape=[jax.ShapeDtypeStruct((t, D_MODEL), F32),
                   jax.ShapeDtypeStruct((t * XW, LANES), U32),
                   jax.ShapeDtypeStruct((t, LANES), jnp.int32),
                   jax.ShapeDtypeStruct((t, LANES), F32),
                   jax.ShapeDtypeStruct((8, LANES), jnp.int32)],
        scratch_shapes=[pltpu.VMEM((1, LANES), F32)],
        compiler_params=_cparams(1, 56 * MIB),
        name="mix_out_router",
    )(x_ctx, x_lat, y_ctx, y_lat, u_ctx, u_lat, proj, modr, modr, modr, w_ssd, w_conf, b_conf.reshape(1, D_MODEL),
      b_gate.reshape(1, 2 * D_MODEL), w_o, norm_ffn.reshape(1, D_MODEL), w_router3, b_router)


def _moe_kernel(off_ref, pair_ref, pb_ref, h2i_ref, wgu_hbm, wd_hbm, bgu_ref, bd_ref, out_ref,
                acc, x_scr, y_scr, stage_gu, stage_d, wgu_b, wd_b, wsem, osem):
    e = pl.program_id(0)
    tr = MOE_TILE

    def weight_copies(ex):
        return (pltpu.make_async_copy(wgu_hbm.at[ex], stage_gu, wsem.at[0]),
                pltpu.make_async_copy(wd_hbm.at[ex], stage_d, wsem.at[1]))

    @pl.when(e == 0)
    def _():
        acc[...] = jnp.zeros(acc.shape, F32)
        for cp in weight_copies(0):
            cp.start()

    for cp in weight_copies(e):
        cp.wait()

    start = off_ref[e]
    end = off_ref[e + 1]
    ntiles = (end - start + tr - 1) // tr
    bgu = bgu_ref[...]
    bd = bd_ref[...]
    cast_steps = D_MODEL // MOE_CAST_ROWS
    rows_per_step = tr // cast_steps

    def to_bf16(i, c):
        rows = pl.ds(pl.multiple_of(i * MOE_CAST_ROWS, MOE_CAST_ROWS), MOE_CAST_ROWS)
        wgu_b[rows, :] = stage_gu[rows, :].astype(BF16)
        wd_b[rows, :] = stage_d[rows, :].astype(BF16)
        for rr in range(rows_per_step):
            r = i * rows_per_step + rr
            src = jnp.bitwise_and(pair_ref[start + r], -TOP_K)
            x_scr[0, pl.ds(pl.multiple_of(r * XW, XW), XW), :] = h2i_ref[pl.ds(pl.multiple_of(src, XW), XW), :]
        return c

    lax.fori_loop(0, cast_steps, to_bf16, 0, unroll=2)

    @pl.when(e + 1 < N_EXPERTS)
    def _():
        for cp in weight_copies(e + 1):
            cp.start()

    def gather_rows(base, slot, r0, n):
        for r in range(r0, r0 + n):
            src = jnp.bitwise_and(pair_ref[base + r], -TOP_K)
            x_scr[slot, pl.ds(r * XW, XW), :] = h2i_ref[pl.ds(pl.multiple_of(src, XW), XW), :]

    def scatter_rows(base, slot, r0):
        dsts, ps = [], []
        for rr in range(8):
            pair = pair_ref[base + r0 + rr]
            dsts.append(pl.multiple_of(jnp.bitwise_and(pair, -TOP_K) * (AW // XW), AW))
            ps.append(lax.bitcast_convert_type(pb_ref[pair], F32))
        olds = [acc[pl.ds(dsts[rr], AW), :] for rr in range(8)]
        news = [olds[rr] + ps[rr] * y_scr[slot, pl.ds((r0 + rr) * AW, AW), :] for rr in range(8)]
        for rr in range(8):
            acc[pl.ds(dsts[rr], AW), :] = news[rr]

    def tile_step(ti, first):
        slot = jnp.bitwise_and(ti, 1)
        other = 1 - slot
        base = start + ti * tr
        parts = []
        for qd in range(XW):
            w = x_scr[slot, pl.ds(qd, tr, stride=XW), :]
            parts.append(lax.bitcast_convert_type(jnp.bitwise_and(w, jnp.uint32(0xFFFF0000)), F32))
            parts.append(lax.bitcast_convert_type(jnp.left_shift(w, jnp.uint32(16)), F32))
        x = jnp.concatenate(parts, axis=1).astype(BF16)
        if not first:
            for r0 in range(0, tr, 8):
                scatter_rows(base - tr, other, r0)
        gather_rows(base + tr, other, 0, tr)
        gu = jnp.dot(x, wgu_b[...], preferred_element_type=F32) + bgu
        g = jnp.minimum(gu[:, :D_EXPERT], SWIGLU_LIMIT)
        u = jnp.clip(gu[:, D_EXPERT:], -SWIGLU_LIMIT, SWIGLU_LIMIT)
        act = (u + 1.0) * g * _sigmoid(SWIGLU_ALPHA * g)
        y = jnp.dot(act.astype(BF16), wd_b[...], preferred_element_type=F32) + bd
        for qd in range(AW):
            y_scr[slot, pl.ds(qd, tr, stride=AW), :] = y[:, qd * LANES:(qd + 1) * LANES]

    @pl.when(ntiles > 0)
    def _():
        tile_step(0, True)

    def tile_body(ti, carry):
        tile_step(ti, False)
        return carry

    lax.fori_loop(1, ntiles, tile_body, 0)

    @pl.when(ntiles > 0)
    def _():
        last = ntiles - 1
        lslot = jnp.bitwise_and(last, 1)
        lbase = start + last * tr

        def last_scatter(r8, c2):
            dsts, ps = [], []
            for rr in range(8):
                r = lbase + r8 * 8 + rr
                pair = pair_ref[r]
                dsts.append(pl.multiple_of(jnp.bitwise_and(pair, -TOP_K) * (AW // XW), AW))
                pw = lax.bitcast_convert_type(pb_ref[pair], F32)
                ps.append(jnp.where(r < end, pw, 0.0))
            olds = [acc[pl.ds(dsts[rr], AW), :] for rr in range(8)]
            news = [olds[rr] + ps[rr] * y_scr[lslot, pl.ds(pl.multiple_of((r8 * 8 + rr) * AW, AW), AW), :]
                    for rr in range(8)]
            for rr in range(8):
                acc[pl.ds(dsts[rr], AW), :] = news[rr]
            return c2

        lax.fori_loop(0, tr // 8, last_scatter, 0)

    @pl.when(e == N_EXPERTS - 1)
    def _():
        cp = pltpu.make_async_copy(acc, out_ref, osem)
        cp.start()
        cp.wait()


def _moe_grouped(h2i, off, row_pair, pbits, w_gu, b_gu, w_down, b_down):
    t = h2i.shape[0] // XW
    assert D_EXPERT == D_MODEL
    grid_spec = pltpu.PrefetchScalarGridSpec(
        num_scalar_prefetch=3,
        grid=(N_EXPERTS,),
        in_specs=[pl.BlockSpec((t * XW, LANES), lambda e, *_: (0, 0), pipeline_mode=pl.Buffered(1)),
                  pl.BlockSpec(memory_space=pl.ANY),
                  pl.BlockSpec(memory_space=pl.ANY),
                  pl.BlockSpec((None, 1, 2 * D_EXPERT), lambda e, *_: (e, 0, 0)),
                  pl.BlockSpec((None, 1, D_MODEL), lambda e, *_: (e, 0, 0))],
        out_specs=pl.BlockSpec(memory_space=pl.ANY),
        scratch_shapes=[pltpu.VMEM((t * AW, LANES), F32),
                        pltpu.VMEM((2, MOE_TILE * XW, LANES), U32),
                        pltpu.VMEM((2, MOE_TILE * AW, LANES), F32),
                        pltpu.VMEM((D_MODEL, 2 * D_EXPERT), F32),
                        pltpu.VMEM((D_EXPERT, D_MODEL), F32),
                        pltpu.VMEM((D_MODEL, 2 * D_EXPERT), BF16),
                        pltpu.VMEM((D_EXPERT, D_MODEL), BF16),
                        pltpu.SemaphoreType.DMA((2,)),
                        pltpu.SemaphoreType.DMA(())])
    return pl.pallas_call(
        _moe_kernel,
        grid_spec=grid_spec,
        out_shape=jax.ShapeDtypeStruct((t * AW, LANES), F32),
        compiler_params=_cparams(1, VMEM_LIMIT),
        name="moe_grouped",
    )(off, row_pair, pbits, h2i, w_gu, w_down, b_gu.reshape(N_EXPERTS, 1, 2 * D_EXPERT),
      b_down.reshape(N_EXPERTS, 1, D_MODEL))


def _residual_kernel(x1_ref, moe_ref, g2_ref, nw_ref, oc_ref, ol_ref, *, normalize, ctx_tiles):
    tm = x1_ref.shape[0]
    moe = jnp.concatenate([moe_ref[pl.ds(qd, tm, stride=AW), :] for qd in range(AW)], axis=1)
    x2 = x1_ref[...] + g2_ref[...] * moe
    if normalize:
        ms = jnp.mean(x2 * x2, axis=-1, keepdims=True)
        x2 = (x2 * lax.rsqrt(ms + EPS)) * nw_ref[...]

    @pl.when(pl.program_id(0) < ctx_tiles)
    def _():
        oc_ref[...] = x2

    @pl.when(pl.program_id(0) >= ctx_tiles)
    def _():
        ol_ref[...] = x2


def _residual(x1, moe_i, modr, norm_w, *, normalize, t_ctx):
    t = x1.shape[0]
    tm = 512
    upt = tm // UNIT
    ctx_tiles = t_ctx // tm
    return pl.pallas_call(
        functools.partial(_residual_kernel, normalize=normalize, ctx_tiles=ctx_tiles),
        grid=(t // tm,),
        in_specs=[pl.BlockSpec((tm, D_MODEL), lambda m: (m, 0)),
                  pl.BlockSpec((tm * AW, LANES), lambda m: (m, 0)),
                  pl.BlockSpec((None, None, 1, D_MODEL), lambda m: (m * upt, 5, 0, 0)),
                  pl.BlockSpec((1, D_MODEL), lambda m: (0, 0))],
        out_specs=_two_source_specs(tm, D_MODEL, ctx_tiles, 1),
        out_shape=[jax.ShapeDtypeStruct((t_ctx, D_MODEL), F32),
                   jax.ShapeDtypeStruct((t - t_ctx, D_MODEL), F32)],
        compiler_params=_cparams(1),
        name="ffn_residual_norm",
    )(x1, moe_i, modr, norm_w.reshape(1, D_MODEL))


def _route_kernel(pos_ref, pair_ref, *, n_pairs):
    def pad(i, c):
        pair_ref[n_pairs + i] = 0
        return c

    lax.fori_loop(0, ROUTE_PAD, pad, 0)

    def place(i8, c):
        for ii in range(8):
            i = i8 * 8 + ii
            pair_ref[pos_ref[i]] = i
        return c

    lax.fori_loop(0, n_pairs // 8, place, 0)


def _route(er_flat, cnt):
    n = er_flat.shape[0]
    off_end = jnp.cumsum(cnt)
    off = off_end - cnt
    expert = jnp.bitwise_and(er_flat, N_EXPERTS - 1)
    onehot = expert[:, None] == jnp.arange(N_EXPERTS, dtype=jnp.int32)[None, :]
    pos = lax.shift_right_logical(er_flat, RANK_SHIFT) + jnp.sum(jnp.where(onehot, off[None, :], 0), axis=1)
    smem = pl.BlockSpec(memory_space=pltpu.SMEM)
    row_pair = pl.pallas_call(
        functools.partial(_route_kernel, n_pairs=n),
        in_specs=[smem],
        out_specs=smem,
        out_shape=jax.ShapeDtypeStruct((n + ROUTE_PAD,), jnp.int32),
        name="moe_route",
    )(pos)
    return jnp.concatenate([off, jnp.broadcast_to(off_end[-1:], (8,))]), row_pair


def kernel(x_prompt, x_sample, state_ssm, c, c_ctx, w_ada, b_ada, norm_mix, norm_ffn, w_in, ssm_conv_w, ssm_conv_b, dt_bias, a_log, d_skip, ssm_norm_w, w_ssd_out, conf_dw_w, conf_dw_b, conf_ln_w, conf_ln_b, w_conf_out, b_conf_out, b_gate, w_o, w_router, b_router, w_gu, b_gu, w_down, b_down, norm_final):
    n_ctx, seq_ctx, _ = x_prompt.shape
    n_lat, seq_lat, _ = x_sample.shape
    depth = w_in.shape[0]
    t_ctx, t_lat = n_ctx * seq_ctx, n_lat * seq_lat
    t_all = t_ctx + t_lat
    assert seq_ctx == UNIT and seq_lat % 1024 == 0 and t_ctx % 1024 == 0
    assert n_lat + 1 <= 8 and seq_lat % GRID_W == 0 and UNIT % GRID_W == 0

    x_ctx, x_lat = x_prompt.reshape(t_ctx, D_MODEL), x_sample.reshape(t_lat, D_MODEL)
    cvec =jnp.concatenate([c_ctx[None, :], c, jnp.zeros((8 - 1 - n_lat, D_MODEL), F32)], axis=0)
    unit_row = jnp.concatenate([jnp.zeros((t_ctx // UNIT,), jnp.int32),
                                1 + jnp.arange(t_lat // UNIT, dtype=jnp.int32) // (seq_lat // UNIT)])
    assert XW == TOP_K and N_EXPERTS == 1 << RANK_SHIFT

    new_states = []
    for l in range(depth):
        mod = _ada_mod(cvec, w_ada[l], b_ada[l])
        modr = mod[unit_row].reshape(t_all // UNIT, 6, 1, D_MODEL)

        w_a, w_b, w_dt = _in_proj_weights(w_in[l])
        proj, dt_raw = _in_proj(x_ctx, x_lat, norm_mix[l], modr, w_a, w_b, w_dt)

        pad_h = lambda v: jnp.pad(v.reshape(2, 1, N_HEADS), ((0, 0), (0, 0), (0, LANES - N_HEADS)))
        dtb, alog = pad_h(dt_bias[l]), pad_h(a_log[l])
        dskip = jnp.repeat(d_skip[l], HEAD_DIM).reshape(1, D_INNER)
        nw = ssm_norm_w[l].reshape(1, D_INNER)
        h0t = state_ssm[:, l].astype(F32).reshape(n_lat, 2, D_INNER, D_STATE)

        xbc_ctx = _ssm_conv(proj, ssm_conv_w[l], ssm_conv_b[l], seq=seq_ctx, nseq=n_ctx, row0=0)
        xbc_lat = _ssm_conv(proj, ssm_conv_w[l], ssm_conv_b[l], seq=seq_lat, nseq=n_lat, row0=t_ctx)
        y_ctx, st_ctx = _ssd(xbc_ctx, proj, dt_raw, dtb, alog, dskip, nw, None,
                             seq=seq_ctx, nseq=n_ctx, row0=0, write_state=True)
        (y_lat,) = _ssd(xbc_lat, proj, dt_raw, dtb, alog, dskip, nw, h0t,
                        seq=seq_lat, nseq=n_lat, row0=t_ctx, write_state=False)
        u_ctx = _conformer(proj, conf_dw_w[l], conf_dw_b[l], conf_ln_w[l], conf_ln_b[l],
                           seg=seq_ctx, ntok=t_ctx, row0=0)
        u_lat = _conformer(proj, conf_dw_w[l], conf_dw_b[l], conf_ln_w[l], conf_ln_b[l],
                           seg=GRID_W, ntok=t_lat, row0=t_ctx)
        wr = jnp.pad(w_router[l], ((0, 0), (0, LANES - N_EXPERTS)))
        wr_hi = wr.astype(BF16)
        wr_lo = (wr - wr_hi.astype(F32)).astype(BF16)
        br = jnp.pad(b_router[l], (0, LANES - N_EXPERTS)).reshape(1, LANES)
        x1, h2i, er, topp, cnt = _mix_out(
            x_ctx, x_lat, y_ctx, y_lat, u_ctx, u_lat, proj, modr, w_ssd_out[l].astype(BF16),
            w_conf_out[l].astype(BF16), b_conf_out[l], b_gate[l], w_o[l].astype(BF16), norm_ffn[l],
            jnp.concatenate([wr_hi, wr_lo, wr_hi], axis=0), br)

        off, row_pair = _route(er[:, :TOP_K].reshape(-1), cnt[0, :N_EXPERTS])
        pbits = lax.bitcast_convert_type(topp[:, :TOP_K], jnp.int32).reshape(-1)
        moe_i = _moe_grouped(h2i, off, row_pair, pbits, w_gu[l], b_gu[l], w_down[l], b_down[l])
        last = l + 1 == depth
        x_ctx, x_lat = _residual(x1, moe_i, modr, norm_final if last else norm_ffn[l], normalize=last, t_ctx=t_ctx)
        new_states.append(st_ctx.reshape(n_ctx, 1, 2, N_HEADS, HEAD_DIM, D_STATE))

    y_prompt = x_ctx.reshape(n_ctx, seq_ctx, D_MODEL)
    y_sample = x_lat.reshape(n_lat, seq_lat, D_MODEL)
    new_state_ssm = new_states[0] if depth == 1 else jnp.concatenate(new_states, axis=1)
    return (y_prompt, y_sample, new_state_ssm)
```

```python
import functools

import jax
import jax.numpy as jnp
from jax import lax
from jax.experimental import pallas as pl
from jax.experimental.pallas import tpu as pltpu

F32 = jnp.float32
BF16 = jnp.bfloat16
U32 = jnp.uint32

D_MODEL = 1024
GRID_W = 64
D_INNER = 2 * D_MODEL
HEAD_DIM = 64
N_HEADS = D_INNER // HEAD_DIM
N_GROUPS = 8
HEADS_PER_GROUP = N_HEADS // N_GROUPS
D_STATE = 128
SSM_CONV = 5
CHUNK = 128
D_XBC = D_INNER + 2 * N_GROUPS * D_STATE
D_CONF = D_MODEL
CONF_KERNEL = 31
N_EXPERTS = 32
TOP_K = 4
D_EXPERT = D_MODEL
SWIGLU_LIMIT = 7.0
SWIGLU_ALPHA = 1.702
EPS = 1e-6

LANES = 128
UNIT = 256
GROUP_W = D_INNER // N_GROUPS
SSD_CHUNKS = 2
SSD_GROUP_UNROLL = 8
MIX_CHAINS = 2
MOE_TILE = 128
ROUTE_PAD = 2 * MOE_TILE
MOE_CAST_ROWS = 16
XW = D_MODEL // (2 * LANES)
AW = D_MODEL // LANES
RANK_SHIFT = 5
MIB = 1024 * 1024
V7X_VMEM_BYTES = 64 * MIB
VMEM_LIMIT = V7X_VMEM_BYTES - 2 * MIB

COL_Z = 0
COL_XBC = D_INNER
COL_GLU = D_XBC + D_INNER
COL_GATE = D_XBC + D_INNER + 2 * D_CONF
N_MAIN = D_XBC + D_INNER + 2 * D_CONF + 2 * D_MODEL


def _sigmoid(x):
    return 0.5 * jnp.tanh(0.5 * x) + 0.5


def _silu(x):
    h = 0.5 * x
    return h * jnp.tanh(h) + h


def _cparams(n_axes, vmem=None):
    return pltpu.CompilerParams(
        dimension_semantics=("arbitrary",) * n_axes,
        vmem_limit_bytes=vmem)


def _ada_kernel(c_ref, w_ref, b_ref, o_ref):
    s = _silu(c_ref[...])
    o_ref[...] = jnp.dot(s.astype(BF16), w_ref[...].astype(BF16),
                         preferred_element_type=F32) + b_ref[...]


def _ada_mod(cvec, w_ada, b_ada):
    n = w_ada.shape[1]
    tn = 1536
    return pl.pallas_call(
        _ada_kernel,
        grid=(n // tn,),
        in_specs=[pl.BlockSpec((8, D_MODEL), lambda j: (0, 0)),
                  pl.BlockSpec((D_MODEL, tn), lambda j: (0, j)),
                  pl.BlockSpec((1, tn), lambda j: (0, j))],
        out_specs=pl.BlockSpec((8, tn), lambda j: (0, j)),
        out_shape=jax.ShapeDtypeStruct((8, n), F32),
        compiler_params=_cparams(1, 40 * MIB),
        name="ada_mod",
    )(cvec, w_ada, b_ada.reshape(1, n))


def _wprep_kernel(w_ref, wn_ref, oa_ref, ob_ref, odt_ref, *, a_tiles):
    n = pl.program_id(0)
    tn = w_ref.shape[0]
    shift = 2 * N_HEADS

    def put(o_ref, c, rows_t):
        for j in range(D_MODEL // LANES):
            o_ref[j * LANES:(j + 1) * LANES, c * LANES:(c + 1) * LANES] = \
                rows_t[:, j * LANES:(j + 1) * LANES].T.astype(BF16)

    @pl.when(n < a_tiles)
    def _():
        for c in range(tn // LANES):
            put(oa_ref, c, w_ref[c * LANES:(c + 1) * LANES, :])

    @pl.when(n >= a_tiles)
    def _():
        for c in range(tn // LANES):
            lo = w_ref[c * LANES + shift:(c + 1) * LANES, :]
            hi = w_ref[(c + 1) * LANES:(c + 1) * LANES + shift, :] if c + 1 < tn // LANES else wn_ref[0:shift, :]
            put(ob_ref, c, jnp.concatenate([lo, hi], axis=0))

    @pl.when(n == a_tiles)
    def _():
        zeros = jnp.zeros((LANES - N_HEADS, D_MODEL), F32)
        for d in range(2):
            put(odt_ref, d, jnp.concatenate([w_ref[d * N_HEADS:(d + 1) * N_HEADS, :], zeros], axis=0))


def _in_proj_weights(w_in):
    tn = 2048
    n_a = D_INNER + D_XBC
    n_b = 2 * D_CONF + 2 * D_MODEL
    a_tiles = n_a // tn
    assert n_a % tn == 0 and n_b % tn == 0 and 2 * N_HEADS < LANES
    n_tiles = a_tiles + n_b // tn
    assert w_in.shape[1] == n_tiles * tn + 2 * N_HEADS
    w_t = w_in.T
    return pl.pallas_call(
        functools.partial(_wprep_kernel, a_tiles=a_tiles),
        grid=(n_tiles,),
        in_specs=[pl.BlockSpec((tn, D_MODEL), lambda n: (n, 0)),
                  pl.BlockSpec((LANES, D_MODEL), lambda n: ((n + 1) * (tn // LANES), 0))],
        out_specs=[pl.BlockSpec((D_MODEL, tn), lambda n: (0, jnp.minimum(n, a_tiles - 1))),
                   pl.BlockSpec((D_MODEL, tn), lambda n: (0, jnp.maximum(n - a_tiles, 0))),
                   pl.BlockSpec((D_MODEL, 2 * LANES), lambda n: (0, 0))],
        out_shape=[jax.ShapeDtypeStruct((D_MODEL, n_a), BF16),
                   jax.ShapeDtypeStruct((D_MODEL, n_b), BF16),
                   jax.ShapeDtypeStruct((D_MODEL, 2 * LANES), BF16)],
        compiler_params=_cparams(1, 48 * MIB),
        name="in_proj_weights",
    )(w_t, w_t)


def _inproj_kernel(xc_ref, xl_ref, nw_ref, sc_ref, sh_ref, wa_ref, wb_ref, wdt_ref, o_ref, dt_ref, h_scr, *,
                   ctx_tiles, a_tiles):
    @pl.when(pl.program_id(1) == 0)
    def _():
        xf = jnp.where(pl.program_id(0) < ctx_tiles, xc_ref[...], xl_ref[...])
        ms = jnp.mean(xf * xf, axis=-1, keepdims=True)
        hn = (xf * lax.rsqrt(ms + EPS)) * nw_ref[...]
        hn = hn * (1.0 + sc_ref[...]) + sh_ref[...]
        hb = hn.astype(BF16)
        h_scr[...] = hb
        dt_ref[...] = jnp.dot(hb, wdt_ref[...], preferred_element_type=F32)

    @pl.when(pl.program_id(1) < a_tiles)
    def _():
        o_ref[...] = jnp.dot(h_scr[...], wa_ref[...], preferred_element_type=F32).astype(BF16)

    @pl.when(pl.program_id(1) >= a_tiles)
    def _():
        o_ref[...] = jnp.dot(h_scr[...], wb_ref[...], preferred_element_type=F32).astype(BF16)


def _two_source_specs(tm, width, ctx_tiles, nargs):
    if nargs == 1:
        return [pl.BlockSpec((tm, width), lambda m: (jnp.minimum(m, ctx_tiles - 1), 0)),
                pl.BlockSpec((tm, width), lambda m: (jnp.maximum(m - ctx_tiles, 0), 0))]
    return [pl.BlockSpec((tm, width), lambda m, n: (jnp.minimum(m, ctx_tiles - 1), 0)),
            pl.BlockSpec((tm, width), lambda m, n: (jnp.maximum(m - ctx_tiles, 0), 0))]


def _in_proj(x_ctx, x_lat, norm_w, modr, w_a, w_b, w_dt):
    t_ctx = x_ctx.shape[0]
    t = t_ctx + x_lat.shape[0]
    tm, tn = 1024, 2048
    upt = tm // UNIT
    a_tiles = w_a.shape[1] // tn
    return pl.pallas_call(
        functools.partial(_inproj_kernel, ctx_tiles=t_ctx // tm, a_tiles=a_tiles),
        grid=(t // tm, N_MAIN // tn),
        in_specs=_two_source_specs(tm, D_MODEL, t_ctx // tm, 2) + [
                  pl.BlockSpec((1, D_MODEL), lambda m, n: (0, 0)),
                  pl.BlockSpec((None, None, 1, D_MODEL), lambda m, n: (m * upt, 1, 0, 0)),
                  pl.BlockSpec((None, None, 1, D_MODEL), lambda m, n: (m * upt, 0, 0, 0)),
                  pl.BlockSpec((D_MODEL, tn), lambda m, n: (0, jnp.minimum(n, a_tiles - 1))),
                  pl.BlockSpec((D_MODEL, tn), lambda m, n: (0, jnp.maximum(n - a_tiles, 0))),
                  pl.BlockSpec((D_MODEL, 2 * LANES), lambda m, n: (0, 0))],
        out_specs=[pl.BlockSpec((tm, tn), lambda m, n: (m, n)),
                   pl.BlockSpec((tm, 2 * LANES), lambda m, n: (m, 0))],
        out_shape=[jax.ShapeDtypeStruct((t, N_MAIN), BF16),
                   jax.ShapeDtypeStruct((t, 2 * LANES), F32)],
        scratch_shapes=[pltpu.VMEM((tm, D_MODEL), BF16)],
        compiler_params=_cparams(2, 48 * MIB),
        name="in_proj",
    )(x_ctx, x_lat, norm_w.reshape(1, D_MODEL), modr, modr, w_a, w_b, w_dt)


def _ssm_conv_kernel(x_ref, w_ref, b_ref, o_ref, pad_scr, *, seq):
    cb = x_ref.shape[1]
    pad_scr[0:8, :] = jnp.zeros((8, cb), F32)
    pad_scr[8 + seq:16 + seq, :] = jnp.zeros((8, cb), F32)
    pad_scr[8:8 + seq, :] = x_ref[...].astype(F32)
    half = SSM_CONV // 2
    rows, cw = 256, 512
    for c0 in range(0, cb, cw):
        cs = slice(c0, c0 + cw)
        for r0 in range(0, seq, rows):
            win = pad_scr[r0:r0 + rows + 16, cs]
            acc = jnp.broadcast_to(b_ref[:, cs], (rows, cw))
            for k in range(SSM_CONV):
                tap = win if k == half else pltpu.roll(win, (half - k) % (rows + 16), 0)
                acc = acc + w_ref[k:k + 1, cs] * tap[8:8 + rows]
            o_ref[r0:r0 + rows, cs] = _silu(acc).astype(BF16)


def _ssm_conv(proj, conv_w, conv_b, *, seq, nseq, row0):
    cb = 2048
    blk0 = row0 // seq
    return pl.pallas_call(
        functools.partial(_ssm_conv_kernel, seq=seq),
        grid=(nseq, D_XBC // cb),
        in_specs=[pl.BlockSpec((seq, cb), lambda s, j: (blk0 + s, COL_XBC // cb + j)),
                  pl.BlockSpec((SSM_CONV, cb), lambda s, j: (0, j)),
                  pl.BlockSpec((1, cb), lambda s, j: (0, j))],
        out_specs=pl.BlockSpec((seq, cb), lambda s, j: (s, j)),
        out_shape=jax.ShapeDtypeStruct((nseq * seq, D_XBC), BF16),
        scratch_shapes=[pltpu.VMEM((seq + 16, cb), F32)],
        compiler_params=_cparams(2, 40 * MIB),
        name="ssm_conv",
    )(proj, conv_w, conv_b.reshape(1, D_XBC))


def _split_bf16(v):
    hi = v.astype(BF16)
    lo = (v - hi.astype(F32)).astype(BF16)
    return jnp.concatenate([hi, lo], axis=1)


def _head_select_matrices():
    j = jnp.arange(2 * LANES, dtype=jnp.int32)[:, None] % LANES
    full = (j == (jnp.arange(N_HEADS * LANES, dtype=jnp.int32)[None, :] // LANES)).astype(BF16)
    exp = (j == (jnp.arange(D_INNER, dtype=jnp.int32)[None, :] // HEAD_DIM)).astype(BF16)
    return full, exp


def _ssd_kernel(*refs, nc, has_h0, write_state):
    (xs_ref, b_ref, c_ref, z_ref, dtr_ref, dtb_ref, alog_ref, dskip_ref, nw_ref, self_ref, sele_ref), rest = \
        refs[:11], refs[11:]
    if has_h0:
        h0_ref, rest = rest[0], rest[1:]
    y_ref, rest = rest[0], rest[1:]
    if write_state:
        st_ref, rest = rest[0], rest[1:]
    h_scr, ybuf, ychunk, colb_scr, wexp_scr, eexp_scr, texp_scr, rowq_scr = rest

    q = CHUNK
    phase = pl.program_id(1)
    c = pl.program_id(2)
    is_fwd = phase == 1
    c_eff = jnp.where(is_fwd, c, nc - 1 - c)

    @pl.when(c == 0)
    def _():
        if has_h0:
            for j in range(D_INNER // LANES):
                h_scr[:, j * LANES:(j + 1) * LANES] = h0_ref[j * LANES:(j + 1) * LANES, :].T
        else:
            h_scr[...] = jnp.zeros(h_scr.shape, F32)

    nrow = SSD_CHUNKS * q
    xdt = dtr_ref[...] + dtb_ref[...]
    dt = jnp.maximum(xdt, 0.0) + jnp.log(1.0 + jnp.exp(-jnp.abs(xdt)))
    a = -jnp.exp(alog_ref[...])
    adt = dt * a
    sgn = jnp.where(is_fwd, 1, -1)
    row = lax.broadcasted_iota(jnp.int32, (nrow, nrow), 0)
    col = lax.broadcasted_iota(jnp.int32, (nrow, nrow), 1)
    tri_blk = jnp.logical_and(row // q == col // q, (col - row) * sgn <= 0)
    cum2 = jnp.dot(tri_blk.astype(BF16), _split_bf16(adt), preferred_element_type=F32)
    cum = cum2[:, :LANES] + cum2[:, LANES:]
    tots = [jnp.where(is_fwd, cum[k * q + q - 1:k * q + q, :], cum[k * q:k * q + 1, :]) for k in range(SSD_CHUNKS)]
    tot_rows = jnp.concatenate([jnp.broadcast_to(t, (q, LANES)) for t in tots], axis=0)
    rowq = (cum - jnp.log(dt)).T
    for k in range(SSD_CHUNKS):
        rowq_scr[k] = rowq[:, k * q:(k + 1) * q]
    wdec = dt * jnp.exp(tot_rows - cum)
    eo = jnp.exp(cum)
    etot = jnp.concatenate([jnp.broadcast_to(jnp.exp(t), (8, LANES)) for t in tots], axis=0)
    colb_scr[...] = jnp.dot(_split_bf16(cum), self_ref[...], preferred_element_type=F32)
    expanded = jnp.dot(jnp.concatenate([_split_bf16(wdec), _split_bf16(eo), _split_bf16(etot)], axis=0),
                       sele_ref[...], preferred_element_type=F32)
    wexp_scr[...] = expanded[0:nrow]
    eexp_scr[...] = expanded[nrow:2 * nrow]
    for k in range(SSD_CHUNKS):
        texp_scr[k] = expanded[2 * nrow + 8 * k:2 * nrow + 8 * (k + 1)]
    lrow = lax.broadcasted_iota(jnp.int32, (q, q), 0)
    lcol = lax.broadcasted_iota(jnp.int32, (q, q), 1)
    tri = (lcol - lrow) * sgn <= 0
    lane_g = lax.broadcasted_iota(jnp.int32, (1, GROUP_W), 1)
    head_mask = [(lane_g // HEAD_DIM == hh).astype(BF16) for hh in range(HEADS_PER_GROUP)]
    neg_inf = jnp.float32(-jnp.inf)

    def group_body(g, carry):
        gs = pl.ds(pl.multiple_of(g * GROUP_W, GROUP_W), GROUP_W)
        ns = pl.ds(pl.multiple_of(g * D_STATE, D_STATE), D_STATE)
        for k in range(SSD_CHUNKS):
            ci = jnp.where(is_fwd, k, SSD_CHUNKS - 1 - k)
            r0 = pl.multiple_of(ci * q, q)
            rs = pl.ds(r0, q)
            bg = b_ref[rs, ns]
            cg = c_ref[rs, ns]
            scores = lax.dot_general(cg, bg, (((1,), (1,)), ((), ())), preferred_element_type=F32)
            xs_g = xs_ref[rs, gs]
            ms, xb = [], []
            for hh in range(HEADS_PER_GROUP):
                h = HEADS_PER_GROUP * g + hh
                colb = colb_scr[rs, pl.ds(pl.multiple_of(h * LANES, LANES), LANES)]
                seg = jnp.where(tri, colb - rowq_scr[ci, pl.ds(h, 1), :], neg_inf)
                ms.append((scores * jnp.exp(seg)).astype(BF16))
                xb.append(xs_g * head_mask[hh])
            y_diag = jnp.dot(jnp.concatenate(ms, axis=1), jnp.concatenate(xb, axis=0),
                             preferred_element_type=F32)
            xd = (xs_g.astype(F32) * wexp_scr[rs, gs]).astype(BF16)
            bt = bg.astype(F32).T.astype(BF16)
            st = jnp.dot(bt, xd, preferred_element_type=F32)
            hg = h_scr[:, gs]
            y_off = jnp.dot(cg, hg.astype(BF16), preferred_element_type=F32) * eexp_scr[rs, gs]
            ychunk[rs, gs] = y_diag + y_off
            h_scr[:, gs] = hg * texp_scr[ci, 0:1, gs] + st
        return carry

    lax.fori_loop(0, N_GROUPS, group_body, 0, unroll=SSD_GROUP_UNROLL)

    rows = pl.ds(pl.multiple_of(c_eff * nrow, nrow), nrow)

    @pl.when(jnp.logical_not(is_fwd))
    def _():
        ybuf[rows, :] = ychunk[...]

    @pl.when(is_fwd)
    def _():
        zf = z_ref[...].astype(F32)
        yt = ychunk[...] + ybuf[rows, :] + xs_ref[...].astype(F32) * dskip_ref[...]
        yz = yt * _silu(zf)
        for g in range(N_GROUPS):
            gs = slice(g * GROUP_W, (g + 1) * GROUP_W)
            blk = yz[:, gs]
            ms = jnp.mean(blk * blk, axis=-1, keepdims=True)
            y_ref[:, gs] = (blk * lax.rsqrt(ms + EPS) * nw_ref[:, gs]).astype(BF16)

    if write_state:
        @pl.when(c == nc - 1)
        def _():
            for j in range(D_INNER // LANES):
                st_ref[j * LANES:(j + 1) * LANES, :] = h_scr[:, j * LANES:(j + 1) * LANES].T


def _ssd(xbc_c, proj, dt_raw, dt_bias, a_log, d_skip, norm_w, h0t, *, seq, nseq, row0, write_state):
    rb = SSD_CHUNKS * CHUNK
    nc = seq // rb
    blk0 = row0 // rb
    has_h0 = h0t is not None
    assert seq % rb == 0 and row0 % rb == 0

    def tok(s, p, c):
        return s * nc + p * c + (1 - p) * (nc - 1 - c)

    in_specs = [
        pl.BlockSpec((rb, D_INNER), lambda s, p, c: (tok(s, p, c), 0)),
        pl.BlockSpec((rb, N_GROUPS * D_STATE), lambda s, p, c: (tok(s, p, c), 2)),
        pl.BlockSpec((rb, N_GROUPS * D_STATE), lambda s, p, c: (tok(s, p, c), 3)),
        pl.BlockSpec((rb, D_INNER), lambda s, p, c: (blk0 + s * nc + p * c, COL_Z // D_INNER)),
        pl.BlockSpec((rb, LANES), lambda s, p, c: (blk0 + tok(s, p, c), 1 - p)),
        pl.BlockSpec((None, 1, LANES), lambda s, p, c: (1 - p, 0, 0)),
        pl.BlockSpec((None, 1, LANES), lambda s, p, c: (1 - p, 0, 0)),
        pl.BlockSpec((1, D_INNER), lambda s, p, c: (0, 0)),
        pl.BlockSpec((1, D_INNER), lambda s, p, c: (0, 0)),
        pl.BlockSpec((2 * LANES, N_HEADS * LANES), lambda s, p, c: (0, 0)),
        pl.BlockSpec((2 * LANES, D_INNER), lambda s, p, c: (0, 0)),
    ]
    sel_full, sel_exp = _head_select_matrices()
    args = [xbc_c, xbc_c, xbc_c, proj, dt_raw, dt_bias, a_log, d_skip, norm_w, sel_full, sel_exp]
    if has_h0:
        in_specs.append(pl.BlockSpec((None, None, D_INNER, D_STATE), lambda s, p, c: (s, 1 - p, 0, 0)))
        args.append(h0t)
    out_specs = [pl.BlockSpec((rb, D_INNER), lambda s, p, c: (s * nc + p * c, 0))]
    out_shape = [jax.ShapeDtypeStruct((nseq * seq, D_INNER), BF16)]
    if write_state:
        out_specs.append(pl.BlockSpec((None, None, D_INNER, D_STATE), lambda s, p, c: (s, 1 - p, 0, 0)))
        out_shape.append(jax.ShapeDtypeStruct((nseq, 2, D_INNER, D_STATE), F32))
    return pl.pallas_call(
        functools.partial(_ssd_kernel, nc=nc, has_h0=has_h0, write_state=write_state),
        grid=(nseq, 2, nc),
        in_specs=in_specs,
        out_specs=out_specs,
        out_shape=out_shape,
        scratch_shapes=[pltpu.VMEM((D_STATE, D_INNER), F32),
                        pltpu.VMEM((seq, D_INNER), F32),
                        pltpu.VMEM((rb, D_INNER), F32),
                        pltpu.VMEM((rb, N_HEADS * LANES), F32),
                        pltpu.VMEM((rb, D_INNER), F32),
                        pltpu.VMEM((rb, D_INNER), F32),
                        pltpu.VMEM((SSD_CHUNKS, 8, D_INNER), F32),
                        pltpu.VMEM((SSD_CHUNKS, LANES, CHUNK), F32)],
        compiler_params=_cparams(3, 48 * MIB),
        name="ssd_scan",
    )(*args)


def _conf_kernel(glu_ref, w_ref, b_ref, lnw_ref, lnb_ref, o_ref, pad_scr, sh_scr, conv_scr, *, seg):
    rows = glu_ref.shape[0]
    nseg = rows // seg
    half = CONF_KERNEL // 2
    front = 16
    span = seg + 24
    a = glu_ref[:, :D_CONF].astype(F32)
    b = glu_ref[:, D_CONF:].astype(F32)
    u = a * _sigmoid(b)
    for i in range(nseg):
        pad_scr[i, 0:front, :] = jnp.zeros((front, D_CONF), F32)
        pad_scr[i, front + seg:front + seg + 16, :] = jnp.zeros((16, D_CONF), F32)
        pad_scr[i, front:front + seg, :] = u[i * seg:(i + 1) * seg, :]
    padded = seg + 32
    for i in range(nseg):
        for cbi in range(D_CONF // LANES):
            cs = slice(cbi * LANES, (cbi + 1) * LANES)
            seg_pad = pad_scr[i, :, cs]
            for s in range(8):
                sh_scr[s, i, :, cs] = (seg_pad if s == 0 else pltpu.roll(seg_pad, padded - s, 0))[0:span]
    rb = 64
    for i in range(nseg):
        for cbi in range(D_CONF // LANES):
            cs = slice(cbi * LANES, (cbi + 1) * LANES)
            for r0 in range(0, seg, rb):
                acc = jnp.broadcast_to(b_ref[:, cs], (rb, LANES))
                for k in range(CONF_KERNEL):
                    start = front + r0 + k - half
                    al = start - start % 8
                    acc = acc + w_ref[k:k + 1, cs] * sh_scr[start % 8, i, al:al + rb, cs]
                conv_scr[i * seg + r0:i * seg + r0 + rb, cs] = acc
    v = conv_scr[...]
    mu = jnp.mean(v, axis=-1, keepdims=True)
    vc = v - mu
    var = jnp.mean(vc * vc, axis=-1, keepdims=True)
    ln = (vc * lax.rsqrt(var + EPS)) * lnw_ref[...] + lnb_ref[...]
    o_ref[...] = _silu(ln).astype(BF16)


def _conformer(proj, dw_w, dw_b, ln_w, ln_b, *, seg, ntok, row0):
    rows = UNIT
    blk0 = row0 // rows
    return pl.pallas_call(
        functools.partial(_conf_kernel, seg=seg),
        grid=(ntok // rows,),
        in_specs=[pl.BlockSpec((rows, 2 * D_CONF), lambda i: (blk0 + i, COL_GLU // (2 * D_CONF))),
                  pl.BlockSpec((CONF_KERNEL, D_CONF), lambda i: (0, 0)),
                  pl.BlockSpec((1, D_CONF), lambda i: (0, 0)),
                  pl.BlockSpec((1, D_CONF), lambda i: (0, 0)),
                  pl.BlockSpec((1, D_CONF), lambda i: (0, 0))],
        out_specs=pl.BlockSpec((rows, D_CONF), lambda i: (i, 0)),
        out_shape=jax.ShapeDtypeStruct((ntok, D_CONF), BF16),
        scratch_shapes=[pltpu.VMEM((rows // seg, seg + 32, D_CONF), F32),
                        pltpu.VMEM((8, rows // seg, seg + 24, D_CONF), F32),
                        pltpu.VMEM((rows, D_CONF), F32)],
        compiler_params=_cparams(1, 40 * MIB),
        name="conformer_conv",
    )(proj, dw_w, dw_b.reshape(1, D_CONF), ln_w.reshape(1, D_CONF), ln_b.reshape(1, D_CONF))


def _pack_halves(x):
    outs = []
    for cb in range(x.shape[1] // (2 * LANES)):
        hi = x[:, cb * 2 * LANES:cb * 2 * LANES + LANES].astype(BF16).astype(F32)
        lo = x[:, cb * 2 * LANES + LANES:(cb + 1) * 2 * LANES].astype(BF16).astype(F32)
        hw = lax.bitcast_convert_type(hi, U32)
        lw = jnp.right_shift(lax.bitcast_convert_type(lo, U32), jnp.uint32(16))
        outs.append(jnp.bitwise_or(hw, lw))
    return jnp.concatenate(outs, axis=1)


def _mix_kernel(xc_ref, xl_ref, yc_ref, yl_ref, uc_ref, ul_ref, gate_ref, g1_ref, sc2_ref, sh2_ref,
                wssd_ref, wconf_ref, bconf_ref, bgate_ref, wo_ref, nffn_ref, wr_ref, br_ref,
                x1_ref, h2i_ref, er_ref, topp_ref, cnt_ref, cnt_scr, *, ctx_tiles):
    @pl.when(pl.program_id(0) == 0)
    def _():
        cnt_scr[...] = jnp.zeros(cnt_scr.shape, F32)

    is_ctx = pl.program_id(0) < ctx_tiles
    rows = x1_ref.shape[0]
    sub = rows // MIX_CHAINS
    lane = lax.broadcasted_iota(jnp.int32, (sub, LANES), 1)
    lane_f = lane.astype(F32)
    neg_inf = jnp.float32(-jnp.inf)

    def chain(c):
        rs = slice(c * sub, (c + 1) * sub)
        y_in = jnp.where(is_ctx, yc_ref[rs, :], yl_ref[rs, :])
        u_in = jnp.where(is_ctx, uc_ref[rs, :], ul_ref[rs, :])
        o_ssd = jnp.dot(y_in, wssd_ref[...], preferred_element_type=F32)
        o_conf = jnp.dot(u_in, wconf_ref[...], preferred_element_type=F32) + bconf_ref[...]
        gates = _sigmoid(gate_ref[rs, :].astype(F32) + bgate_ref[...])
        merged = gates[:, :D_MODEL] * o_ssd + gates[:, D_MODEL:] * o_conf
        out = jnp.dot(merged.astype(BF16), wo_ref[...], preferred_element_type=F32)
        x1 = jnp.where(is_ctx, xc_ref[rs, :], xl_ref[rs, :]) + g1_ref[...] * out
        x1_ref[rs, :] = x1
        ms = jnp.mean(x1 * x1, axis=-1, keepdims=True)
        h2 = (x1 * lax.rsqrt(ms + EPS)) * nffn_ref[...]
        h2 = h2 * (1.0 + sc2_ref[...]) + sh2_ref[...]
        packed = _pack_halves(h2)
        for qd in range(XW):
            h2i_ref[pl.ds(c * sub * XW + qd, sub, stride=XW), :] = packed[:, qd * LANES:(qd + 1) * LANES]
        h_hi = h2.astype(BF16)
        h_lo = (h2 - h_hi.astype(F32)).astype(BF16)
        logits = jnp.dot(jnp.concatenate([h_hi, h_hi, h_lo], axis=1), wr_ref[...],
                         preferred_element_type=F32) + br_ref[...]
        work = jnp.where(lane < N_EXPERTS, logits, neg_inf)
        vals, idxs = [], []
        for _ in range(TOP_K):
            m = jnp.max(work, axis=-1, keepdims=True)
            idx = jnp.min(jnp.where(work == m, lane_f, jnp.float32(LANES)), axis=-1, keepdims=True)
            vals.append(m)
            idxs.append(idx)
            work = jnp.where(lane_f == idx, neg_inf, work)
        es = [jnp.exp(v - vals[0]) for v in vals]
        denom = es[0] + es[1] + es[2] + es[3]
        member = jnp.zeros((sub, LANES), F32)
        topp = jnp.zeros((sub, LANES), F32)
        for k in range(TOP_K):
            member = member + jnp.where(lane_f == idxs[k], 1.0, 0.0)
            topp = jnp.where(lane == k, es[k] / denom, topp)
        topp_ref[rs, :] = topp
        return idxs, member

    results = [chain(c) for c in range(MIX_CHAINS)]
    member = jnp.concatenate([m for _, m in results], axis=0)
    r_i = lax.broadcasted_iota(jnp.int32, (rows, rows), 0)
    c_i = lax.broadcasted_iota(jnp.int32, (rows, rows), 1)
    earlier = jnp.where(c_i < r_i, 1.0, 0.0).astype(BF16)
    rank = jnp.dot(earlier, member.astype(BF16), preferred_element_type=F32) + cnt_scr[...]
    cnt = cnt_scr[...] + jnp.sum(member, axis=0, keepdims=True)
    cnt_scr[...] = cnt
    cnt_ref[...] = jnp.broadcast_to(cnt, cnt_ref.shape).astype(jnp.int32)
    for c, (idxs, _) in enumerate(results):
        rs = slice(c * sub, (c + 1) * sub)
        er = jnp.zeros((sub, LANES), F32)
        for k in range(TOP_K):
            rank_k = jnp.sum(jnp.where(lane_f == idxs[k], rank[rs, :], 0.0), axis=-1, keepdims=True)
            er = jnp.where(lane == k, idxs[k] + N_EXPERTS * rank_k, er)
        er_ref[rs, :] = er.astype(jnp.int32)


def _mix_out(x_ctx, x_lat, y_ctx, y_lat, u_ctx, u_lat, proj, modr, w_ssd, w_conf, b_conf, b_gate, w_o, norm_ffn,
             w_router3, b_router):
    t_ctx = x_ctx.shape[0]
    t = t_ctx + x_lat.shape[0]
    tm = 512
    upt = tm // UNIT
    ctx_tiles = t_ctx // tm
    full = lambda shape: pl.BlockSpec(shape, lambda m: (0,) * len(shape), pipeline_mode=pl.Buffered(1))
    mod = lambda which: pl.BlockSpec((None, None, 1, D_MODEL), lambda m: (m * upt, which, 0, 0))
    return pl.pallas_call(
        functools.partial(_mix_kernel, ctx_tiles=ctx_tiles),
        grid=(t // tm,),
        in_specs=_two_source_specs(tm, D_MODEL, ctx_tiles, 1) + _two_source_specs(tm, D_INNER, ctx_tiles, 1)
                 + _two_source_specs(tm, D_CONF, ctx_tiles, 1) + [
                  pl.BlockSpec((tm, 2 * D_MODEL), lambda m: (m, COL_GATE // (2 * D_MODEL))),
                  mod(2), mod(4), mod(3),
                  full((D_INNER, D_MODEL)), full((D_CONF, D_MODEL)), full((1, D_MODEL)),
                  full((1, 2 * D_MODEL)), full((D_MODEL, D_MODEL)), full((1, D_MODEL)),
                  full((3 * D_MODEL, LANES)), full((1, LANES))],
        out_specs=[pl.BlockSpec((tm, D_MODEL), lambda m: (m, 0)),
                   pl.BlockSpec((tm * XW, LANES), lambda m: (m, 0)),
                   pl.BlockSpec((tm, LANES), lambda m: (m, 0)),
                   pl.BlockSpec((tm, LANES), lambda m: (m, 0)),
                   pl.BlockSpec((8, LANES), lambda m: (0, 0))],
        out_shape=[jax.ShapeDtypeStruct((t, D_MODEL), F32),
                   jax.ShapeDtypeStruct((t * XW, LANES), U32),
                   jax.ShapeDtypeStruct((t, LANES), jnp.int32),
                   jax.ShapeDtypeStruct((t, LANES), F32),
                   jax.ShapeDtypeStruct((8, LANES), jnp.int32)],
        scratch_shapes=[pltpu.VMEM((1, LANES), F32)],
        compiler_params=_cparams(1, 56 * MIB),
        name="mix_out_router",
    )(x_ctx, x_lat, y_ctx, y_lat, u_ctx, u_lat, proj, modr, modr, modr, w_ssd, w_conf, b_conf.reshape(1, D_MODEL),
      b_gate.reshape(1, 2 * D_MODEL), w_o, norm_ffn.reshape(1, D_MODEL), w_router3, b_router)


def _moe_kernel(off_ref, pair_ref, pb_ref, h2i_ref, wgu_hbm, wd_hbm, bgu_ref, bd_ref, out_ref,
                acc, x_scr, y_scr, stage_gu, stage_d, wgu_b, wd_b, wsem, osem):
    e = pl.program_id(0)
    tr = MOE_TILE

    def weight_copies(ex):
        return (pltpu.make_async_copy(wgu_hbm.at[ex], stage_gu, wsem.at[0]),
                pltpu.make_async_copy(wd_hbm.at[ex], stage_d, wsem.at[1]))

    @pl.when(e == 0)
    def _():
        acc[...] = jnp.zeros(acc.shape, F32)
        for cp in weight_copies(0):
            cp.start()

    for cp in weight_copies(e):
        cp.wait()

    start = off_ref[e]
    end = off_ref[e + 1]
    ntiles = (end - start + tr - 1) // tr
    bgu = bgu_ref[...]
    bd = bd_ref[...]
    cast_steps = D_MODEL // MOE_CAST_ROWS
    rows_per_step = tr // cast_steps

    def to_bf16(i, c):
        rows = pl.ds(pl.multiple_of(i * MOE_CAST_ROWS, MOE_CAST_ROWS), MOE_CAST_ROWS)
        wgu_b[rows, :] = stage_gu[rows, :].astype(BF16)
        wd_b[rows, :] = stage_d[rows, :].astype(BF16)
        for rr in range(rows_per_step):
            r = i * rows_per_step + rr
            src = jnp.bitwise_and(pair_ref[start + r], -TOP_K)
            x_scr[0, pl.ds(pl.multiple_of(r * XW, XW), XW), :] = h2i_ref[pl.ds(pl.multiple_of(src, XW), XW), :]
        return c

    lax.fori_loop(0, cast_steps, to_bf16, 0, unroll=2)

    @pl.when(e + 1 < N_EXPERTS)
    def _():
        for cp in weight_copies(e + 1):
            cp.start()

    def gather_rows(base, slot, r0, n):
        for r in range(r0, r0 + n):
            src = jnp.bitwise_and(pair_ref[base + r], -TOP_K)
            x_scr[slot, pl.ds(r * XW, XW), :] = h2i_ref[pl.ds(pl.multiple_of(src, XW), XW), :]

    def scatter_rows(base, slot, r0):
        dsts, ps = [], []
        for rr in range(8):
            pair = pair_ref[base + r0 + rr]
            dsts.append(pl.multiple_of(jnp.bitwise_and(pair, -TOP_K) * (AW // XW), AW))
            ps.append(lax.bitcast_convert_type(pb_ref[pair], F32))
        olds = [acc[pl.ds(dsts[rr], AW), :] for rr in range(8)]
        news = [olds[rr] + ps[rr] * y_scr[slot, pl.ds((r0 + rr) * AW, AW), :] for rr in range(8)]
        for rr in range(8):
            acc[pl.ds(dsts[rr], AW), :] = news[rr]

    def tile_step(ti, first):
        slot = jnp.bitwise_and(ti, 1)
        other = 1 - slot
        base = start + ti * tr
        parts = []
        for qd in range(XW):
            w = x_scr[slot, pl.ds(qd, tr, stride=XW), :]
            parts.append(lax.bitcast_convert_type(jnp.bitwise_and(w, jnp.uint32(0xFFFF0000)), F32))
            parts.append(lax.bitcast_convert_type(jnp.left_shift(w, jnp.uint32(16)), F32))
        x = jnp.concatenate(parts, axis=1).astype(BF16)
        if not first:
            for r0 in range(0, tr, 8):
                scatter_rows(base - tr, other, r0)
        gather_rows(base + tr, other, 0, tr)
        gu = jnp.dot(x, wgu_b[...], preferred_element_type=F32) + bgu
        g = jnp.minimum(gu[:, :D_EXPERT], SWIGLU_LIMIT)
        u = jnp.clip(gu[:, D_EXPERT:], -SWIGLU_LIMIT, SWIGLU_LIMIT)
        act = (u + 1.0) * g * _sigmoid(SWIGLU_ALPHA * g)
        y = jnp.dot(act.astype(BF16), wd_b[...], preferred_element_type=F32) + bd
        for qd in range(AW):
            y_scr[slot, pl.ds(qd, tr, stride=AW), :] = y[:, qd * LANES:(qd + 1) * LANES]

    @pl.when(ntiles > 0)
    def _():
        tile_step(0, True)

    def tile_body(ti, carry):
        tile_step(ti, False)
        return carry

    lax.fori_loop(1, ntiles, tile_body, 0)

    @pl.when(ntiles > 0)
    def _():
        last = ntiles - 1
        lslot = jnp.bitwise_and(last, 1)
        lbase = start + last * tr

        def last_scatter(r8, c2):
            dsts, ps = [], []
            for rr in range(8):
                r = lbase + r8 * 8 + rr
                pair = pair_ref[r]
                dsts.append(pl.multiple_of(jnp.bitwise_and(pair, -TOP_K) * (AW // XW), AW))
                pw = lax.bitcast_convert_type(pb_ref[pair], F32)
                ps.append(jnp.where(r < end, pw, 0.0))
            olds = [acc[pl.ds(dsts[rr], AW), :] for rr in range(8)]
            news = [olds[rr] + ps[rr] * y_scr[lslot, pl.ds(pl.multiple_of((r8 * 8 + rr) * AW, AW), AW), :]
                    for rr in range(8)]
            for rr in range(8):
                acc[pl.ds(dsts[rr], AW), :] = news[rr]
            return c2

        lax.fori_loop(0, tr // 8, last_scatter, 0)

    @pl.when(e == N_EXPERTS - 1)
    def _():
        cp = pltpu.make_async_copy(acc, out_ref, osem)
        cp.start()
        cp.wait()


def _moe_grouped(h2i, off, row_pair, pbits, w_gu, b_gu, w_down, b_down):
    t = h2i.shape[0] // XW
    assert D_EXPERT == D_MODEL
    grid_spec = pltpu.PrefetchScalarGridSpec(
        num_scalar_prefetch=3,
        grid=(N_EXPERTS,),
        in_specs=[pl.BlockSpec((t * XW, LANES), lambda e, *_: (0, 0), pipeline_mode=pl.Buffered(1)),
                  pl.BlockSpec(memory_space=pl.ANY),
                  pl.BlockSpec(memory_space=pl.ANY),
                  pl.BlockSpec((None, 1, 2 * D_EXPERT), lambda e, *_: (e, 0, 0)),
                  pl.BlockSpec((None, 1, D_MODEL), lambda e, *_: (e, 0, 0))],
        out_specs=pl.BlockSpec(memory_space=pl.ANY),
        scratch_shapes=[pltpu.VMEM((t * AW, LANES), F32),
                        pltpu.VMEM((2, MOE_TILE * XW, LANES), U32),
                        pltpu.VMEM((2, MOE_TILE * AW, LANES), F32),
                        pltpu.VMEM((D_MODEL, 2 * D_EXPERT), F32),
                        pltpu.VMEM((D_EXPERT, D_MODEL), F32),
                        pltpu.VMEM((D_MODEL, 2 * D_EXPERT), BF16),
                        pltpu.VMEM((D_EXPERT, D_MODEL), BF16),
                        pltpu.SemaphoreType.DMA((2,)),
                        pltpu.SemaphoreType.DMA(())])
    return pl.pallas_call(
        _moe_kernel,
        grid_spec=grid_spec,
        out_shape=jax.ShapeDtypeStruct((t * AW, LANES), F32),
        compiler_params=_cparams(1, VMEM_LIMIT),
        name="moe_grouped",
    )(off, row_pair, pbits, h2i, w_gu, w_down, b_gu.reshape(N_EXPERTS, 1, 2 * D_EXPERT),
      b_down.reshape(N_EXPERTS, 1, D_MODEL))


def _residual_kernel(x1_ref, moe_ref, g2_ref, nw_ref, oc_ref, ol_ref, *, normalize, ctx_tiles):
    tm = x1_ref.shape[0]
    moe = jnp.concatenate([moe_ref[pl.ds(qd, tm, stride=AW), :] for qd in range(AW)], axis=1)
    x2 = x1_ref[...] + g2_ref[...] * moe
    if normalize:
        ms = jnp.mean(x2 * x2, axis=-1, keepdims=True)
        x2 = (x2 * lax.rsqrt(ms + EPS)) * nw_ref[...]

    @pl.when(pl.program_id(0) < ctx_tiles)
    def _():
        oc_ref[...] = x2

    @pl.when(pl.program_id(0) >= ctx_tiles)
    def _():
        ol_ref[...] = x2


def _residual(x1, moe_i, modr, norm_w, *, normalize, t_ctx):
    t = x1.shape[0]
    tm = 512
    upt = tm // UNIT
    ctx_tiles = t_ctx // tm
    return pl.pallas_call(
        functools.partial(_residual_kernel, normalize=normalize, ctx_tiles=ctx_tiles),
        grid=(t // tm,),
        in_specs=[pl.BlockSpec((tm, D_MODEL), lambda m: (m, 0)),
                  pl.BlockSpec((tm * AW, LANES), lambda m: (m, 0)),
                  pl.BlockSpec((None, None, 1, D_MODEL), lambda m: (m * upt, 5, 0, 0)),
                  pl.BlockSpec((1, D_MODEL), lambda m: (0, 0))],
        out_specs=_two_source_specs(tm, D_MODEL, ctx_tiles, 1),
        out_shape=[jax.ShapeDtypeStruct((t_ctx, D_MODEL), F32),
                   jax.ShapeDtypeStruct((t - t_ctx, D_MODEL), F32)],
        compiler_params=_cparams(1),
        name="ffn_residual_norm",
    )(x1, moe_i, modr, norm_w.reshape(1, D_MODEL))


def _route_kernel(pos_ref, pair_ref, *, n_pairs):
    def pad(i, c):
        pair_ref[n_pairs + i] = 0
        return c

    lax.fori_loop(0, ROUTE_PAD, pad, 0)

    def place(i8, c):
        for ii in range(8):
            i = i8 * 8 + ii
            pair_ref[pos_ref[i]] = i
        return c

    lax.fori_loop(0, n_pairs // 8, place, 0)


def _route(er_flat, cnt):
    n = er_flat.shape[0]
    off_end = jnp.cumsum(cnt)
    off = off_end - cnt
    expert = jnp.bitwise_and(er_flat, N_EXPERTS - 1)
    onehot = expert[:, None] == jnp.arange(N_EXPERTS, dtype=jnp.int32)[None, :]
    pos = lax.shift_right_logical(er_flat, RANK_SHIFT) + jnp.sum(jnp.where(onehot, off[None, :], 0), axis=1)
    smem = pl.BlockSpec(memory_space=pltpu.SMEM)
    row_pair = pl.pallas_call(
        functools.partial(_route_kernel, n_pairs=n),
        in_specs=[smem],
        out_specs=smem,
        out_shape=jax.ShapeDtypeStruct((n + ROUTE_PAD,), jnp.int32),
        name="moe_route",
    )(pos)
    return jnp.concatenate([off, jnp.broadcast_to(off_end[-1:], (8,))]), row_pair


def kernel(x_prompt, x_sample, state_ssm, c, c_ctx, w_ada, b_ada, norm_mix, norm_ffn, w_in, ssm_conv_w, ssm_conv_b, dt_bias, a_log, d_skip, ssm_norm_w, w_ssd_out, conf_dw_w, conf_dw_b, conf_ln_w, conf_ln_b, w_conf_out, b_conf_out, b_gate, w_o, w_router, b_router, w_gu, b_gu, w_down, b_down, norm_final):
    n_ctx, seq_ctx, _ = x_prompt.shape
    n_lat, seq_lat, _ = x_sample.shape
    depth = w_in.shape[0]
    t_ctx, t_lat = n_ctx * seq_ctx, n_lat * seq_lat
    t_all = t_ctx + t_lat
    assert seq_ctx == UNIT and seq_lat % 1024 == 0 and t_ctx % 1024 == 0
    assert n_lat + 1 <= 8 and seq_lat % GRID_W == 0 and UNIT % GRID_W == 0

    x_ctx, x_lat = x_prompt.reshape(t_ctx, D_MODEL), x_sample.reshape(t_lat, D_MODEL)
    cvec =jnp.concatenate([c_ctx[None, :], c, jnp.zeros((8 - 1 - n_lat, D_MODEL), F32)], axis=0)
    unit_row = jnp.concatenate([jnp.zeros((t_ctx // UNIT,), jnp.int32),
                                1 + jnp.arange(t_lat // UNIT, dtype=jnp.int32) // (seq_lat // UNIT)])
    assert XW == TOP_K and N_EXPERTS == 1 << RANK_SHIFT

    new_states = []
    for l in range(depth):
        mod = _ada_mod(cvec, w_ada[l], b_ada[l])
        modr = mod[unit_row].reshape(t_all // UNIT, 6, 1, D_MODEL)

        w_a, w_b, w_dt = _in_proj_weights(w_in[l])
        proj, dt_raw = _in_proj(x_ctx, x_lat, norm_mix[l], modr, w_a, w_b, w_dt)

        pad_h = lambda v: jnp.pad(v.reshape(2, 1, N_HEADS), ((0, 0), (0, 0), (0, LANES - N_HEADS)))
        dtb, alog = pad_h(dt_bias[l]), pad_h(a_log[l])
        dskip = jnp.repeat(d_skip[l], HEAD_DIM).reshape(1, D_INNER)
        nw = ssm_norm_w[l].reshape(1, D_INNER)
        h0t = state_ssm[:, l].astype(F32).reshape(n_lat, 2, D_INNER, D_STATE)

        xbc_ctx = _ssm_conv(proj, ssm_conv_w[l], ssm_conv_b[l], seq=seq_ctx, nseq=n_ctx, row0=0)
        xbc_lat = _ssm_conv(proj, ssm_conv_w[l], ssm_conv_b[l], seq=seq_lat, nseq=n_lat, row0=t_ctx)
        y_ctx, st_ctx = _ssd(xbc_ctx, proj, dt_raw, dtb, alog, dskip, nw, None,
                             seq=seq_ctx, nseq=n_ctx, row0=0, write_state=True)
        (y_lat,) = _ssd(xbc_lat, proj, dt_raw, dtb, alog, dskip, nw, h0t,
                        seq=seq_lat, nseq=n_lat, row0=t_ctx, write_state=False)
        u_ctx = _conformer(proj, conf_dw_w[l], conf_dw_b[l], conf_ln_w[l], conf_ln_b[l],
                           seg=seq_ctx, ntok=t_ctx, row0=0)
        u_lat = _conformer(proj, conf_dw_w[l], conf_dw_b[l], conf_ln_w[l], conf_ln_b[l],
                           seg=GRID_W, ntok=t_lat, row0=t_ctx)
        wr = jnp.pad(w_router[l], ((0, 0), (0, LANES - N_EXPERTS)))
        wr_hi = wr.astype(BF16)
        wr_lo = (wr - wr_hi.astype(F32)).astype(BF16)
        br = jnp.pad(b_router[l], (0, LANES - N_EXPERTS)).reshape(1, LANES)
        x1, h2i, er, topp, cnt = _mix_out(
            x_ctx, x_lat, y_ctx, y_lat, u_ctx, u_lat, proj, modr, w_ssd_out[l].astype(BF16),
            w_conf_out[l].astype(BF16), b_conf_out[l], b_gate[l], w_o[l].astype(BF16), norm_ffn[l],
            jnp.concatenate([wr_hi, wr_lo, wr_hi], axis=0), br)

        off, row_pair = _route(er[:, :TOP_K].reshape(-1), cnt[0, :N_EXPERTS])
        pbits = lax.bitcast_convert_type(topp[:, :TOP_K], jnp.int32).reshape(-1)
        moe_i = _moe_grouped(h2i, off, row_pair, pbits, w_gu[l], b_gu[l], w_down[l], b_down[l])
        last = l + 1 == depth
        x_ctx, x_lat = _residual(x1, moe_i, modr, norm_final if last else norm_ffn[l], normalize=last, t_ctx=t_ctx)
        new_states.append(st_ctx.reshape(n_ctx, 1, 2, N_HEADS, HEAD_DIM, D_STATE))

    y_prompt = x_ctx.reshape(n_ctx, seq_ctx, D_MODEL)
    y_sample = x_lat.reshape(n_lat, seq_lat, D_MODEL)
    new_state_ssm = new_states[0] if depth == 1 else jnp.concatenate(new_states, axis=1)
    return (y_prompt, y_sample, new_state_ssm)
```

```python
import functools

import jax
import jax.numpy as jnp
from jax import lax
from jax.experimental import pallas as pl
from jax.experimental.pallas import tpu as pltpu

F32 = jnp.float32
BF16 = jnp.bfloat16
U32 = jnp.uint32

D_MODEL = 1024
GRID_W = 64
D_INNER = 2 * D_MODEL
HEAD_DIM = 64
N_HEADS = D_INNER // HEAD_DIM
N_GROUPS = 8
HEADS_PER_GROUP = N_HEADS // N_GROUPS
D_STATE = 128
SSM_CONV = 5
CHUNK = 128
D_XBC = D_INNER + 2 * N_GROUPS * D_STATE
D_CONF = D_MODEL
CONF_KERNEL = 31
N_EXPERTS = 32
TOP_K = 4
D_EXPERT = D_MODEL
SWIGLU_LIMIT = 7.0
SWIGLU_ALPHA = 1.702
EPS = 1e-6

LANES = 128
UNIT = 256
GROUP_W = D_INNER // N_GROUPS
SSD_CHUNKS = 2
SSD_GROUP_UNROLL = 8
MIX_CHAINS = 2
MOE_TILE = 128
ROUTE_PAD = 2 * MOE_TILE
MOE_CAST_ROWS = 16
XW = D_MODEL // (2 * LANES)
AW = D_MODEL // LANES
RANK_SHIFT = 5
MIB = 1024 * 1024
V7X_VMEM_BYTES = 64 * MIB
VMEM_LIMIT = V7X_VMEM_BYTES - 2 * MIB

COL_Z = 0
COL_XBC = D_INNER
COL_GLU = D_XBC + D_INNER
COL_GATE = D_XBC + D_INNER + 2 * D_CONF
N_MAIN = D_XBC + D_INNER + 2 * D_CONF + 2 * D_MODEL


def _sigmoid(x):
    return 0.5 * jnp.tanh(0.5 * x) + 0.5


def _silu(x):
    h = 0.5 * x
    return h * jnp.tanh(h) + h


def _cparams(n_axes, vmem=None):
    return pltpu.CompilerParams(
        dimension_semantics=("arbitrary",) * n_axes,
        vmem_limit_bytes=vmem)


def _ada_kernel(c_ref, w_ref, b_ref, o_ref):
    s = _silu(c_ref[...])
    o_ref[...] = jnp.dot(s.astype(BF16), w_ref[...].astype(BF16),
                         preferred_element_type=F32) + b_ref[...]


def _ada_mod(cvec, w_ada, b_ada):
    n = w_ada.shape[1]
    tn = 1536
    return pl.pallas_call(
        _ada_kernel,
        grid=(n // tn,),
        in_specs=[pl.BlockSpec((8, D_MODEL), lambda j: (0, 0)),
                  pl.BlockSpec((D_MODEL, tn), lambda j: (0, j)),
                  pl.BlockSpec((1, tn), lambda j: (0, j))],
        out_specs=pl.BlockSpec((8, tn), lambda j: (0, j)),
        out_shape=jax.ShapeDtypeStruct((8, n), F32),
        compiler_params=_cparams(1, 40 * MIB),
        name="ada_mod",
    )(cvec, w_ada, b_ada.reshape(1, n))


def _wprep_kernel(w_ref, wn_ref, oa_ref, ob_ref, odt_ref, *, a_tiles):
    n = pl.program_id(0)
    tn = w_ref.shape[0]
    shift = 2 * N_HEADS

    def put(o_ref, c, rows_t):
        for j in range(D_MODEL // LANES):
            o_ref[j * LANES:(j + 1) * LANES, c * LANES:(c + 1) * LANES] = \
                rows_t[:, j * LANES:(j + 1) * LANES].T.astype(BF16)

    @pl.when(n < a_tiles)
    def _():
        for c in range(tn // LANES):
            put(oa_ref, c, w_ref[c * LANES:(c + 1) * LANES, :])

    @pl.when(n >= a_tiles)
    def _():
        for c in range(tn // LANES):
            lo = w_ref[c * LANES + shift:(c + 1) * LANES, :]
            hi = w_ref[(c + 1) * LANES:(c + 1) * LANES + shift, :] if c + 1 < tn // LANES else wn_ref[0:shift, :]
            put(ob_ref, c, jnp.concatenate([lo, hi], axis=0))

    @pl.when(n == a_tiles)
    def _():
        zeros = jnp.zeros((LANES - N_HEADS, D_MODEL), F32)
        for d in range(2):
            put(odt_ref, d, jnp.concatenate([w_ref[d * N_HEADS:(d + 1) * N_HEADS, :], zeros], axis=0))


def _in_proj_weights(w_in):
    tn = 2048
    n_a = D_INNER + D_XBC
    n_b = 2 * D_CONF + 2 * D_MODEL
    a_tiles = n_a // tn
    assert n_a % tn == 0 and n_b % tn == 0 and 2 * N_HEADS < LANES
    n_tiles = a_tiles + n_b // tn
    assert w_in.shape[1] == n_tiles * tn + 2 * N_HEADS
    w_t = w_in.T
    return pl.pallas_call(
        functools.partial(_wprep_kernel, a_tiles=a_tiles),
        grid=(n_tiles,),
        in_specs=[pl.BlockSpec((tn, D_MODEL), lambda n: (n, 0)),
                  pl.BlockSpec((LANES, D_MODEL), lambda n: ((n + 1) * (tn // LANES), 0))],
        out_specs=[pl.BlockSpec((D_MODEL, tn), lambda n: (0, jnp.minimum(n, a_tiles - 1))),
                   pl.BlockSpec((D_MODEL, tn), lambda n: (0, jnp.maximum(n - a_tiles, 0))),
                   pl.BlockSpec((D_MODEL, 2 * LANES), lambda n: (0, 0))],
        out_shape=[jax.ShapeDtypeStruct((D_MODEL, n_a), BF16),
                   jax.ShapeDtypeStruct((D_MODEL, n_b), BF16),
                   jax.ShapeDtypeStruct((D_MODEL, 2 * LANES), BF16)],
        compiler_params=_cparams(1, 48 * MIB),
        name="in_proj_weights",
    )(w_t, w_t)


def _inproj_kernel(xc_ref, xl_ref, nw_ref, sc_ref, sh_ref, wa_ref, wb_ref, wdt_ref, o_ref, dt_ref, h_scr, *,
                   ctx_tiles, a_tiles):
    @pl.when(pl.program_id(1) == 0)
    def _():
        xf = jnp.where(pl.program_id(0) < ctx_tiles, xc_ref[...], xl_ref[...])
        ms = jnp.mean(xf * xf, axis=-1, keepdims=True)
        hn = (xf * lax.rsqrt(ms + EPS)) * nw_ref[...]
        hn = hn * (1.0 + sc_ref[...]) + sh_ref[...]
        hb = hn.astype(BF16)
        h_scr[...] = hb
        dt_ref[...] = jnp.dot(hb, wdt_ref[...], preferred_element_type=F32)

    @pl.when(pl.program_id(1) < a_tiles)
    def _():
        o_ref[...] = jnp.dot(h_scr[...], wa_ref[...], preferred_element_type=F32).astype(BF16)

    @pl.when(pl.program_id(1) >= a_tiles)
    def _():
        o_ref[...] = jnp.dot(h_scr[...], wb_ref[...], preferred_element_type=F32).astype(BF16)


def _two_source_specs(tm, width, ctx_tiles, nargs):
    if nargs == 1:
        return [pl.BlockSpec((tm, width), lambda m: (jnp.minimum(m, ctx_tiles - 1), 0)),
                pl.BlockSpec((tm, width), lambda m: (jnp.maximum(m - ctx_tiles, 0), 0))]
    return [pl.BlockSpec((tm, width), lambda m, n: (jnp.minimum(m, ctx_tiles - 1), 0)),
            pl.BlockSpec((tm, width), lambda m, n: (jnp.maximum(m - ctx_tiles, 0), 0))]


def _in_proj(x_ctx, x_lat, norm_w, modr, w_a, w_b, w_dt):
    t_ctx = x_ctx.shape[0]
    t = t_ctx + x_lat.shape[0]
    tm, tn = 1024, 2048
    upt = tm // UNIT
    a_tiles = w_a.shape[1] // tn
    return pl.pallas_call(
        functools.partial(_inproj_kernel, ctx_tiles=t_ctx // tm, a_tiles=a_tiles),
        grid=(t // tm, N_MAIN // tn),
        in_specs=_two_source_specs(tm, D_MODEL, t_ctx // tm, 2) + [
                  pl.BlockSpec((1, D_MODEL), lambda m, n: (0, 0)),
                  pl.BlockSpec((None, None, 1, D_MODEL), lambda m, n: (m * upt, 1, 0, 0)),
                  pl.BlockSpec((None, None, 1, D_MODEL), lambda m, n: (m * upt, 0, 0, 0)),
                  pl.BlockSpec((D_MODEL, tn), lambda m, n: (0, jnp.minimum(n, a_tiles - 1))),
                  pl.BlockSpec((D_MODEL, tn), lambda m, n: (0, jnp.maximum(n - a_tiles, 0))),
                  pl.BlockSpec((D_MODEL, 2 * LANES), lambda m, n: (0, 0))],
        out_specs=[pl.BlockSpec((tm, tn), lambda m, n: (m, n)),
                   pl.BlockSpec((tm, 2 * LANES), lambda m, n: (m, 0))],
        out_shape=[jax.ShapeDtypeStruct((t, N_MAIN), BF16),
                   jax.ShapeDtypeStruct((t, 2 * LANES), F32)],
        scratch_shapes=[pltpu.VMEM((tm, D_MODEL), BF16)],
        compiler_params=_cparams(2, 48 * MIB),
        name="in_proj",
    )(x_ctx, x_lat, norm_w.reshape(1, D_MODEL), modr, modr, w_a, w_b, w_dt)


def _ssm_conv_kernel(x_ref, w_ref, b_ref, o_ref, pad_scr, *, seq):
    cb = x_ref.shape[1]
    pad_scr[0:8, :] = jnp.zeros((8, cb), F32)
    pad_scr[8 + seq:16 + seq, :] = jnp.zeros((8, cb), F32)
    pad_scr[8:8 + seq, :] = x_ref[...].astype(F32)
    half = SSM_CONV // 2
    rows, cw = 256, 512
    for c0 in range(0, cb, cw):
        cs = slice(c0, c0 + cw)
        for r0 in range(0, seq, rows):
            win = pad_scr[r0:r0 + rows + 16, cs]
            acc = jnp.broadcast_to(b_ref[:, cs], (rows, cw))
            for k in range(SSM_CONV):
                tap = win if k == half else pltpu.roll(win, (half - k) % (rows + 16), 0)
                acc = acc + w_ref[k:k + 1, cs] * tap[8:8 + rows]
            o_ref[r0:r0 + rows, cs] = _silu(acc).astype(BF16)


def _ssm_conv(proj, conv_w, conv_b, *, seq, nseq, row0):
    cb = 2048
    blk0 = row0 // seq
    return pl.pallas_call(
        functools.partial(_ssm_conv_kernel, seq=seq),
        grid=(nseq, D_XBC // cb),
        in_specs=[pl.BlockSpec((seq, cb), lambda s, j: (blk0 + s, COL_XBC // cb + j)),
                  pl.BlockSpec((SSM_CONV, cb), lambda s, j: (0, j)),
                  pl.BlockSpec((1, cb), lambda s, j: (0, j))],
        out_specs=pl.BlockSpec((seq, cb), lambda s, j: (s, j)),
        out_shape=jax.ShapeDtypeStruct((nseq * seq, D_XBC), BF16),
        scratch_shapes=[pltpu.VMEM((seq + 16, cb), F32)],
        compiler_params=_cparams(2, 40 * MIB),
        name="ssm_conv",
    )(proj, conv_w, conv_b.reshape(1, D_XBC))


def _split_bf16(v):
    hi = v.astype(BF16)
    lo = (v - hi.astype(F32)).astype(BF16)
    return jnp.concatenate([hi, lo], axis=1)


def _head_select_matrices():
    j = jnp.arange(2 * LANES, dtype=jnp.int32)[:, None] % LANES
    full = (j == (jnp.arange(N_HEADS * LANES, dtype=jnp.int32)[None, :] // LANES)).astype(BF16)
    exp = (j == (jnp.arange(D_INNER, dtype=jnp.int32)[None, :] // HEAD_DIM)).astype(BF16)
    return full, exp


def _ssd_kernel(*refs, nc, has_h0, write_state):
    (xs_ref, b_ref, c_ref, z_ref, dtr_ref, dtb_ref, alog_ref, dskip_ref, nw_ref, self_ref, sele_ref), rest = \
        refs[:11], refs[11:]
    if has_h0:
        h0_ref, rest = rest[0], rest[1:]
    y_ref, rest = rest[0], rest[1:]
    if write_state:
        st_ref, rest = rest[0], rest[1:]
    h_scr, ybuf, ychunk, colb_scr, wexp_scr, eexp_scr, texp_scr, rowq_scr = rest

    q = CHUNK
    phase = pl.program_id(1)
    c = pl.program_id(2)
    is_fwd = phase == 1
    c_eff = jnp.where(is_fwd, c, nc - 1 - c)

    @pl.when(c == 0)
    def _():
        if has_h0:
            for j in range(D_INNER // LANES):
                h_scr[:, j * LANES:(j + 1) * LANES] = h0_ref[j * LANES:(j + 1) * LANES, :].T
        else:
            h_scr[...] = jnp.zeros(h_scr.shape, F32)

    nrow = SSD_CHUNKS * q
    xdt = dtr_ref[...] + dtb_ref[...]
    dt = jnp.maximum(xdt, 0.0) + jnp.log(1.0 + jnp.exp(-jnp.abs(xdt)))
    a = -jnp.exp(alog_ref[...])
    adt = dt * a
    sgn = jnp.where(is_fwd, 1, -1)
    row = lax.broadcasted_iota(jnp.int32, (nrow, nrow), 0)
    col = lax.broadcasted_iota(jnp.int32, (nrow, nrow), 1)
    tri_blk = jnp.logical_and(row // q == col // q, (col - row) * sgn <= 0)
    cum2 = jnp.dot(tri_blk.astype(BF16), _split_bf16(adt), preferred_element_type=F32)
    cum = cum2[:, :LANES] + cum2[:, LANES:]
    tots = [jnp.where(is_fwd, cum[k * q + q - 1:k * q + q, :], cum[k * q:k * q + 1, :]) for k in range(SSD_CHUNKS)]
    tot_rows = jnp.concatenate([jnp.broadcast_to(t, (q, LANES)) for t in tots], axis=0)
    rowq = (cum - jnp.log(dt)).T
    for k in range(SSD_CHUNKS):
        rowq_scr[k] = rowq[:, k * q:(k + 1) * q]
    wdec = dt * jnp.exp(tot_rows - cum)
    eo = jnp.exp(cum)
    etot = jnp.concatenate([jnp.broadcast_to(jnp.exp(t), (8, LANES)) for t in tots], axis=0)
    colb_scr[...] = jnp.dot(_split_bf16(cum), self_ref[...], preferred_element_type=F32)
    expanded = jnp.dot(jnp.concatenate([_split_bf16(wdec), _split_bf16(eo), _split_bf16(etot)], axis=0),
                       sele_ref[...], preferred_element_type=F32)
    wexp_scr[...] = expanded[0:nrow]
    eexp_scr[...] = expanded[nrow:2 * nrow]
    for k in range(SSD_CHUNKS):
        texp_scr[k] = expanded[2 * nrow + 8 * k:2 * nrow + 8 * (k + 1)]
    lrow = lax.broadcasted_iota(jnp.int32, (q, q), 0)
    lcol = lax.broadcasted_iota(jnp.int32, (q, q), 1)
    tri = (lcol - lrow) * sgn <= 0
    lane_g = lax.broadcasted_iota(jnp.int32, (1, GROUP_W), 1)
    head_mask = [(lane_g // HEAD_DIM == hh).astype(BF16) for hh in range(HEADS_PER_GROUP)]
    neg_inf = jnp.float32(-jnp.inf)

    def group_body(g, carry):
        gs = pl.ds(pl.multiple_of(g * GROUP_W, GROUP_W), GROUP_W)
        ns = pl.ds(pl.multiple_of(g * D_STATE, D_STATE), D_STATE)
        for k in range(SSD_CHUNKS):
            ci = jnp.where(is_fwd, k, SSD_CHUNKS - 1 - k)
            r0 = pl.multiple_of(ci * q, q)
            rs = pl.ds(r0, q)
            bg = b_ref[rs, ns]
            cg = c_ref[rs, ns]
            scores = lax.dot_general(cg, bg, (((1,), (1,)), ((), ())), preferred_element_type=F32)
            xs_g = xs_ref[rs, gs]
            ms, xb = [], []
            for hh in range(HEADS_PER_GROUP):
                h = HEADS_PER_GROUP * g + hh
                colb = colb_scr[rs, pl.ds(pl.multiple_of(h * LANES, LANES), LANES)]
                seg = jnp.where(tri, colb - rowq_scr[ci, pl.ds(h, 1), :], neg_inf)
                ms.append((scores * jnp.exp(seg)).astype(BF16))
                xb.append(xs_g * head_mask[hh])
            y_diag = jnp.dot(jnp.concatenate(ms, axis=1), jnp.concatenate(xb, axis=0),
                             preferred_element_type=F32)
            xd = (xs_g.astype(F32) * wexp_scr[rs, gs]).astype(BF16)
            bt = bg.astype(F32).T.astype(BF16)
            st = jnp.dot(bt, xd, preferred_element_type=F32)
            hg = h_scr[:, gs]
            y_off = jnp.dot(cg, hg.astype(BF16), preferred_element_type=F32) * eexp_scr[rs, gs]
            ychunk[rs, gs] = y_diag + y_off
            h_scr[:, gs] = hg * texp_scr[ci, 0:1, gs] + st
        return carry

    lax.fori_loop(0, N_GROUPS, group_body, 0, unroll=SSD_GROUP_UNROLL)

    rows = pl.ds(pl.multiple_of(c_eff * nrow, nrow), nrow)

    @pl.when(jnp.logical_not(is_fwd))
    def _():
        ybuf[rows, :] = ychunk[...]

    @pl.when(is_fwd)
    def _():
        zf = z_ref[...].astype(F32)
        yt = ychunk[...] + ybuf[rows, :] + xs_ref[...].astype(F32) * dskip_ref[...]
        yz = yt * _silu(zf)
        for g in range(N_GROUPS):
            gs = slice(g * GROUP_W, (g + 1) * GROUP_W)
            blk = yz[:, gs]
            ms = jnp.mean(blk * blk, axis=-1, keepdims=True)
            y_ref[:, gs] = (blk * lax.rsqrt(ms + EPS) * nw_ref[:, gs]).astype(BF16)

    if write_state:
        @pl.when(c == nc - 1)
        def _():
            for j in range(D_INNER // LANES):
                st_ref[j * LANES:(j + 1) * LANES, :] = h_scr[:, j * LANES:(j + 1) * LANES].T


def _ssd(xbc_c, proj, dt_raw, dt_bias, a_log, d_skip, norm_w, h0t, *, seq, nseq, row0, write_state):
    rb = SSD_CHUNKS * CHUNK
    nc = seq // rb
    blk0 = row0 // rb
    has_h0 = h0t is not None
    assert seq % rb == 0 and row0 % rb == 0

    def tok(s, p, c):
        return s * nc + p * c + (1 - p) * (nc - 1 - c)

    in_specs = [
        pl.BlockSpec((rb, D_INNER), lambda s, p, c: (tok(s, p, c), 0)),
        pl.BlockSpec((rb, N_GROUPS * D_STATE), lambda s, p, c: (tok(s, p, c), 2)),
        pl.BlockSpec((rb, N_GROUPS * D_STATE), lambda s, p, c: (tok(s, p, c), 3)),
        pl.BlockSpec((rb, D_INNER), lambda s, p, c: (blk0 + s * nc + p * c, COL_Z // D_INNER)),
        pl.BlockSpec((rb, LANES), lambda s, p, c: (blk0 + tok(s, p, c), 1 - p)),
        pl.BlockSpec((None, 1, LANES), lambda s, p, c: (1 - p, 0, 0)),
        pl.BlockSpec((None, 1, LANES), lambda s, p, c: (1 - p, 0, 0)),
        pl.BlockSpec((1, D_INNER), lambda s, p, c: (0, 0)),
        pl.BlockSpec((1, D_INNER), lambda s, p, c: (0, 0)),
        pl.BlockSpec((2 * LANES, N_HEADS * LANES), lambda s, p, c: (0, 0)),
        pl.BlockSpec((2 * LANES, D_INNER), lambda s, p, c: (0, 0)),
    ]
    sel_full, sel_exp = _head_select_matrices()
    args = [xbc_c, xbc_c, xbc_c, proj, dt_raw, dt_bias, a_log, d_skip, norm_w, sel_full, sel_exp]
    if has_h0:
        in_specs.append(pl.BlockSpec((None, None, D_INNER, D_STATE), lambda s, p, c: (s, 1 - p, 0, 0)))
        args.append(h0t)
    out_specs = [pl.BlockSpec((rb, D_INNER), lambda s, p, c: (s * nc + p * c, 0))]
    out_shape = [jax.ShapeDtypeStruct((nseq * seq, D_INNER), BF16)]
    if write_state:
        out_specs.append(pl.BlockSpec((None, None, D_INNER, D_STATE), lambda s, p, c: (s, 1 - p, 0, 0)))
        out_shape.append(jax.ShapeDtypeStruct((nseq, 2, D_INNER, D_STATE), F32))
    return pl.pallas_call(
        functools.partial(_ssd_kernel, nc=nc, has_h0=has_h0, write_state=write_state),
        grid=(nseq, 2, nc),
        in_specs=in_specs,
        out_specs=out_specs,
        out_shape=out_shape,
        scratch_shapes=[pltpu.VMEM((D_STATE, D_INNER), F32),
                        pltpu.VMEM((seq, D_INNER), F32),
                        pltpu.VMEM((rb, D_INNER), F32),
                        pltpu.VMEM((rb, N_HEADS * LANES), F32),
                        pltpu.VMEM((rb, D_INNER), F32),
                        pltpu.VMEM((rb, D_INNER), F32),
                        pltpu.VMEM((SSD_CHUNKS, 8, D_INNER), F32),
                        pltpu.VMEM((SSD_CHUNKS, LANES, CHUNK), F32)],
        compiler_params=_cparams(3, 48 * MIB),
        name="ssd_scan",
    )(*args)


def _conf_kernel(glu_ref, w_ref, b_ref, lnw_ref, lnb_ref, o_ref, pad_scr, sh_scr, conv_scr, *, seg):
    rows = glu_ref.shape[0]
    nseg = rows // seg
    half = CONF_KERNEL // 2
    front = 16
    span = seg + 24
    rc = 32
    for i in range(nseg):
        pad_scr[i, 0:front, :] = jnp.zeros((front, D_CONF), F32)
        pad_scr[i, front + seg:front + seg + 16, :] = jnp.zeros((16, D_CONF), F32)
        for r0 in range(0, seg, rc):
            g = glu_ref[i * seg + r0:i * seg + r0 + rc, :]
            pad_scr[i, front + r0:front + r0 + rc, :] = \
                g[:, :D_CONF].astype(F32) * _sigmoid(g[:, D_CONF:].astype(F32))
    padded = seg + 32
    for i in range(nseg):
        for cbi in range(D_CONF // LANES):
            cs = slice(cbi * LANES, (cbi + 1) * LANES)
            seg_pad = pad_scr[i, :, cs]
            for s in range(8):
                sh_scr[s, i, :, cs] = (seg_pad if s == 0 else pltpu.roll(seg_pad, padded - s, 0))[0:span]
    rb = 64
    blocks = [(i, r0) for i in range(nseg) for r0 in range(0, seg, rb)]
    for cbi in range(D_CONF // LANES):
        cs = slice(cbi * LANES, (cbi + 1) * LANES)
        accs = [jnp.broadcast_to(b_ref[:, cs], (rb, LANES)) for _ in blocks]
        for k in range(CONF_KERNEL):
            wk = w_ref[k:k + 1, cs]
            for bi, (i, r0) in enumerate(blocks):
                start = front + r0 + k - half
                al = start - start % 8
                accs[bi] = accs[bi] + wk * sh_scr[start % 8, i, al:al + rb, cs]
        for bi, (i, r0) in enumerate(blocks):
            conv_scr[i * seg + r0:i * seg + r0 + rb, cs] = accs[bi]
    for r0 in range(0, rows, rc):
        v = conv_scr[r0:r0 + rc, :]
        mu = jnp.mean(v, axis=-1, keepdims=True)
        vc = v - mu
        var = jnp.mean(vc * vc, axis=-1, keepdims=True)
        ln = (vc * lax.rsqrt(var + EPS)) * lnw_ref[...] + lnb_ref[...]
        o_ref[r0:r0 + rc, :] = _silu(ln).astype(BF16)


def _conformer(proj, dw_w, dw_b, ln_w, ln_b, *, seg, ntok, row0):
    rows = UNIT
    blk0 = row0 // rows
    return pl.pallas_call(
        functools.partial(_conf_kernel, seg=seg),
        grid=(ntok // rows,),
        in_specs=[pl.BlockSpec((rows, 2 * D_CONF), lambda i: (blk0 + i, COL_GLU // (2 * D_CONF))),
                  pl.BlockSpec((CONF_KERNEL, D_CONF), lambda i: (0, 0)),
                  pl.BlockSpec((1, D_CONF), lambda i: (0, 0)),
                  pl.BlockSpec((1, D_CONF), lambda i: (0, 0)),
                  pl.BlockSpec((1, D_CONF), lambda i: (0, 0))],
        out_specs=pl.BlockSpec((rows, D_CONF), lambda i: (i, 0)),
        out_shape=jax.ShapeDtypeStruct((ntok, D_CONF), BF16),
        scratch_shapes=[pltpu.VMEM((rows // seg, seg + 32, D_CONF), F32),
                        pltpu.VMEM((8, rows // seg, seg + 24, D_CONF), F32),
                        pltpu.VMEM((rows, D_CONF), F32)],
        compiler_params=_cparams(1, 40 * MIB),
        name="conformer_conv",
    )(proj, dw_w, dw_b.reshape(1, D_CONF), ln_w.reshape(1, D_CONF), ln_b.reshape(1, D_CONF))


def _pack_halves(x):
    outs = []
    for cb in range(x.shape[1] // (2 * LANES)):
        hi = x[:, cb * 2 * LANES:cb * 2 * LANES + LANES].astype(BF16).astype(F32)
        lo = x[:, cb * 2 * LANES + LANES:(cb + 1) * 2 * LANES].astype(BF16).astype(F32)
        hw = lax.bitcast_convert_type(hi, U32)
        lw = jnp.right_shift(lax.bitcast_convert_type(lo, U32), jnp.uint32(16))
        outs.append(jnp.bitwise_or(hw, lw))
    return jnp.concatenate(outs, axis=1)


def _mix_kernel(xc_ref, xl_ref, yc_ref, yl_ref, uc_ref, ul_ref, gate_ref, g1_ref, sc2_ref, sh2_ref,
                wssd_ref, wconf_ref, bconf_ref, bgate_ref, wo_ref, nffn_ref, wr_ref, br_ref,
                x1_ref, h2i_ref, er_ref, topp_ref, cnt_ref, cnt_scr, *, ctx_tiles):
    @pl.when(pl.program_id(0) == 0)
    def _():
        cnt_scr[...] = jnp.zeros(cnt_scr.shape, F32)

    is_ctx = pl.program_id(0) < ctx_tiles
    rows = x1_ref.shape[0]
    sub = rows // MIX_CHAINS
    lane = lax.broadcasted_iota(jnp.int32, (sub, LANES), 1)
    lane_f = lane.astype(F32)
    neg_inf = jnp.float32(-jnp.inf)

    def chain(c):
        rs = slice(c * sub, (c + 1) * sub)
        y_in = jnp.where(is_ctx, yc_ref[rs, :], yl_ref[rs, :])
        u_in = jnp.where(is_ctx, uc_ref[rs, :], ul_ref[rs, :])
        o_ssd = jnp.dot(y_in, wssd_ref[...], preferred_element_type=F32)
        o_conf = jnp.dot(u_in, wconf_ref[...], preferred_element_type=F32) + bconf_ref[...]
        gates = _sigmoid(gate_ref[rs, :].astype(F32) + bgate_ref[...])
        merged = gates[:, :D_MODEL] * o_ssd + gates[:, D_MODEL:] * o_conf
        out = jnp.dot(merged.astype(BF16), wo_ref[...], preferred_element_type=F32)
        x1 = jnp.where(is_ctx, xc_ref[rs, :], xl_ref[rs, :]) + g1_ref[...] * out
        x1_ref[rs, :] = x1
        ms = jnp.mean(x1 * x1, axis=-1, keepdims=True)
        h2 = (x1 * lax.rsqrt(ms + EPS)) * nffn_ref[...]
        h2 = h2 * (1.0 + sc2_ref[...]) + sh2_ref[...]
        packed = _pack_halves(h2)
        for qd in range(XW):
            h2i_ref[pl.ds(c * sub * XW + qd, sub, stride=XW), :] = packed[:, qd * LANES:(qd + 1) * LANES]
        h_hi = h2.astype(BF16)
        h_lo = (h2 - h_hi.astype(F32)).astype(BF16)
        logits = jnp.dot(jnp.concatenate([h_hi, h_hi, h_lo], axis=1), wr_ref[...],
                         preferred_element_type=F32) + br_ref[...]
        work = jnp.where(lane < N_EXPERTS, logits, neg_inf)
        vals, idxs = [], []
        for _ in range(TOP_K):
            m = jnp.max(work, axis=-1, keepdims=True)
            idx = jnp.min(jnp.where(work == m, lane_f, jnp.float32(LANES)), axis=-1, keepdims=True)
            vals.append(m)
            idxs.append(idx)
            work = jnp.where(lane_f == idx, neg_inf, work)
        es = [jnp.exp(v - vals[0]) for v in vals]
        denom = es[0] + es[1] + es[2] + es[3]
        member = jnp.zeros((sub, LANES), F32)
        topp = jnp.zeros((sub, LANES), F32)
        for k in range(TOP_K):
            member = member + jnp.where(lane_f == idxs[k], 1.0, 0.0)
            topp = jnp.where(lane == k, es[k] / denom, topp)
        topp_ref[rs, :] = topp
        return idxs, member

    results = [chain(c) for c in range(MIX_CHAINS)]
    member = jnp.concatenate([m for _, m in results], axis=0)
    r_i = lax.broadcasted_iota(jnp.int32, (rows, rows), 0)
    c_i = lax.broadcasted_iota(jnp.int32, (rows, rows), 1)
    earlier = jnp.where(c_i < r_i, 1.0, 0.0).astype(BF16)
    rank = jnp.dot(earlier, member.astype(BF16), preferred_element_type=F32) + cnt_scr[...]
    cnt = cnt_scr[...] + jnp.sum(member, axis=0, keepdims=True)
    cnt_scr[...] = cnt
    cnt_ref[...] = jnp.broadcast_to(cnt, cnt_ref.shape).astype(jnp.int32)
    for c, (idxs, _) in enumerate(results):
        rs = slice(c * sub, (c + 1) * sub)
        er = jnp.zeros((sub, LANES), F32)
        for k in range(TOP_K):
            rank_k = jnp.sum(jnp.where(lane_f == idxs[k], rank[rs, :], 0.0), axis=-1, keepdims=True)
            er = jnp.where(lane == k, idxs[k] + N_EXPERTS * rank_k, er)
        er_ref[rs, :] = er.astype(jnp.int32)


def _mix_out(x_ctx, x_lat, y_ctx, y_lat, u_ctx, u_lat, proj, modr, w_ssd, w_conf, b_conf, b_gate, w_o, norm_ffn,
             w_router3, b_router):
    t_ctx = x_ctx.shape[0]
    t = t_ctx + x_lat.shape[0]
    tm = 512
    upt = tm // UNIT
    ctx_tiles = t_ctx // tm
    full = lambda shape: pl.BlockSpec(shape, lambda m: (0,) * len(shape), pipeline_mode=pl.Buffered(1))
    mod = lambda which: pl.BlockSpec((None, None, 1, D_MODEL), lambda m: (m * upt, which, 0, 0))
    return pl.pallas_call(
        functools.partial(_mix_kernel, ctx_tiles=ctx_tiles),
        grid=(t // tm,),
        in_specs=_two_source_specs(tm, D_MODEL, ctx_tiles, 1) + _two_source_specs(tm, D_INNER, ctx_tiles, 1)
                 + _two_source_specs(tm, D_CONF, ctx_tiles, 1) + [
                  pl.BlockSpec((tm, 2 * D_MODEL), lambda m: (m, COL_GATE // (2 * D_MODEL))),
                  mod(2), mod(4), mod(3),
                  full((D_INNER, D_MODEL)), full((D_CONF, D_MODEL)), full((1, D_MODEL)),
                  full((1, 2 * D_MODEL)), full((D_MODEL, D_MODEL)), full((1, D_MODEL)),
                  full((3 * D_MODEL, LANES)), full((1, LANES))],
        out_specs=[pl.BlockSpec((tm, D_MODEL), lambda m: (m, 0)),
                   pl.BlockSpec((tm * XW, LANES), lambda m: (m, 0)),
                   pl.BlockSpec((tm, LANES), lambda m: (m, 0)),
                   pl.BlockSpec((tm, LANES), lambda m: (m, 0)),
                   pl.BlockSpec((8, LANES), lambda m: (0, 0))],
        out_shape=[jax.ShapeDtypeStruct((t, D_MODEL), F32),
                   jax.ShapeDtypeStruct((t * XW, LANES), U32),
                   jax.ShapeDtypeStruct((t, LANES), jnp.int32),
                   jax.ShapeDtypeStruct((t, LANES), F32),
                   jax.ShapeDtypeStruct((8, LANES), jnp.int32)],
        scratch_shapes=[pltpu.VMEM((1, LANES), F32)],
        compiler_params=_cparams(1, 56 * MIB),
        name="mix_out_router",
    )(x_ctx, x_lat, y_ctx, y_lat, u_ctx, u_lat, proj, modr, modr, modr, w_ssd, w_conf, b_conf.reshape(1, D_MODEL),
      b_gate.reshape(1, 2 * D_MODEL), w_o, norm_ffn.reshape(1, D_MODEL), w_router3, b_router)


def _moe_kernel(off_ref, pair_ref, pb_ref, h2i_ref, wgu_hbm, wd_hbm, bgu_ref, bd_ref, out_ref,
                acc, x_scr, y_scr, stage_gu, stage_d, wgu_b, wd_b, wsem, osem):
    e = pl.program_id(0)
    tr = MOE_TILE

    def weight_copies(ex):
        return (pltpu.make_async_copy(wgu_hbm.at[ex], stage_gu, wsem.at[0]),
                pltpu.make_async_copy(wd_hbm.at[ex], stage_d, wsem.at[1]))

    @pl.when(e == 0)
    def _():
        acc[...] = jnp.zeros(acc.shape, F32)
        for cp in weight_copies(0):
            cp.start()

    for cp in weight_copies(e):
        cp.wait()

    start = off_ref[e]
    end = off_ref[e + 1]
    ntiles = (end - start + tr - 1) // tr
    bgu = bgu_ref[...]
    bd = bd_ref[...]
    cast_steps = D_MODEL // MOE_CAST_ROWS
    rows_per_step = tr // cast_steps

    def to_bf16(i, c):
        rows = pl.ds(pl.multiple_of(i * MOE_CAST_ROWS, MOE_CAST_ROWS), MOE_CAST_ROWS)
        wgu_b[rows, :] = stage_gu[rows, :].astype(BF16)
        wd_b[rows, :] = stage_d[rows, :].astype(BF16)
        for rr in range(rows_per_step):
            r = i * rows_per_step + rr
            src = jnp.bitwise_and(pair_ref[start + r], -TOP_K)
            x_scr[0, pl.ds(pl.multiple_of(r * XW, XW), XW), :] = h2i_ref[pl.ds(pl.multiple_of(src, XW), XW), :]
        return c

    lax.fori_loop(0, cast_steps, to_bf16, 0, unroll=2)

    @pl.when(e + 1 < N_EXPERTS)
    def _():
        for cp in weight_copies(e + 1):
            cp.start()

    def gather_rows(base, slot, r0, n):
        for r in range(r0, r0 + n):
            src = jnp.bitwise_and(pair_ref[base + r], -TOP_K)
            x_scr[slot, pl.ds(r * XW, XW), :] = h2i_ref[pl.ds(pl.multiple_of(src, XW), XW), :]

    def scatter_rows(base, slot, r0):
        dsts, ps = [], []
        for rr in range(8):
            pair = pair_ref[base + r0 + rr]
            dsts.append(pl.multiple_of(jnp.bitwise_and(pair, -TOP_K) * (AW // XW), AW))
            ps.append(lax.bitcast_convert_type(pb_ref[pair], F32))
        olds = [acc[pl.ds(dsts[rr], AW), :] for rr in range(8)]
        news = [olds[rr] + ps[rr] * y_scr[slot, pl.ds((r0 + rr) * AW, AW), :] for rr in range(8)]
        for rr in range(8):
            acc[pl.ds(dsts[rr], AW), :] = news[rr]

    def tile_step(ti, first):
        slot = jnp.bitwise_and(ti, 1)
        other = 1 - slot
        base = start + ti * tr
        parts = []
        for qd in range(XW):
            w = x_scr[slot, pl.ds(qd, tr, stride=XW), :]
            parts.append(lax.bitcast_convert_type(jnp.bitwise_and(w, jnp.uint32(0xFFFF0000)), F32))
            parts.append(lax.bitcast_convert_type(jnp.left_shift(w, jnp.uint32(16)), F32))
        x = jnp.concatenate(parts, axis=1).astype(BF16)
        if not first:
            for r0 in range(0, tr, 8):
                scatter_rows(base - tr, other, r0)
        gather_rows(base + tr, other, 0, tr)
        gu = jnp.dot(x, wgu_b[...], preferred_element_type=F32) + bgu
        g = jnp.minimum(gu[:, :D_EXPERT], SWIGLU_LIMIT)
        u = jnp.clip(gu[:, D_EXPERT:], -SWIGLU_LIMIT, SWIGLU_LIMIT)
        act = (u + 1.0) * g * _sigmoid(SWIGLU_ALPHA * g)
        y = jnp.dot(act.astype(BF16), wd_b[...], preferred_element_type=F32) + bd
        for qd in range(AW):
            y_scr[slot, pl.ds(qd, tr, stride=AW), :] = y[:, qd * LANES:(qd + 1) * LANES]

    @pl.when(ntiles > 0)
    def _():
        tile_step(0, True)

    def tile_body(ti, carry):
        tile_step(ti, False)
        return carry

    lax.fori_loop(1, ntiles, tile_body, 0)

    @pl.when(ntiles > 0)
    def _():
        last = ntiles - 1
        lslot = jnp.bitwise_and(last, 1)
        lbase = start + last * tr

        def last_scatter(r8, c2):
            dsts, ps = [], []
            for rr in range(8):
                r = lbase + r8 * 8 + rr
                pair = pair_ref[r]
                dsts.append(pl.multiple_of(jnp.bitwise_and(pair, -TOP_K) * (AW // XW), AW))
                pw = lax.bitcast_convert_type(pb_ref[pair], F32)
                ps.append(jnp.where(r < end, pw, 0.0))
            olds = [acc[pl.ds(dsts[rr], AW), :] for rr in range(8)]
            news = [olds[rr] + ps[rr] * y_scr[lslot, pl.ds(pl.multiple_of((r8 * 8 + rr) * AW, AW), AW), :]
                    for rr in range(8)]
            for rr in range(8):
                acc[pl.ds(dsts[rr], AW), :] = news[rr]
            return c2

        lax.fori_loop(0, tr // 8, last_scatter, 0)

    @pl.when(e == N_EXPERTS - 1)
    def _():
        cp = pltpu.make_async_copy(acc, out_ref, osem)
        cp.start()
        cp.wait()


def _moe_grouped(h2i, off, row_pair, pbits, w_gu, b_gu, w_down, b_down):
    t = h2i.shape[0] // XW
    assert D_EXPERT == D_MODEL
    grid_spec = pltpu.PrefetchScalarGridSpec(
        num_scalar_prefetch=3,
        grid=(N_EXPERTS,),
        in_specs=[pl.BlockSpec((t * XW, LANES), lambda e, *_: (0, 0), pipeline_mode=pl.Buffered(1)),
                  pl.BlockSpec(memory_space=pl.ANY),
                  pl.BlockSpec(memory_space=pl.ANY),
                  pl.BlockSpec((None, 1, 2 * D_EXPERT), lambda e, *_: (e, 0, 0)),
                  pl.BlockSpec((None, 1, D_MODEL), lambda e, *_: (e, 0, 0))],
        out_specs=pl.BlockSpec(memory_space=pl.ANY),
        scratch_shapes=[pltpu.VMEM((t * AW, LANES), F32),
                        pltpu.VMEM((2, MOE_TILE * XW, LANES), U32),
                        pltpu.VMEM((2, MOE_TILE * AW, LANES), F32),
                        pltpu.VMEM((D_MODEL, 2 * D_EXPERT), F32),
                        pltpu.VMEM((D_EXPERT, D_MODEL), F32),
                        pltpu.VMEM((D_MODEL, 2 * D_EXPERT), BF16),
                        pltpu.VMEM((D_EXPERT, D_MODEL), BF16),
                        pltpu.SemaphoreType.DMA((2,)),
                        pltpu.SemaphoreType.DMA(())])
    return pl.pallas_call(
        _moe_kernel,
        grid_spec=grid_spec,
        out_shape=jax.ShapeDtypeStruct((t * AW, LANES), F32),
        compiler_params=_cparams(1, VMEM_LIMIT),
        name="moe_grouped",
    )(off, row_pair, pbits, h2i, w_gu, w_down, b_gu.reshape(N_EXPERTS, 1, 2 * D_EXPERT),
      b_down.reshape(N_EXPERTS, 1, D_MODEL))


def _residual_kernel(x1_ref, moe_ref, g2_ref, nw_ref, oc_ref, ol_ref, *, normalize, ctx_tiles):
    tm = x1_ref.shape[0]
    moe = jnp.concatenate([moe_ref[pl.ds(qd, tm, stride=AW), :] for qd in range(AW)], axis=1)
    x2 = x1_ref[...] + g2_ref[...] * moe
    if normalize:
        ms = jnp.mean(x2 * x2, axis=-1, keepdims=True)
        x2 = (x2 * lax.rsqrt(ms + EPS)) * nw_ref[...]

    @pl.when(pl.program_id(0) < ctx_tiles)
    def _():
        oc_ref[...] = x2

    @pl.when(pl.program_id(0) >= ctx_tiles)
    def _():
        ol_ref[...] = x2


def _residual(x1, moe_i, modr, norm_w, *, normalize, t_ctx):
    t = x1.shape[0]
    tm = 512
    upt = tm // UNIT
    ctx_tiles = t_ctx // tm
    return pl.pallas_call(
        functools.partial(_residual_kernel, normalize=normalize, ctx_tiles=ctx_tiles),
        grid=(t // tm,),
        in_specs=[pl.BlockSpec((tm, D_MODEL), lambda m: (m, 0)),
                  pl.BlockSpec((tm * AW, LANES), lambda m: (m, 0)),
                  pl.BlockSpec((None, None, 1, D_MODEL), lambda m: (m * upt, 5, 0, 0)),
                  pl.BlockSpec((1, D_MODEL), lambda m: (0, 0))],
        out_specs=_two_source_specs(tm, D_MODEL, ctx_tiles, 1),
        out_shape=[jax.ShapeDtypeStruct((t_ctx, D_MODEL), F32),
                   jax.ShapeDtypeStruct((t - t_ctx, D_MODEL), F32)],
        compiler_params=_cparams(1),
        name="ffn_residual_norm",
    )(x1, moe_i, modr, norm_w.reshape(1, D_MODEL))


def _route_kernel(pos_ref, pair_ref, *, n_pairs):
    def pad(i, c):
        pair_ref[n_pairs + i] = 0
        return c

    lax.fori_loop(0, ROUTE_PAD, pad, 0)

    def place(i8, c):
        for ii in range(8):
            i = i8 * 8 + ii
            pair_ref[pos_ref[i]] = i
        return c

    lax.fori_loop(0, n_pairs // 8, place, 0)


def _route(er_flat, cnt):
    n = er_flat.shape[0]
    off_end = jnp.cumsum(cnt)
    off = off_end - cnt
    expert = jnp.bitwise_and(er_flat, N_EXPERTS - 1)
    onehot = expert[:, None] == jnp.arange(N_EXPERTS, dtype=jnp.int32)[None, :]
    pos = lax.shift_right_logical(er_flat, RANK_SHIFT) + jnp.sum(jnp.where(onehot, off[None, :], 0), axis=1)
    smem = pl.BlockSpec(memory_space=pltpu.SMEM)
    row_pair = pl.pallas_call(
        functools.partial(_route_kernel, n_pairs=n),
        in_specs=[smem],
        out_specs=smem,
        out_shape=jax.ShapeDtypeStruct((n + ROUTE_PAD,), jnp.int32),
        name="moe_route",
    )(pos)
    return jnp.concatenate([off, jnp.broadcast_to(off_end[-1:], (8,))]), row_pair


def kernel(x_prompt, x_sample, state_ssm, c, c_ctx, w_ada, b_ada, norm_mix, norm_ffn, w_in, ssm_conv_w, ssm_conv_b, dt_bias, a_log, d_skip, ssm_norm_w, w_ssd_out, conf_dw_w, conf_dw_b, conf_ln_w, conf_ln_b, w_conf_out, b_conf_out, b_gate, w_o, w_router, b_router, w_gu, b_gu, w_down, b_down, norm_final):
    n_ctx, seq_ctx, _ = x_prompt.shape
    n_lat, seq_lat, _ = x_sample.shape
    depth = w_in.shape[0]
    t_ctx, t_lat = n_ctx * seq_ctx, n_lat * seq_lat
    t_all = t_ctx + t_lat
    assert seq_ctx == UNIT and seq_lat % 1024 == 0 and t_ctx % 1024 == 0
    assert n_lat + 1 <= 8 and seq_lat % GRID_W == 0 and UNIT % GRID_W == 0

    x_ctx, x_lat = x_prompt.reshape(t_ctx, D_MODEL), x_sample.reshape(t_lat, D_MODEL)
    cvec =jnp.concatenate([c_ctx[None, :], c, jnp.zeros((8 - 1 - n_lat, D_MODEL), F32)], axis=0)
    unit_row = jnp.concatenate([jnp.zeros((t_ctx // UNIT,), jnp.int32),
                                1 + jnp.arange(t_lat // UNIT, dtype=jnp.int32) // (seq_lat // UNIT)])
    assert XW == TOP_K and N_EXPERTS == 1 << RANK_SHIFT

    new_states = []
    for l in range(depth):
        mod = _ada_mod(cvec, w_ada[l], b_ada[l])
        modr = mod[unit_row].reshape(t_all // UNIT, 6, 1, D_MODEL)

        w_a, w_b, w_dt = _in_proj_weights(w_in[l])
        proj, dt_raw = _in_proj(x_ctx, x_lat, norm_mix[l], modr, w_a, w_b, w_dt)

        pad_h = lambda v: jnp.pad(v.reshape(2, 1, N_HEADS), ((0, 0), (0, 0), (0, LANES - N_HEADS)))
        dtb, alog = pad_h(dt_bias[l]), pad_h(a_log[l])
        dskip = jnp.repeat(d_skip[l], HEAD_DIM).reshape(1, D_INNER)
        nw = ssm_norm_w[l].reshape(1, D_INNER)
        h0t = state_ssm[:, l].astype(F32).reshape(n_lat, 2, D_INNER, D_STATE)

        xbc_ctx = _ssm_conv(proj, ssm_conv_w[l], ssm_conv_b[l], seq=seq_ctx, nseq=n_ctx, row0=0)
        xbc_lat = _ssm_conv(proj, ssm_conv_w[l], ssm_conv_b[l], seq=seq_lat, nseq=n_lat, row0=t_ctx)
        y_ctx, st_ctx = _ssd(xbc_ctx, proj, dt_raw, dtb, alog, dskip, nw, None,
                             seq=seq_ctx, nseq=n_ctx, row0=0, write_state=True)
        (y_lat,) = _ssd(xbc_lat, proj, dt_raw, dtb, alog, dskip, nw, h0t,
                        seq=seq_lat, nseq=n_lat, row0=t_ctx, write_state=False)
        u_ctx = _conformer(proj, conf_dw_w[l], conf_dw_b[l], conf_ln_w[l], conf_ln_b[l],
                           seg=seq_ctx, ntok=t_ctx, row0=0)
        u_lat = _conformer(proj, conf_dw_w[l], conf_dw_b[l], conf_ln_w[l], conf_ln_b[l],
                           seg=GRID_W, ntok=t_lat, row0=t_ctx)
        wr = jnp.pad(w_router[l], ((0, 0), (0, LANES - N_EXPERTS)))
        wr_hi = wr.astype(BF16)
        wr_lo = (wr - wr_hi.astype(F32)).astype(BF16)
        br = jnp.pad(b_router[l], (0, LANES - N_EXPERTS)).reshape(1, LANES)
        x1, h2i, er, topp, cnt = _mix_out(
            x_ctx, x_lat, y_ctx, y_lat, u_ctx, u_lat, proj, modr, w_ssd_out[l].astype(BF16),
            w_conf_out[l].astype(BF16), b_conf_out[l], b_gate[l], w_o[l].astype(BF16), norm_ffn[l],
            jnp.concatenate([wr_hi, wr_lo, wr_hi], axis=0), br)

        off, row_pair = _route(er[:, :TOP_K].reshape(-1), cnt[0, :N_EXPERTS])
        pbits = lax.bitcast_convert_type(topp[:, :TOP_K], jnp.int32).reshape(-1)
        moe_i = _moe_grouped(h2i, off, row_pair, pbits, w_gu[l], b_gu[l], w_down[l], b_down[l])
        last = l + 1 == depth
        x_ctx, x_lat = _residual(x1, moe_i, modr, norm_final if last else norm_ffn[l], normalize=last, t_ctx=t_ctx)
        new_states.append(st_ctx.reshape(n_ctx, 1, 2, N_HEADS, HEAD_DIM, D_STATE))

    y_prompt = x_ctx.reshape(n_ctx, seq_ctx, D_MODEL)
    y_sample = x_lat.reshape(n_lat, seq_lat, D_MODEL)
    new_state_ssm = new_states[0] if depth == 1 else jnp.concatenate(new_states, axis=1)
    return (y_prompt, y_sample, new_state_ssm)
```
